```python
import jax
import jax.numpy as jnp
from jax import lax
import numpy as np

D_MODEL = 1024
BATCH = 4
SEQ = 8192
DEPTH = 1

HEAD_DIM = 64
RWKV_WIDTH = D_MODEL // 2
RWKV_HEADS = RWKV_WIDTH // HEAD_DIM
DECAY_LORA = 64
ICLR_LORA = 64
RWKV_GN_EPS = 64e-5
NSA_WIDTH = D_MODEL // 4
NSA_HEADS = NSA_WIDTH // HEAD_DIM
N_BRANCH = 3
CMP_BLOCK = 32
CMP_STRIDE = 16
SEL_BLOCK = 64
SEL_TOPK = 16
WINDOW = 512
MEM_LEN = 256
MEM_WIDTH = D_MODEL // 4
MEM_HEADS = 4
MEM_HEAD_DIM = MEM_WIDTH // MEM_HEADS
MIX_WIDTH = RWKV_WIDTH + NSA_WIDTH + MEM_WIDTH
ROPE_THETA = 500000.0
ROPE_DIM = HEAD_DIM // 4
Q_BLOCK = 128
NORM_EPS = 1e-6
NEG_INF = -1e30
FORCE_SCORE = 1e4

RWKV_SIZES = (RWKV_WIDTH, RWKV_WIDTH, RWKV_WIDTH, RWKV_WIDTH, DECAY_LORA, ICLR_LORA)
NSA_SIZES = (NSA_WIDTH, NSA_WIDTH, NSA_HEADS * N_BRANCH) + (HEAD_DIM,) * 6
MEM_SIZES = (MEM_WIDTH, MEM_WIDTH)
RWKV_COLS = sum(RWKV_SIZES)
NSA_COLS = sum(NSA_SIZES)
MEM_COLS = sum(MEM_SIZES)
IN_COLS = RWKV_COLS + NSA_COLS + MEM_COLS

kernel_name = 'hymba_rwkv7_nsa_memx_layer'


def split_cols(p, sizes):
    offs = [int(o) for o in np.cumsum(sizes)[:-1]]
    return jnp.split(p, offs, axis=-1)


def rms_norm(x, g, eps=NORM_EPS):
    xf = x.astype(jnp.float32)
    y = xf * lax.rsqrt(jnp.mean(xf * xf, axis=-1, keepdims=True) + eps)
    return (y * g.astype(jnp.float32)).astype(x.dtype)


def partial_rope(x, pos):
    half = ROPE_DIM // 2
    inv_freq = ROPE_THETA ** (-jnp.arange(half, dtype=jnp.float32) / half)
    ang = pos.astype(jnp.float32)[:, None] * inv_freq[None, :]
    cos = jnp.cos(ang)[:, None, :]
    sin = jnp.sin(ang)[:, None, :]
    xf = x.astype(jnp.float32)
    x1 = xf[..., :half]
    x2 = xf[..., half:ROPE_DIM]
    out = jnp.concatenate([x1 * cos - x2 * sin, x2 * cos + x1 * sin, xf[..., ROPE_DIM:]], axis=-1)
    return out.astype(x.dtype)


def masked_softmax(s, mask):
    s = jnp.where(mask, s.astype(jnp.float32), NEG_INF)
    m = jnp.max(s, axis=-1, keepdims=True)
    e = jnp.exp(s - m) * mask
    return e / jnp.maximum(jnp.sum(e, axis=-1, keepdims=True), 1e-30)


def head_rms_norm(o, g):
    of = o.astype(jnp.float32)
    y = of * lax.rsqrt(jnp.mean(of * of, axis=-1, keepdims=True) + NORM_EPS)
    return (y * g.astype(jnp.float32)).astype(o.dtype)


def rwkv7_scan(r, decay, k, v, kk, a):
    B, T, H, N = r.shape

    def step(S, inp):
        r_t, w_t, k_t, v_t, kk_t, a_t = inp
        sa = jnp.einsum('bhvk,bhk->bhv', S, -kk_t)
        S = (S * w_t[:, :, None, :]
             + sa[..., None] * (kk_t * a_t)[:, :, None, :]
             + v_t[..., None] * k_t[:, :, None, :])
        y = jnp.einsum('bhvk,bhk->bhv', S, r_t)
        return S, y

    xs = tuple(jnp.moveaxis(u, 1, 0) for u in (r, decay, k, v, kk, a))
    S0 = jnp.zeros((B, H, N, N), jnp.float32)
    _, y = lax.scan(step, S0, xs)
    return jnp.moveaxis(y, 0, 1)


def rwkv7_group(p, mu, w0, w_up, a0, a_up, k_k, k_a, r_k, ln_w, ln_b):
    B, T, _ = p.shape
    H, N = RWKV_HEADS, HEAD_DIM
    pf = p.astype(jnp.float32)
    prev = jnp.pad(pf, ((0, 0), (1, 0), (0, 0)))[:, :-1]
    pf = pf + mu * (prev - pf)
    r, k, v, gate, wd, ad = split_cols(pf, RWKV_SIZES)
    w_log = -jax.nn.softplus(-(w0 + jnp.tanh(wd) @ w_up)) - 0.5
    decay = jnp.exp(-jnp.exp(w_log))
    a = jax.nn.sigmoid(a0 + ad @ a_up)
    kk = k * k_k
    k = k * (1.0 + (a - 1.0) * k_a)
    r, k, v, kk, a, decay = (u.reshape(B, T, H, N) for u in (r, k, v, kk, a, decay))
    kk = kk / jnp.maximum(jnp.sqrt(jnp.sum(kk * kk, axis=-1, keepdims=True)), 1e-12)
    y = rwkv7_scan(r, decay, k, v, kk, a)
    mean = jnp.mean(y, axis=-1, keepdims=True)
    var = jnp.mean(jnp.square(y - mean), axis=-1, keepdims=True)
    y = (y - mean) * lax.rsqrt(var + RWKV_GN_EPS) * ln_w.reshape(H, N) + ln_b.reshape(H, N)
    y = y + jnp.sum(r * k * r_k, axis=-1, keepdims=True) * v
    y = y.reshape(B, T, RWKV_WIDTH) * jax.nn.silu(gate)
    return y.astype(p.dtype)


def nsa_group(p, pos, cmp_pos, ck_w1, ck_w2, cv_w1, cv_w2, gate_b, out_g):
    B, T, _ = p.shape
    Hn, D = NSA_HEADS, HEAD_DIM
    q, gate, glog, kc, vc, ks, vs, kw, vw = split_cols(p, NSA_SIZES)
    q = q.reshape(B, T, Hn, D)
    q_rope = partial_rope(q, pos)
    ks = partial_rope(ks[:, :, None, :], pos)[:, :, 0]
    kw = partial_rope(kw[:, :, None, :], pos)[:, :, 0]
    scale = HEAD_DIM ** -0.5

    n_cmp = (T - CMP_BLOCK) // CMP_STRIDE + 1
    blk = jnp.arange(n_cmp)[:, None] * CMP_STRIDE + jnp.arange(CMP_BLOCK)[None, :]

    def compress(u, w1, w2):
        ub = (u[:, blk] + cmp_pos).reshape(B, n_cmp, CMP_BLOCK * D)
        return jax.nn.silu(ub @ w1) @ w2

    k_cmp = compress(kc, ck_w1, ck_w2)
    v_cmp = compress(vc, cv_w1, cv_w2)
    cmp_end = blk[:, -1]

    n_sel = T // SEL_BLOCK
    n_top = min(SEL_TOPK, n_sel)
    sel_start = jnp.arange(n_sel) * SEL_BLOCK
    overlap = jnp.clip(jnp.minimum(blk[:, -1:] + 1, sel_start[None, :] + SEL_BLOCK)
                       - jnp.maximum(blk[:, :1], sel_start[None, :]), 0, None)
    overlap = overlap.astype(jnp.float32) / CMP_BLOCK
    k_sel_blk = ks.reshape(B, n_sel, SEL_BLOCK, D)
    v_sel_blk = vs.reshape(B, n_sel, SEL_BLOCK, D)
    blk_id = jnp.arange(n_sel)

    kw_pad = jnp.pad(kw, ((0, 0), (WINDOW, 0), (0, 0)))
    vw_pad = jnp.pad(vw, ((0, 0), (WINDOW, 0), (0, 0)))

    def per_block(qb):
        t0 = qb * Q_BLOCK
        tq = t0 + jnp.arange(Q_BLOCK)
        qn = lax.dynamic_slice_in_dim(q, t0, Q_BLOCK, axis=1)
        qr = lax.dynamic_slice_in_dim(q_rope, t0, Q_BLOCK, axis=1)
        s_c = jnp.einsum('bqhd,bcd->bhqc', qn, k_cmp) * scale
        p_c = masked_softmax(s_c, cmp_end[None, :] <= tq[:, None])
        o_c = jnp.einsum('bhqc,bcd->bqhd', p_c.astype(v_cmp.dtype), v_cmp)
        imp = jnp.einsum('bhqc,cs->bqs', p_c, overlap)
        cur = tq // SEL_BLOCK
        valid = blk_id[None, :] <= cur[:, None]
        forced = (blk_id[None, :] == 0) | (blk_id[None, :] == cur[:, None]) | (blk_id[None, :] == cur[:, None] - 1)
        score = jnp.where(forced, FORCE_SCORE, jnp.where(valid, imp, -1.0))
        _, idx = lax.top_k(score, n_top)
        kg = jax.vmap(lambda kb, ib: kb[ib])(k_sel_blk, idx).reshape(B, Q_BLOCK, n_top * SEL_BLOCK, D)
        vg = jax.vmap(lambda vb, ib: vb[ib])(v_sel_blk, idx).reshape(B, Q_BLOCK, n_top * SEL_BLOCK, D)
        kpos = (idx[..., None] * SEL_BLOCK + jnp.arange(SEL_BLOCK)).reshape(B, Q_BLOCK, n_top * SEL_BLOCK)
        s_s = jnp.einsum('bqhd,bqkd->bhqk', qr, kg) * scale
        p_s = masked_softmax(s_s, (kpos <= tq[None, :, None])[:, None])
        o_s = jnp.einsum('bhqk,bqkd->bqhd', p_s.astype(vg.dtype), vg)
        kwb = lax.dynamic_slice_in_dim(kw_pad, t0, Q_BLOCK + WINDOW, axis=1)
        vwb = lax.dynamic_slice_in_dim(vw_pad, t0, Q_BLOCK + WINDOW, axis=1)
        wpos = t0 - WINDOW + jnp.arange(Q_BLOCK + WINDOW)
        diff = tq[:, None] - wpos[None, :]
        m_w = (diff >= 0) & (diff < WINDOW) & (wpos[None, :] >= 0)
        s_w = jnp.einsum('bqhd,bkd->bhqk', qr, kwb) * scale
        p_w = masked_softmax(s_w, m_w)
        o_w = jnp.einsum('bhqk,bkd->bqhd', p_w.astype(vwb.dtype), vwb)
        return o_c, o_s, o_w

    o_c, o_s, o_w = lax.map(per_block, jnp.arange(T // Q_BLOCK))
    o_c, o_s, o_w = (jnp.moveaxis(o, 0, 1).reshape(B, T, Hn, D) for o in (o_c, o_s, o_w))
    g = jax.nn.sigmoid(glog.reshape(B, T, Hn, N_BRANCH) + gate_b.reshape(Hn, N_BRANCH))
    o = g[..., 0:1] * o_c + g[..., 1:2] * o_s + g[..., 2:3] * o_w
    o = head_rms_norm(o, out_g.reshape(Hn, D))
    return o.reshape(B, T, NSA_WIDTH) * jax.nn.silu(gate)


def memory_group(p, mem_k, mem_v, out_g):
    B, T, _ = p.shape
    q, gate = split_cols(p, MEM_SIZES)
    q = q.reshape(B, T, MEM_HEADS, MEM_HEAD_DIM)
    s = jnp.einsum('bthd,bmhd->bhtm', q, mem_k) * (MEM_HEAD_DIM ** -0.5)
    prob = jax.nn.softmax(s.astype(jnp.float32), axis=-1)
    o = jnp.einsum('bhtm,bmhd->bthd', prob.astype(mem_v.dtype), mem_v)
    o = head_rms_norm(o, out_g.reshape(MEM_HEADS, MEM_HEAD_DIM))
    return o.reshape(B, T, MEM_WIDTH) * jax.nn.silu(gate)


def setup_inputs(seed: int = 0) -> dict:
    key = jax.random.key(seed)
    ks = jax.random.split(key, 32)
    L = DEPTH
    f32 = jnp.float32

    def nrm(k, shape, scale):
        return jax.random.normal(k, shape, f32) * scale

    def gain(k, shape):
        return 1.0 + 0.05 * jax.random.normal(k, shape, f32)

    cmp_in = CMP_BLOCK * HEAD_DIM
    return {
        'x': nrm(ks[0], (BATCH, SEQ, D_MODEL), 1.0),
        'mem': nrm(ks[1], (BATCH, MEM_LEN, D_MODEL), 1.0),
        'norm_in_g': gain(ks[2], (L, D_MODEL)),
        'w_in': nrm(ks[3], (L, D_MODEL, IN_COLS), D_MODEL ** -0.5),
        'rwkv_mu': jax.random.uniform(ks[4], (L, RWKV_COLS), f32, 0.2, 0.8),
        'rwkv_w0': jax.random.uniform(ks[5], (L, RWKV_WIDTH), f32, -5.0, -1.0),
        'rwkv_w_up': nrm(ks[6], (L, DECAY_LORA, RWKV_WIDTH), 0.1 * DECAY_LORA ** -0.5),
        'rwkv_a0': nrm(ks[7], (L, RWKV_WIDTH), 0.5),
        'rwkv_a_up': nrm(ks[8], (L, ICLR_LORA, RWKV_WIDTH), ICLR_LORA ** -0.5),
        'rwkv_k_k': 0.85 + 0.05 * jax.random.normal(ks[9], (L, RWKV_WIDTH), f32),
        'rwkv_k_a': 1.0 + 0.05 * jax.random.normal(ks[10], (L, RWKV_WIDTH), f32),
        'rwkv_r_k': nrm(ks[11], (L, RWKV_HEADS, HEAD_DIM), 0.1),
        'rwkv_ln_w': gain(ks[12], (L, RWKV_WIDTH)),
        'rwkv_ln_b': nrm(ks[13], (L, RWKV_WIDTH), 0.02),
        'nsa_cmp_pos': nrm(ks[14], (L, CMP_BLOCK, HEAD_DIM), 0.1),
        'nsa_cmp_k_w1': nrm(ks[15], (L, cmp_in, HEAD_DIM), cmp_in ** -0.5),
        'nsa_cmp_k_w2': nrm(ks[16], (L, HEAD_DIM, HEAD_DIM), HEAD_DIM ** -0.5),
        'nsa_cmp_v_w1': nrm(ks[17], (L, cmp_in, HEAD_DIM), cmp_in ** -0.5),
        'nsa_cmp_v_w2': nrm(ks[18], (L, HEAD_DIM, HEAD_DIM), HEAD_DIM ** -0.5),
        'nsa_gate_b': nrm(ks[19], (L, NSA_HEADS * N_BRANCH), 0.1),
        'nsa_out_g': gain(ks[20], (L, NSA_WIDTH)),
        'mem_norm_g': gain(ks[21], (L, D_MODEL)),
        'w_mem_kv': nrm(ks[22], (L, D_MODEL, 2 * MEM_WIDTH), D_MODEL ** -0.5),
        'mem_out_g': gain(ks[23], (L, MEM_WIDTH)),
        'w_out': nrm(ks[24], (L, MIX_WIDTH, D_MODEL), MIX_WIDTH ** -0.5),
        'norm_final_g': gain(ks[25], (D_MODEL,)),
    }


def reference(x, mem, norm_in_g, w_in, rwkv_mu, rwkv_w0, rwkv_w_up, rwkv_a0, rwkv_a_up,
              rwkv_k_k, rwkv_k_a, rwkv_r_k, rwkv_ln_w, rwkv_ln_b, nsa_cmp_pos, nsa_cmp_k_w1,
              nsa_cmp_k_w2, nsa_cmp_v_w1, nsa_cmp_v_w2, nsa_gate_b, nsa_out_g, mem_norm_g,
              w_mem_kv, mem_out_g, w_out, norm_final_g):
    B, T, _ = x.shape
    M = mem.shape[1]
    pos = jnp.arange(T)
    for l in range(DEPTH):
        h = rms_norm(x, norm_in_g[l])
        p = h @ w_in[l]
        p_rwkv, p_nsa, p_mem = jnp.split(p, [RWKV_COLS, RWKV_COLS + NSA_COLS], axis=-1)
        y_rwkv = rwkv7_group(p_rwkv, rwkv_mu[l], rwkv_w0[l], rwkv_w_up[l], rwkv_a0[l], rwkv_a_up[l],
                             rwkv_k_k[l], rwkv_k_a[l], rwkv_r_k[l], rwkv_ln_w[l], rwkv_ln_b[l])
        y_nsa = nsa_group(p_nsa, pos, nsa_cmp_pos[l], nsa_cmp_k_w1[l], nsa_cmp_k_w2[l],
                          nsa_cmp_v_w1[l], nsa_cmp_v_w2[l], nsa_gate_b[l], nsa_out_g[l])
        kv = rms_norm(mem, mem_norm_g[l]) @ w_mem_kv[l]
        mem_k, mem_v = jnp.split(kv.reshape(B, M, 2, MEM_HEADS, MEM_HEAD_DIM), 2, axis=2)
        y_mem = memory_group(p_mem, mem_k[:, :, 0], mem_v[:, :, 0], mem_out_g[l])
        y = jnp.concatenate([y_rwkv.astype(x.dtype), y_nsa.astype(x.dtype), y_mem.astype(x.dtype)], axis=-1)
        x = x + y @ w_out[l]
    return rms_norm(x, norm_final_g)
```

```python
import functools

import numpy as np
import jax
import jax.numpy as jnp
from jax import lax
from jax.experimental import pallas as pl
from jax.experimental.pallas import tpu as pltpu

F32 = jnp.float32
BF16 = jnp.bfloat16

HEAD_DIM = 64
RWKV_HEADS = 8
RWKV_WIDTH = RWKV_HEADS * HEAD_DIM
LORA = 64
RWKV_COLS = 4 * RWKV_WIDTH + 2 * LORA
RWKV_GN_EPS = 64e-5
NSA_HEADS = 4
NSA_WIDTH = NSA_HEADS * HEAD_DIM
N_BRANCH = 3
CMP_BLOCK = 32
CMP_STRIDE = 16
SEL_BLOCK = 64
SEL_TOPK = 16
WINDOW = 512
MEM_HEADS = 4
MEM_WIDTH = MEM_HEADS * HEAD_DIM
ROPE_THETA = 500000.0
ROPE_HALF = 8
Q_BLOCK = 128
NORM_EPS = 1e-6
NEG_INF = -1e30
FORCE_SCORE = 1e4

RWKV_CHUNK = 64
SEL_KEY_CHUNK = 512
LANES = 128
VMEM_LIMIT = 48 * 1024 * 1024

HI = lax.Precision.HIGHEST


def _dot(a, b, precision=None):
    return jnp.dot(a, b, preferred_element_type=F32, precision=precision)


def _dot_nt(a, b, precision=None):
    return lax.dot_general(a, b, (((1,), (1,)), ((), ())), preferred_element_type=F32,
                           precision=precision)


def _dot_tn(a, b, precision=None):
    return lax.dot_general(a, b, (((0,), (0,)), ((), ())), preferred_element_type=F32,
                           precision=precision)


def _sigmoid(x):
    return 1.0 / (1.0 + jnp.exp(-x))


def _silu(x):
    return x * _sigmoid(x)


def _params(*sem):
    return pltpu.CompilerParams(dimension_semantics=sem, vmem_limit_bytes=VMEM_LIMIT)


def _inproj_kernel(x_ref, g_ref, wrow_ref, wt_ref, prow_ref, kc_ref, pt_ref):
    x = x_ref[0]
    h = x * lax.rsqrt(jnp.mean(x * x, axis=-1, keepdims=True) + NORM_EPS) * g_ref[...]
    hb = h.astype(BF16)
    row = _dot(hb, wrow_ref[...])
    prow_ref[0] = row[:, :RWKV_COLS]
    kc_ref[0] = row[:, RWKV_COLS:]
    pt_ref[0] = _dot_nt(wt_ref[...], hb)


def _input_projection(x, g, w_row, w_t, tm):
    B, T, D = x.shape
    n_row = w_row.shape[1]
    n_t = w_t.shape[0]
    return pl.pallas_call(
        _inproj_kernel,
        grid=(B, T // tm),
        in_specs=[
            pl.BlockSpec((1, tm, D), lambda b, i: (b, i, 0)),
            pl.BlockSpec((1, D), lambda b, i: (0, 0)),
            pl.BlockSpec((D, n_row), lambda b, i: (0, 0)),
            pl.BlockSpec((n_t, D), lambda b, i: (0, 0)),
        ],
        out_specs=[
            pl.BlockSpec((1, tm, RWKV_COLS), lambda b, i: (b, i, 0)),
            pl.BlockSpec((1, tm, n_row - RWKV_COLS), lambda b, i: (b, i, 0)),
            pl.BlockSpec((1, n_t, tm), lambda b, i: (b, 0, i)),
        ],
        out_shape=[
            jax.ShapeDtypeStruct((B, T, RWKV_COLS), F32),
            jax.ShapeDtypeStruct((B, T, n_row - RWKV_COLS), F32),
            jax.ShapeDtypeStruct((B, n_t, T), F32),
        ],
        compiler_params=_params("parallel", "parallel"),
        name="input_projection",
    )(x, g, w_row, w_t)


def _rwkv_kernel(p_ref, prev_ref, mu_ref, w0_ref, wup_ref, a0_ref, aup_ref, kk_ref, ka_ref,
                 rk_ref, lnw_ref, lnb_ref, bd_ref, tri_ref, y_ref, s_ref):
    C, W, N, H = RWKV_CHUNK, RWKV_WIDTH, HEAD_DIM, RWKV_HEADS
    n = pl.program_id(1)

    @pl.when(n == 0)
    def _():
        s_ref[...] = jnp.zeros_like(s_ref)

    p = p_ref[0]
    prev_last = jnp.where(n == 0, 0.0, prev_ref[0][7:8, :])
    row = lax.broadcasted_iota(jnp.int32, (C, 1), 0)
    prev = jnp.where(row == 0, prev_last, pltpu.roll(p, 1, axis=0))
    pf = p + mu_ref[...] * (prev - p)
    r = pf[:, 0:W]
    k = pf[:, W:2 * W]
    v = pf[:, 2 * W:3 * W]
    gate = pf[:, 3 * W:4 * W]
    wd = pf[:, 4 * W:4 * W + LORA]
    ad = pf[:, 4 * W + LORA:4 * W + 2 * LORA]

    z = w0_ref[...] + _dot(jnp.tanh(wd), wup_ref[...], HI)
    w_log = -(jnp.maximum(-z, 0.0) + jnp.log(1.0 + jnp.exp(-jnp.abs(z)))) - 0.5
    logw = -jnp.exp(w_log)
    eta = _sigmoid(a0_ref[...] + _dot(ad, aup_ref[...], HI))
    kk = k * kk_ref[...]
    k2 = k * (1.0 + (eta - 1.0) * ka_ref[...])
    bd = bd_ref[...]
    kk = kk / jnp.maximum(jnp.sqrt(_dot(kk * kk, bd, HI)), 1e-12)
    bvec = kk * eta
    bonus = _dot(r * k2 * rk_ref[...], bd, HI)

    cs = _dot(tri_ref[...], logw, HI)
    mid = cs[C // 2 - 1:C // 2, :]
    csm = cs - mid
    g_mid = jnp.exp(mid)
    g_end = jnp.exp(cs[C - 1:C, :])
    g_end_mid = jnp.exp(cs[C - 1:C, :] - mid)
    e_in = jnp.exp(csm)
    e_out = jnp.exp(-csm)
    r_t = r * e_in
    a_t = -kk * jnp.exp(csm - logw)
    k_t = k2 * e_out
    b_t = bvec * e_out

    ri = lax.broadcasted_iota(jnp.int32, (C, C), 0)
    ci = lax.broadcasted_iota(jnp.int32, (C, C), 1)
    strict = ri > ci
    incl = ri >= ci
    eye = (ri == ci).astype(F32)

    ys = []
    for h in range(H):
        sl = slice(h * N, (h + 1) * N)
        ar = jnp.concatenate([a_t[:, sl], r_t[:, sl]], axis=0)
        kb = jnp.concatenate([k_t[:, sl], b_t[:, sl]], axis=0)
        vh = v[:, sl]
        g = _dot_nt(ar, kb, HI)
        a_ak = jnp.where(strict, g[0:C, 0:C], 0.0)
        a_ab = jnp.where(strict, g[0:C, C:2 * C], 0.0)
        a_rk = jnp.where(incl, g[C:2 * C, 0:C], 0.0)
        a_rb = jnp.where(incl, g[C:2 * C, C:2 * C], 0.0)
        tm = eye + a_ab
        pw = _dot(a_ab, a_ab, HI)
        span = 2
        while span < C:
            if 2 * span < C:
                tp = _dot(pw, jnp.concatenate([tm, pw], axis=1), HI)
                tm = tm + tp[:, 0:C]
                pw = tp[:, C:2 * C]
            else:
                tm = tm + _dot(pw, tm, HI)
            span *= 2
        s0 = s_ref[h]
        uy0 = _dot_nt(ar, s0 * g_mid[:, sl], HI)
        u = _dot(tm, uy0[0:C] + _dot(a_ak, vh, HI), HI)
        vu = jnp.concatenate([vh, u], axis=0)
        y = uy0[C:2 * C] + _dot(jnp.concatenate([a_rk, a_rb], axis=1), vu, HI)
        s_ref[h] = s0 * g_end[:, sl] + _dot_tn(vu, kb, HI) * g_end_mid[:, sl]
        mean = jnp.mean(y, axis=-1, keepdims=True)
        var = jnp.mean(jnp.square(y - mean), axis=-1, keepdims=True)
        ys.append((y - mean) * lax.rsqrt(var + RWKV_GN_EPS))
    y = jnp.concatenate(ys, axis=1) * lnw_ref[...] + lnb_ref[...] + bonus * v
    y_ref[0] = (y * _silu(gate)).astype(y_ref.dtype)


def _rwkv_group(p_row, mu, w0, w_up, a0, a_up, k_k, k_a, r_k, ln_w, ln_b):
    B, T, _ = p_row.shape
    C, W = RWKV_CHUNK, RWKV_WIDTH
    heads = np.arange(W) // HEAD_DIM
    bd = jnp.asarray(heads[:, None] == heads[None, :], F32)
    tri = jnp.asarray(np.tril(np.ones((C, C), np.float32)))
    vec = lambda a: a.reshape(1, -1)
    const = lambda shape: pl.BlockSpec(shape, lambda b, n: (0,) * len(shape))
    return pl.pallas_call(
        _rwkv_kernel,
        grid=(B, T // C),
        in_specs=[
            pl.BlockSpec((1, C, RWKV_COLS), lambda b, n: (b, n, 0)),
            pl.BlockSpec((1, 8, RWKV_COLS), lambda b, n: (b, jnp.maximum(n * (C // 8) - 1, 0), 0)),
            const((1, RWKV_COLS)), const((1, W)), const((LORA, W)), const((1, W)), const((LORA, W)),
            const((1, W)), const((1, W)), const((1, W)), const((1, W)), const((1, W)),
            const((W, W)), const((C, C)),
        ],
        out_specs=pl.BlockSpec((1, C, W), lambda b, n: (b, n, 0)),
        out_shape=jax.ShapeDtypeStruct((B, T, W), BF16),
        scratch_shapes=[pltpu.VMEM((RWKV_HEADS, HEAD_DIM, HEAD_DIM), F32)],
        compiler_params=_params("parallel", "arbitrary"),
        name="rwkv7_group",
    )(p_row, p_row, vec(mu), vec(w0), w_up, vec(a0), a_up, vec(k_k), vec(k_a), vec(r_k),
      vec(ln_w), vec(ln_b), bd, tri)


def _rope_rows(x, cos, sin):
    x1, x2 = x[0:ROPE_HALF], x[ROPE_HALF:2 * ROPE_HALF]
    return jnp.concatenate([x1 * cos - x2 * sin, x2 * cos + x1 * sin, x[2 * ROPE_HALF:]], axis=0)


def _nsa_keys_kernel(kv_ref, rope_ref, kskw_ref, vt_ref):
    kv = kv_ref[0]
    cos, sin = rope_ref[0:ROPE_HALF], rope_ref[ROPE_HALF:2 * ROPE_HALF]
    keys = jnp.concatenate([_rope_rows(kv[0:HEAD_DIM], cos, sin),
                            _rope_rows(kv[HEAD_DIM:2 * HEAD_DIM], cos, sin)], axis=0)
    kskw_ref[0] = keys.T.astype(BF16)
    vals = kv[2 * HEAD_DIM:4 * HEAD_DIM].astype(BF16)
    for u in range(vals.shape[1] // LANES):
        vt_ref[0, u] = vals[:, u * LANES:(u + 1) * LANES]


def _nsa_keys(pt, rope, tk):
    B, _, T = pt.shape
    return pl.pallas_call(
        _nsa_keys_kernel,
        grid=(B, T // tk),
        in_specs=[
            pl.BlockSpec((1, 4 * HEAD_DIM, tk), lambda b, i: (b, 0, i)),
            pl.BlockSpec((2 * ROPE_HALF, tk), lambda b, i: (0, i)),
        ],
        out_specs=[
            pl.BlockSpec((1, tk, LANES), lambda b, i: (b, i, 0)),
            pl.BlockSpec((1, tk // LANES, LANES, LANES), lambda b, i: (b, i, 0, 0)),
        ],
        out_shape=[
            jax.ShapeDtypeStruct((B, T, LANES), BF16),
            jax.ShapeDtypeStruct((B, T // LANES, LANES, LANES), BF16),
        ],
        compiler_params=_params("parallel", "parallel"),
        name="nsa_keys",
    )(pt, rope)


def _nsa_compress_kernel(g_ref, wc_ref, pos_ref, w2_ref, w2t_ref, rm_ref, t_ref):
    ng = g_ref.shape[1]
    wc = wc_ref[...]
    m = _dot(g_ref[0].astype(BF16), wc)
    pm = _dot(pos_ref[...], wc)
    pos_term = pm[0:1, 0:LANES] + pm[1:2, LANES:2 * LANES]
    pre = m[:, 0:LANES] + pltpu.roll(m[:, LANES:2 * LANES], ng - 1, axis=0) + pos_term
    act = _silu(pre).astype(BF16)
    row = lax.broadcasted_iota(jnp.int32, (ng, 1), 0)
    col = lax.broadcasted_iota(jnp.int32, (1, ng), 1)
    rm_ref[0] = jnp.where(row < ng - 1, _dot(act, w2_ref[...]), 0.0).astype(BF16)
    t_ref[0] = jnp.where(col < ng - 1, _dot_nt(w2t_ref[...], act), 0.0).astype(BF16)


def _nsa_compress(kcvc, cmp_pos, k_w1, k_w2, v_w1, v_w2):
    B, T, _ = kcvc.shape
    ng = T // CMP_STRIDE
    half = CMP_BLOCK // 2
    g = kcvc.reshape(B, ng, half * LANES)
    D = HEAD_DIM

    def spread(w, second, is_v):
        blk = w[second * half * D:(second + 1) * half * D].reshape(half, D, D)
        z = jnp.zeros_like(blk)
        return jnp.concatenate([z, blk] if is_v else [blk, z], axis=1).reshape(half * LANES, D)

    wc = jnp.concatenate([spread(k_w1, 0, False), spread(v_w1, 0, True),
                          spread(k_w1, 1, False), spread(v_w1, 1, True)], axis=1).astype(BF16)
    pos2 = jnp.concatenate([cmp_pos, cmp_pos], axis=1)
    pos = jnp.zeros((8, half * LANES), F32)
    pos = pos.at[0].set(pos2[:half].reshape(-1)).at[1].set(pos2[half:].reshape(-1)).astype(BF16)
    z = jnp.zeros((D, D), F32)
    w2 = jnp.block([[k_w2, z], [z, v_w2]])
    const = lambda shape: pl.BlockSpec(shape, lambda b: (0,) * len(shape))
    return pl.pallas_call(
        _nsa_compress_kernel,
        grid=(B,),
        in_specs=[
            pl.BlockSpec((1, ng, half * LANES), lambda b: (b, 0, 0)),
            const((half * LANES, 2 * LANES)), const((8, half * LANES)),
            const((LANES, LANES)), const((LANES, LANES)),
        ],
        out_specs=[
            pl.BlockSpec((1, ng, LANES), lambda b: (b, 0, 0)),
            pl.BlockSpec((1, LANES, ng), lambda b: (b, 0, 0)),
        ],
        out_shape=[
            jax.ShapeDtypeStruct((B, ng, LANES), BF16),
            jax.ShapeDtypeStruct((B, LANES, ng), BF16),
        ],
        compiler_params=_params("parallel"),
        name="nsa_compress",
    )(g, wc, pos, w2.astype(BF16), w2.T.astype(BF16))


def _tile_heads(x):
    return jnp.concatenate([x] * NSA_HEADS, axis=1)


def _nsa_attn_kernel(q_ref, gate_ref, glog_ref, rope_ref, rm_ref, ct_ref, kskw_ref, vt_ref,
                     ov_ref, gb_ref, og_ref, y_ref, sel_ref, m_ref, l_ref, acc_ref):
    D, Q, Hn = HEAD_DIM, Q_BLOCK, NSA_HEADS
    QH = Q * Hn
    KC = SEL_KEY_CHUNK
    ng = rm_ref.shape[1]
    ns = ov_ref.shape[0]
    n_top = min(SEL_TOPK, ns)
    i = pl.program_id(1)
    t0 = i * Q
    tq = t0 + lax.broadcasted_iota(jnp.int32, (1, Q), 1)

    q = q_ref[0] * (D ** -0.5)
    cos, sin = rope_ref[0:ROPE_HALF], rope_ref[ROPE_HALF:2 * ROPE_HALF]
    qh = [q[h * D:(h + 1) * D] for h in range(Hn)]
    q4 = jnp.concatenate(qh, axis=1)
    q4r = jnp.concatenate([_rope_rows(x, cos, sin) for x in qh], axis=1)
    zero = jnp.zeros_like(q4)
    q_lo = jnp.concatenate([q4, zero], axis=0).astype(BF16)
    qr_lo = jnp.concatenate([q4r, zero], axis=0).astype(BF16)
    qr_hi = jnp.concatenate([zero, q4r], axis=0).astype(BF16)
    tq4 = _tile_heads(tq)

    s = _dot(rm_ref[0], q_lo)
    cend = lax.broadcasted_iota(jnp.int32, (ng, 1), 0) * CMP_STRIDE + (CMP_BLOCK - 1)
    mask = cend <= tq4
    s = jnp.where(mask, s, NEG_INF)
    e = jnp.where(mask, jnp.exp(s - jnp.max(s, axis=0, keepdims=True)), 0.0)
    p = e / jnp.maximum(jnp.sum(e, axis=0, keepdims=True), 1e-30)
    o_c = _dot(ct_ref[0][D:2 * D], p.astype(BF16))
    psum = p[:, 0:Q]
    for h in range(1, Hn):
        psum = psum + p[:, h * Q:(h + 1) * Q]
    imp = _dot(ov_ref[...], psum.astype(BF16))

    blk = lax.broadcasted_iota(jnp.int32, (ns, Q), 0)
    cur = tq // SEL_BLOCK
    forced = (blk == 0) | (blk == cur) | (blk == cur - 1)
    score = jnp.where(forced, FORCE_SCORE, jnp.where(blk <= cur, imp, -1.0))
    taken = -2.0
    for _ in range(n_top):
        best = jnp.max(score, axis=0, keepdims=True)
        first = jnp.min(jnp.where(score == best, blk, ns), axis=0, keepdims=True)
        score = jnp.where(blk == first, taken, score)
    sel_ref[...] = (score == taken).astype(F32)

    m_ref[...] = jnp.full_like(m_ref, NEG_INF)
    l_ref[...] = jnp.zeros_like(l_ref)
    acc_ref[...] = jnp.zeros_like(acc_ref)
    krow = lax.broadcasted_iota(jnp.int32, (KC, 1), 0)

    def sel_step(j, carry):
        k0 = pl.multiple_of(j * KC, KC)
        s = _dot(kskw_ref[0, pl.ds(k0, KC), :], qr_lo)
        rows = [jnp.broadcast_to(sel_ref[pl.ds(j * (KC // SEL_BLOCK) + b, 1), :], (SEL_BLOCK, Q))
                for b in range(KC // SEL_BLOCK)]
        keep = jnp.where(k0 + krow <= tq, jnp.concatenate(rows, axis=0), 0.0)
        mask = _tile_heads(keep) > 0.5
        s = jnp.where(mask, s, NEG_INF)
        m_old = m_ref[...]
        m_new = jnp.maximum(m_old, jnp.max(s, axis=0, keepdims=True))
        alpha = jnp.exp(m_old - m_new)
        e = jnp.where(mask, jnp.exp(s - m_new), 0.0)
        l_ref[...] = alpha * l_ref[...] + jnp.sum(e, axis=0, keepdims=True)
        eb = e.astype(BF16)
        pv = jnp.zeros((D, QH), F32)
        for u in range(KC // LANES):
            pv = pv + _dot(vt_ref[0, j * (KC // LANES) + u][0:D], eb[u * LANES:(u + 1) * LANES])
        acc_ref[...] = acc_ref[...] * alpha + pv
        m_ref[...] = m_new
        return carry

    lax.fori_loop(0, (t0 + Q + KC - 1) // KC, sel_step, 0)
    o_s = acc_ref[...] / jnp.maximum(l_ref[...], 1e-30)

    WK = WINDOW + Q
    w0 = pl.multiple_of(jnp.maximum(t0 - WINDOW, 0), LANES)
    s = _dot(kskw_ref[0, pl.ds(w0, WK), :], qr_hi)
    diff = tq4 - (w0 + lax.broadcasted_iota(jnp.int32, (WK, 1), 0))
    mask = (diff >= 0) & (diff < WINDOW)
    s = jnp.where(mask, s, NEG_INF)
    e = jnp.where(mask, jnp.exp(s - jnp.max(s, axis=0, keepdims=True)), 0.0)
    pb = (e / jnp.maximum(jnp.sum(e, axis=0, keepdims=True), 1e-30)).astype(BF16)
    o_w = jnp.zeros((D, QH), F32)
    for u in range(WK // LANES):
        o_w = o_w + _dot(vt_ref[0, w0 // LANES + u][D:2 * D], pb[u * LANES:(u + 1) * LANES])

    gl = _sigmoid(glog_ref[0] + gb_ref[...])
    ys = []
    for h in range(Hn):
        cs = slice(h * Q, (h + 1) * Q)
        o = (gl[3 * h:3 * h + 1] * o_c[:, cs] + gl[3 * h + 1:3 * h + 2] * o_s[:, cs]
             + gl[3 * h + 2:3 * h + 3] * o_w[:, cs])
        o = o * lax.rsqrt(jnp.mean(o * o, axis=0, keepdims=True) + NORM_EPS)
        ys.append(o)
    y = jnp.concatenate(ys, axis=0) * og_ref[...] * _silu(gate_ref[0])
    y_ref[0] = y.T.astype(y_ref.dtype)


def _nsa_attention(pt, rope, cmp_rm, cmp_t, kskw, vt, gate_b, out_g):
    B, _, T = pt.shape
    ng = T // CMP_STRIDE
    ns = T // SEL_BLOCK
    Q, W = Q_BLOCK, NSA_WIDTH
    QH = Q * NSA_HEADS
    c0 = np.arange(ng)[None, :] * CMP_STRIDE
    s0 = np.arange(ns)[:, None] * SEL_BLOCK
    ov = np.clip(np.minimum(c0 + CMP_BLOCK, s0 + SEL_BLOCK) - np.maximum(c0, s0), 0, None) / CMP_BLOCK
    ov[:, ng - 1] = 0.0
    gb = jnp.zeros((16, 1), F32).at[:NSA_HEADS * N_BRANCH, 0].set(gate_b)
    glog_blk = (4 * HEAD_DIM + 4 * W) // 16
    return pl.pallas_call(
        _nsa_attn_kernel,
        grid=(B, T // Q),
        in_specs=[
            pl.BlockSpec((1, W, Q), lambda b, i: (b, 1, i)),
            pl.BlockSpec((1, W, Q), lambda b, i: (b, 2, i)),
            pl.BlockSpec((1, 16, Q), lambda b, i: (b, glog_blk, i)),
            pl.BlockSpec((2 * ROPE_HALF, Q), lambda b, i: (0, i)),
            pl.BlockSpec((1, ng, LANES), lambda b, i: (b, 0, 0)),
            pl.BlockSpec((1, LANES, ng), lambda b, i: (b, 0, 0)),
            pl.BlockSpec((1, T, LANES), lambda b, i: (b, 0, 0)),
            pl.BlockSpec((1, T // LANES, LANES, LANES), lambda b, i: (b, 0, 0, 0)),
            pl.BlockSpec((ns, ng), lambda b, i: (0, 0)),
            pl.BlockSpec((16, 1), lambda b, i: (0, 0)),
            pl.BlockSpec((W, 1), lambda b, i: (0, 0)),
        ],
        out_specs=pl.BlockSpec((1, Q, W), lambda b, i: (b, i, 0)),
        out_shape=jax.ShapeDtypeStruct((B, T, W), BF16),
        scratch_shapes=[pltpu.VMEM((ns, Q), F32), pltpu.VMEM((1, QH), F32),
                        pltpu.VMEM((1, QH), F32), pltpu.VMEM((HEAD_DIM, QH), F32)],
        compiler_params=_params("parallel", "arbitrary"),
        name="nsa_attention",
    )(pt, pt, pt, rope, cmp_rm, cmp_t, kskw, vt, jnp.asarray(ov, BF16), gb, out_g.reshape(W, 1))


def _mem_kv_kernel(mem_ref, g_ref, w_ref, wt_ref, k_ref, vt_ref):
    x = mem_ref[0]
    h = x * lax.rsqrt(jnp.mean(x * x, axis=-1, keepdims=True) + NORM_EPS) * g_ref[...]
    hb = h.astype(BF16)
    k_ref[0] = _dot(hb, w_ref[...]).astype(BF16)
    vt_ref[0] = _dot_nt(wt_ref[...], hb).astype(BF16)


def _mem_kv(mem, g, w_kv):
    B, M, D = mem.shape
    W = MEM_WIDTH
    const = lambda shape: pl.BlockSpec(shape, lambda b: (0,) * len(shape))
    return pl.pallas_call(
        _mem_kv_kernel,
        grid=(B,),
        in_specs=[pl.BlockSpec((1, M, D), lambda b: (b, 0, 0)), const((1, D)), const((D, W)),
                  const((W, D))],
        out_specs=[pl.BlockSpec((1, M, W), lambda b: (b, 0, 0)),
                   pl.BlockSpec((1, W, M), lambda b: (b, 0, 0))],
        out_shape=[jax.ShapeDtypeStruct((B, M, W), BF16), jax.ShapeDtypeStruct((B, W, M), BF16)],
        compiler_params=_params("parallel"),
        name="mem_kv",
    )(mem, g.reshape(1, D), w_kv[:, :W].astype(BF16), w_kv[:, W:].T.astype(BF16))


def _mem_attn_kernel(q_ref, gate_ref, k_ref, vt_ref, og_ref, y_ref):
    D, Hm = HEAD_DIM, MEM_HEADS
    q = q_ref[0] * (D ** -0.5)
    k = k_ref[0]
    vt = vt_ref[0]
    row = lax.broadcasted_iota(jnp.int32, (Hm * D, 1), 0)
    ys = []
    for h in range(Hm):
        qh = jnp.where(row // D == h, q, 0.0).astype(BF16)
        s = _dot(k, qh)
        e = jnp.exp(s - jnp.max(s, axis=0, keepdims=True))
        p = e / jnp.sum(e, axis=0, keepdims=True)
        o = _dot(vt[h * D:(h + 1) * D], p.astype(BF16))
        ys.append(o * lax.rsqrt(jnp.mean(o * o, axis=0, keepdims=True) + NORM_EPS))
    y = jnp.concatenate(ys, axis=0) * og_ref[...] * _silu(gate_ref[0])
    y_ref[0] = y.T.astype(y_ref.dtype)


def _mem_attention(pt, mem_k, mem_vt, out_g, tm):
    B, _, T = pt.shape
    M = mem_k.shape[1]
    W = MEM_WIDTH
    return pl.pallas_call(
        _mem_attn_kernel,
        grid=(B, T // tm),
        in_specs=[
            pl.BlockSpec((1, W, tm), lambda b, i: (b, 3, i)),
            pl.BlockSpec((1, W, tm), lambda b, i: (b, 4, i)),
            pl.BlockSpec((1, M, W), lambda b, i: (b, 0, 0)),
            pl.BlockSpec((1, W, M), lambda b, i: (b, 0, 0)),
            pl.BlockSpec((W, 1), lambda b, i: (0, 0)),
        ],
        out_specs=pl.BlockSpec((1, tm, W), lambda b, i: (b, i, 0)),
        out_shape=jax.ShapeDtypeStruct((B, T, W), BF16),
        compiler_params=_params("parallel", "parallel"),
        name="mem_attention",
    )(pt, pt, mem_k, mem_vt, out_g.reshape(W, 1))


def _outproj_kernel(x_ref, yr_ref, yn_ref, ym_ref, wr_ref, wn_ref, wm_ref, g_ref, o_ref):
    z = (x_ref[0] + _dot(yr_ref[0], wr_ref[...]) + _dot(yn_ref[0], wn_ref[...])
         + _dot(ym_ref[0], wm_ref[...]))
    o_ref[0] = z * lax.rsqrt(jnp.mean(z * z, axis=-1, keepdims=True) + NORM_EPS) * g_ref[...]


def _output_projection(x, y_rwkv, y_nsa, y_mem, w_out, g, tm):
    B, T, D = x.shape
    wb = w_out.astype(BF16)
    w_r, w_n, w_m = wb[:RWKV_WIDTH], wb[RWKV_WIDTH:RWKV_WIDTH + NSA_WIDTH], wb[RWKV_WIDTH + NSA_WIDTH:]
    tile = lambda w: pl.BlockSpec((1, tm, w), lambda b, i: (b, i, 0))
    const = lambda shape: pl.BlockSpec(shape, lambda b, i: (0,) * len(shape))
    return pl.pallas_call(
        _outproj_kernel,
        grid=(B, T // tm),
        in_specs=[tile(D), tile(RWKV_WIDTH), tile(NSA_WIDTH), tile(MEM_WIDTH),
                  const((RWKV_WIDTH, D)), const((NSA_WIDTH, D)), const((MEM_WIDTH, D)), const((1, D))],
        out_specs=tile(D),
        out_shape=jax.ShapeDtypeStruct((B, T, D), F32),
        compiler_params=_params("parallel", "parallel"),
        name="output_projection",
    )(x, y_rwkv, y_nsa, y_mem, w_r, w_n, w_m, g.reshape(1, D))


def _rope_table(T):
    inv_freq = ROPE_THETA ** (-jnp.arange(ROPE_HALF, dtype=F32) / ROPE_HALF)
    ang = inv_freq[:, None] * jnp.arange(T).astype(F32)[None, :]
    return jnp.concatenate([jnp.cos(ang), jnp.sin(ang)], axis=0)


def _split_w_in(w):
    D = HEAD_DIM
    n0 = RWKV_COLS
    q, gate, glog = n0, n0 + NSA_WIDTH, n0 + 2 * NSA_WIDTH
    kc = glog + NSA_HEADS * N_BRANCH
    vc, ks, vs, kw, vw = kc + D, kc + 2 * D, kc + 3 * D, kc + 4 * D, kc + 5 * D
    m0 = vw + D
    cols = lambda a, n: w[:, a:a + n]
    w_row = jnp.concatenate([cols(0, n0), cols(kc, D), cols(vc, D)], axis=1)
    w_t = jnp.concatenate([cols(ks, D), cols(kw, D), cols(vs, D), cols(vw, D), cols(q, NSA_WIDTH),
                           cols(gate, NSA_WIDTH), cols(m0, MEM_WIDTH), cols(m0 + MEM_WIDTH, MEM_WIDTH),
                           cols(glog, NSA_HEADS * N_BRANCH),
                           jnp.zeros((w.shape[0], 16 - NSA_HEADS * N_BRANCH), w.dtype)], axis=1)
    return w_row.astype(BF16), w_t.T.astype(BF16)


def kernel(x, mem, norm_in_g, w_in, rwkv_mu, rwkv_w0, rwkv_w_up, rwkv_a0, rwkv_a_up, rwkv_k_k,
           rwkv_k_a, rwkv_r_k, rwkv_ln_w, rwkv_ln_b, nsa_cmp_pos, nsa_cmp_k_w1, nsa_cmp_k_w2,
           nsa_cmp_v_w1, nsa_cmp_v_w2, nsa_gate_b, nsa_out_g, mem_norm_g, w_mem_kv, mem_out_g, w_out,
           norm_final_g):
    B, T, D = x.shape
    assert w_in.shape[0] == 1, "single-layer stack: the final norm is fused into the output projection"
    rope = _rope_table(T)
    w_row, w_t = _split_w_in(w_in[0])
    p_row, kcvc, pt = _input_projection(x, norm_in_g[0].reshape(1, D), w_row, w_t, tm=256)
    y_rwkv = _rwkv_group(p_row, rwkv_mu[0], rwkv_w0[0], rwkv_w_up[0], rwkv_a0[0], rwkv_a_up[0],
                         rwkv_k_k[0], rwkv_k_a[0], rwkv_r_k[0].reshape(-1), rwkv_ln_w[0],
                         rwkv_ln_b[0])
    kskw, vt = _nsa_keys(pt, rope, tk=512)
    cmp_rm, cmp_t = _nsa_compress(kcvc, nsa_cmp_pos[0], nsa_cmp_k_w1[0], nsa_cmp_k_w2[0],
                                  nsa_cmp_v_w1[0], nsa_cmp_v_w2[0])
    y_nsa = _nsa_attention(pt, rope, cmp_rm, cmp_t, kskw, vt, nsa_gate_b[0], nsa_out_g[0])
    mem_k, mem_vt = _mem_kv(mem, mem_norm_g[0], w_mem_kv[0])
    y_mem = _mem_attention(pt, mem_k, mem_vt, mem_out_g[0], tm=512)
    return _output_projection(x, y_rwkv, y_nsa, y_mem, w_out[0], norm_final_g, tm=512)
```

```python
import functools

import numpy as np
import jax
import jax.numpy as jnp
from jax import lax
from jax.experimental import pallas as pl
from jax.experimental.pallas import tpu as pltpu

F32 = jnp.float32
BF16 = jnp.bfloat16

HEAD_DIM = 64
RWKV_HEADS = 8
RWKV_WIDTH = RWKV_HEADS * HEAD_DIM
LORA = 64
RWKV_COLS = 4 * RWKV_WIDTH + 2 * LORA
RWKV_GN_EPS = 64e-5
NSA_HEADS = 4
NSA_WIDTH = NSA_HEADS * HEAD_DIM
N_BRANCH = 3
CMP_BLOCK = 32
CMP_STRIDE = 16
SEL_BLOCK = 64
SEL_TOPK = 16
WINDOW = 512
MEM_HEADS = 4
MEM_WIDTH = MEM_HEADS * HEAD_DIM
ROPE_THETA = 500000.0
ROPE_HALF = 8
Q_BLOCK = 128
NORM_EPS = 1e-6
NEG_INF = -1e30
FORCE_SCORE = 1e4

RWKV_CHUNK = 64
RWKV_HEAD_GROUP = 4
RWKV_BATCH_TILE = 2
SEL_KEY_CHUNK = 512
LANES = 128
VMEM_LIMIT = 48 * 1024 * 1024

def _dot(a, b):
    return jnp.dot(a, b, preferred_element_type=F32)


def _dot_nt(a, b):
    return lax.dot_general(a, b, (((1,), (1,)), ((), ())), preferred_element_type=F32)


def _dot_tn(a, b):
    return lax.dot_general(a, b, (((0,), (0,)), ((), ())), preferred_element_type=F32)


def _bf(a):
    return a.astype(BF16)


def _split_bf16(a, pieces):
    out = []
    for _ in range(pieces):
        part = a.astype(BF16)
        out.append(part)
        a = a - part.astype(F32)
    return out


def _sigmoid(x):
    return 1.0 / (1.0 + jnp.exp(-x))


def _silu(x):
    return x * _sigmoid(x)


def _params(*sem):
    return pltpu.CompilerParams(dimension_semantics=sem, vmem_limit_bytes=VMEM_LIMIT)


def _inproj_kernel(x_ref, g_ref, wrow_ref, wt_ref, prow_ref, kc_ref, pt_ref):
    x = x_ref[0]
    h = x * lax.rsqrt(jnp.mean(x * x, axis=-1, keepdims=True) + NORM_EPS) * g_ref[...]
    hb = h.astype(BF16)
    row = _dot(hb, wrow_ref[...])
    prow_ref[0] = row[:, :RWKV_COLS]
    kc_ref[0] = row[:, RWKV_COLS:]
    pt_ref[0] = _dot_nt(wt_ref[...], hb)


def _input_projection(x, g, w_row, w_t, tm):
    B, T, D = x.shape
    n_row = w_row.shape[1]
    n_t = w_t.shape[0]
    return pl.pallas_call(
        _inproj_kernel,
        grid=(B, T // tm),
        in_specs=[
            pl.BlockSpec((1, tm, D), lambda b, i: (b, i, 0)),
            pl.BlockSpec((1, D), lambda b, i: (0, 0)),
            pl.BlockSpec((D, n_row), lambda b, i: (0, 0)),
            pl.BlockSpec((n_t, D), lambda b, i: (0, 0)),
        ],
        out_specs=[
            pl.BlockSpec((1, tm, RWKV_COLS), lambda b, i: (b, i, 0)),
            pl.BlockSpec((1, tm, n_row - RWKV_COLS), lambda b, i: (b, i, 0)),
            pl.BlockSpec((1, n_t, tm), lambda b, i: (b, 0, i)),
        ],
        out_shape=[
            jax.ShapeDtypeStruct((B, T, RWKV_COLS), F32),
            jax.ShapeDtypeStruct((B, T, n_row - RWKV_COLS), F32),
            jax.ShapeDtypeStruct((B, n_t, T), F32),
        ],
        compiler_params=_params("parallel", "parallel"),
        name="input_projection",
    )(x, g, w_row, w_t)


def _rwkv_kernel(p_ref, prev_ref, *refs):
    n = pl.program_id(1)
    s_ref = refs[-1]

    @pl.when(n == 0)
    def _():
        s_ref[...] = jnp.zeros_like(s_ref)

    for b in range(p_ref.shape[0]):
        _rwkv_chunk(b, n, p_ref, prev_ref, *refs)


def _rwkv_chunk(b, n, p_ref, prev_ref, mu_ref, w0_ref, wup_ref, a0_ref, aup_ref, kk_ref, ka_ref,
                rk_ref, lnw_ref, lnb_ref, bd_ref, tri_ref, y_ref, s_ref):
    C, W, N, H = RWKV_CHUNK, RWKV_WIDTH, HEAD_DIM, RWKV_HEADS
    p = p_ref[b]
    prev_last = jnp.where(n == 0, 0.0, prev_ref[b][7:8, :])
    row = lax.broadcasted_iota(jnp.int32, (C, 1), 0)
    prev = jnp.where(row == 0, prev_last, pltpu.roll(p, 1, axis=0))
    pf = p + mu_ref[...] * (prev - p)
    r = pf[:, 0:W]
    k = pf[:, W:2 * W]
    v = pf[:, 2 * W:3 * W]
    gate = pf[:, 3 * W:4 * W]
    wd = pf[:, 4 * W:4 * W + LORA]
    ad = pf[:, 4 * W + LORA:4 * W + 2 * LORA]

    z = w0_ref[...] + _dot(_bf(jnp.tanh(wd)), wup_ref[...])
    w_log = -(jnp.maximum(-z, 0.0) + jnp.log(1.0 + jnp.exp(-jnp.abs(z)))) - 0.5
    logw = -jnp.exp(w_log)
    eta = _sigmoid(a0_ref[...] + _dot(_bf(ad), aup_ref[...]))
    kk = k * kk_ref[...]
    k2 = k * (1.0 + (eta - 1.0) * ka_ref[...])
    bd = bd_ref[...]
    head_sum = lambda a: sum(_dot(part, bd) for part in _split_bf16(a, 2))
    kk = kk / jnp.maximum(jnp.sqrt(head_sum(kk * kk)), 1e-12)
    bvec = kk * eta
    bonus = head_sum(r * k2 * rk_ref[...])

    tri = tri_ref[...]
    cs = sum(_dot(tri, part) for part in _split_bf16(logw, 3))
    mid = cs[C // 2 - 1:C // 2, :]
    csm = cs - mid
    g_mid = jnp.exp(mid)
    g_end = jnp.exp(cs[C - 1:C, :])
    g_end_mid = jnp.exp(cs[C - 1:C, :] - mid)
    e_in = jnp.exp(csm)
    e_out = jnp.exp(-csm)
    r_t = _bf(r * e_in)
    a_t = _bf(-kk * jnp.exp(csm - logw))
    k_t = _bf(k2 * e_out)
    b_t = _bf(bvec * e_out)
    vb = _bf(v)

    HG = RWKV_HEAD_GROUP
    GW = HG * N
    ri = lax.broadcasted_iota(jnp.int32, (GW, GW), 0)
    ci = lax.broadcasted_iota(jnp.int32, (GW, GW), 1)
    same_head = (ri // N) == (ci // N)
    strict = same_head & ((ri % N) > (ci % N))
    incl = same_head & ((ri % N) >= (ci % N))
    eye = (ri == ci).astype(F32)
    tile = lambda a: jnp.concatenate([a] * HG, axis=0)
    blockdiag = lambda a: jnp.where(same_head, tile(a), jnp.zeros((), a.dtype))

    ys = []
    for g in range(H // HG):
        ls = slice(g * GW, (g + 1) * GW)
        a_bd, r_bd = blockdiag(a_t[:, ls]), blockdiag(r_t[:, ls])
        kb = jnp.concatenate([blockdiag(k_t[:, ls]), blockdiag(b_t[:, ls])], axis=0)
        v_bd = blockdiag(vb[:, ls])
        gc = _dot_nt(jnp.concatenate([a_t[:, ls], r_t[:, ls]], axis=0), kb)
        a_ak = jnp.where(strict, tile(gc[0:C, 0:GW]), 0.0)
        a_ab = jnp.where(strict, tile(gc[0:C, GW:2 * GW]), 0.0)
        a_rk = jnp.where(incl, tile(gc[C:2 * C, 0:GW]), 0.0)
        a_rb = jnp.where(incl, tile(gc[C:2 * C, GW:2 * GW]), 0.0)
        tm = eye + a_ab
        pw = _dot(_bf(a_ab), _bf(a_ab))
        span = 2
        while span < C:
            if 2 * span < C:
                tp = _dot(_bf(pw), _bf(jnp.concatenate([tm, pw], axis=1)))
                tm = tm + tp[:, 0:GW]
                pw = tp[:, GW:2 * GW]
            else:
                tm = tm + _dot(_bf(pw), _bf(tm))
            span *= 2
        s0 = s_ref[b, g]
        uy0 = _dot_nt(jnp.concatenate([a_bd, r_bd], axis=0), _bf(s0 * g_mid[:, ls]))
        u = _dot(_bf(tm), _bf(uy0[0:GW] + _dot(_bf(a_ak), v_bd)))
        vu = jnp.concatenate([v_bd, _bf(u)], axis=0)
        y_bd = uy0[GW:2 * GW] + _dot(_bf(jnp.concatenate([a_rk, a_rb], axis=1)), vu)
        s_ref[b, g] = s0 * g_end[:, ls] + _dot_tn(vu, kb) * g_end_mid[:, ls]
        ys.append(sum(y_bd[h * N:(h + 1) * N] for h in range(HG)))
    y = jnp.concatenate(ys, axis=1)
    mean = head_sum(y) * (1.0 / N)
    var = head_sum(jnp.square(y - mean)) * (1.0 / N)
    y = (y - mean) * lax.rsqrt(var + RWKV_GN_EPS) * lnw_ref[...] + lnb_ref[...] + bonus * v
    y_ref[b] = (y * _silu(gate)).astype(y_ref.dtype)


def _rwkv_group(p_row, mu, w0, w_up, a0, a_up, k_k, k_a, r_k, ln_w, ln_b):
    B, T, _ = p_row.shape
    C, W = RWKV_CHUNK, RWKV_WIDTH
    heads = np.arange(W) // HEAD_DIM
    bd = jnp.asarray(heads[:, None] == heads[None, :], BF16)
    tri = jnp.asarray(np.tril(np.ones((C, C), np.float32)), BF16)
    vec = lambda a: a.reshape(1, -1)
    const = lambda shape: pl.BlockSpec(shape, lambda b, n: (0,) * len(shape))
    bt = RWKV_BATCH_TILE if B % RWKV_BATCH_TILE == 0 else 1
    return pl.pallas_call(
        _rwkv_kernel,
        grid=(B // bt, T // C),
        in_specs=[
            pl.BlockSpec((bt, C, RWKV_COLS), lambda b, n: (b, n, 0)),
            pl.BlockSpec((bt, 8, RWKV_COLS), lambda b, n: (b, jnp.maximum(n * (C // 8) - 1, 0), 0)),
            const((1, RWKV_COLS)), const((1, W)), const((LORA, W)), const((1, W)), const((LORA, W)),
            const((1, W)), const((1, W)), const((1, W)), const((1, W)), const((1, W)),
            const((W, W)), const((C, C)),
        ],
        out_specs=pl.BlockSpec((bt, C, W), lambda b, n: (b, n, 0)),
        out_shape=jax.ShapeDtypeStruct((B, T, W), BF16),
        scratch_shapes=[pltpu.VMEM((bt, RWKV_HEADS // RWKV_HEAD_GROUP) + (RWKV_HEAD_GROUP * HEAD_DIM,) * 2,
                                   F32)],
        compiler_params=_params("parallel", "arbitrary"),
        name="rwkv7_group",
    )(p_row, p_row, vec(mu), vec(w0), _bf(w_up), vec(a0), _bf(a_up), vec(k_k), vec(k_a), vec(r_k),
      vec(ln_w), vec(ln_b), bd, tri)


def _rope_rows(x, cos, sin):
    x1, x2 = x[0:ROPE_HALF], x[ROPE_HALF:2 * ROPE_HALF]
    return jnp.concatenate([x1 * cos - x2 * sin, x2 * cos + x1 * sin, x[2 * ROPE_HALF:]], axis=0)


def _nsa_keys_kernel(kv_ref, rope_ref, kskw_ref, vt_ref):
    kv = kv_ref[0]
    cos, sin = rope_ref[0:ROPE_HALF], rope_ref[ROPE_HALF:2 * ROPE_HALF]
    keys = jnp.concatenate([_rope_rows(kv[0:HEAD_DIM], cos, sin),
                            _rope_rows(kv[HEAD_DIM:2 * HEAD_DIM], cos, sin)], axis=0)
    kskw_ref[0] = keys.T.astype(BF16)
    vals = kv[2 * HEAD_DIM:4 * HEAD_DIM].astype(BF16)
    for u in range(vals.shape[1] // LANES):
        vt_ref[0, u] = vals[:, u * LANES:(u + 1) * LANES]


def _nsa_keys(pt, rope, tk):
    B, _, T = pt.shape
    return pl.pallas_call(
        _nsa_keys_kernel,
        grid=(B, T // tk),
        in_specs=[
            pl.BlockSpec((1, 4 * HEAD_DIM, tk), lambda b, i: (b, 0, i)),
            pl.BlockSpec((2 * ROPE_HALF, tk), lambda b, i: (0, i)),
        ],
        out_specs=[
            pl.BlockSpec((1, tk, LANES), lambda b, i: (b, i, 0)),
            pl.BlockSpec((1, tk // LANES, LANES, LANES), lambda b, i: (b, i, 0, 0)),
        ],
        out_shape=[
            jax.ShapeDtypeStruct((B, T, LANES), BF16),
            jax.ShapeDtypeStruct((B, T // LANES, LANES, LANES), BF16),
        ],
        compiler_params=_params("parallel", "parallel"),
        name="nsa_keys",
    )(pt, rope)


def _nsa_compress_kernel(g_ref, wc_ref, pos_ref, w2_ref, w2t_ref, rm_ref, t_ref):
    ng = g_ref.shape[1]
    wc = wc_ref[...]
    m = _dot(g_ref[0].astype(BF16), wc)
    pm = _dot(pos_ref[...], wc)
    pos_term = pm[0:1, 0:LANES] + pm[1:2, LANES:2 * LANES]
    pre = m[:, 0:LANES] + pltpu.roll(m[:, LANES:2 * LANES], ng - 1, axis=0) + pos_term
    act = _silu(pre).astype(BF16)
    row = lax.broadcasted_iota(jnp.int32, (ng, 1), 0)
    col = lax.broadcasted_iota(jnp.int32, (1, ng), 1)
    rm_ref[0] = jnp.where(row < ng - 1, _dot(act, w2_ref[...]), 0.0).astype(BF16)
    t_ref[0] = jnp.where(col < ng - 1, _dot_nt(w2t_ref[...], act), 0.0).astype(BF16)


def _nsa_compress(kcvc, cmp_pos, k_w1, k_w2, v_w1, v_w2):
    B, T, _ = kcvc.shape
    ng = T // CMP_STRIDE
    half = CMP_BLOCK // 2
    g = kcvc.reshape(B, ng, half * LANES)
    D = HEAD_DIM

    def spread(w, second, is_v):
        blk = w[second * half * D:(second + 1) * half * D].reshape(half, D, D)
        z = jnp.zeros_like(blk)
        return jnp.concatenate([z, blk] if is_v else [blk, z], axis=1).reshape(half * LANES, D)

    wc = jnp.concatenate([spread(k_w1, 0, False), spread(v_w1, 0, True),
                          spread(k_w1, 1, False), spread(v_w1, 1, True)], axis=1).astype(BF16)
    pos2 = jnp.concatenate([cmp_pos, cmp_pos], axis=1)
    pos = jnp.zeros((8, half * LANES), F32)
    pos = pos.at[0].set(pos2[:half].reshape(-1)).at[1].set(pos2[half:].reshape(-1)).astype(BF16)
    z = jnp.zeros((D, D), F32)
    w2 = jnp.block([[k_w2, z], [z, v_w2]])
    const = lambda shape: pl.BlockSpec(shape, lambda b: (0,) * len(shape))
    return pl.pallas_call(
        _nsa_compress_kernel,
        grid=(B,),
        in_specs=[
            pl.BlockSpec((1, ng, half * LANES), lambda b: (b, 0, 0)),
            const((half * LANES, 2 * LANES)), const((8, half * LANES)),
            const((LANES, LANES)), const((LANES, LANES)),
        ],
        out_specs=[
            pl.BlockSpec((1, ng, LANES), lambda b: (b, 0, 0)),
            pl.BlockSpec((1, LANES, ng), lambda b: (b, 0, 0)),
        ],
        out_shape=[
            jax.ShapeDtypeStruct((B, ng, LANES), BF16),
            jax.ShapeDtypeStruct((B, LANES, ng), BF16),
        ],
        compiler_params=_params("parallel"),
        name="nsa_compress",
    )(g, wc, pos, w2.astype(BF16), w2.T.astype(BF16))


def _tile_heads(x):
    return jnp.concatenate([x] * NSA_HEADS, axis=1)


def _nsa_attn_kernel(q_ref, gate_ref, glog_ref, rope_ref, rm_ref, ct_ref, kskw_ref, vt_ref,
                     ov_ref, gb_ref, og_ref, y_ref, sel_ref, m_ref, l_ref, acc_ref):
    D, Q, Hn = HEAD_DIM, Q_BLOCK, NSA_HEADS
    QH = Q * Hn
    KC = SEL_KEY_CHUNK
    ng = rm_ref.shape[1]
    ns = ov_ref.shape[0]
    n_top = min(SEL_TOPK, ns)
    i = pl.program_id(1)
    t0 = i * Q
    tq = t0 + lax.broadcasted_iota(jnp.int32, (1, Q), 1)

    q = q_ref[0] * (D ** -0.5)
    cos, sin = rope_ref[0:ROPE_HALF], rope_ref[ROPE_HALF:2 * ROPE_HALF]
    qh = [q[h * D:(h + 1) * D] for h in range(Hn)]
    q4 = jnp.concatenate(qh, axis=1)
    q4r = jnp.concatenate([_rope_rows(x, cos, sin) for x in qh], axis=1)
    zero = jnp.zeros_like(q4)
    q_lo = jnp.concatenate([q4, zero], axis=0).astype(BF16)
    qr_lo = jnp.concatenate([q4r, zero], axis=0).astype(BF16)
    qr_hi = jnp.concatenate([zero, q4r], axis=0).astype(BF16)
    tq4 = _tile_heads(tq)

    s = _dot(rm_ref[0], q_lo)
    cend = lax.broadcasted_iota(jnp.int32, (ng, 1), 0) * CMP_STRIDE + (CMP_BLOCK - 1)
    mask = cend <= tq4
    s = jnp.where(mask, s, NEG_INF)
    e = jnp.where(mask, jnp.exp(s - jnp.max(s, axis=0, keepdims=True)), 0.0)
    p = e / jnp.maximum(jnp.sum(e, axis=0, keepdims=True), 1e-30)
    o_c = _dot(ct_ref[0][D:2 * D], p.astype(BF16))
    psum = p[:, 0:Q]
    for h in range(1, Hn):
        psum = psum + p[:, h * Q:(h + 1) * Q]
    imp = _dot(ov_ref[...], psum.astype(BF16))

    blk = lax.broadcasted_iota(jnp.int32, (ns, Q), 0)
    cur = tq // SEL_BLOCK
    forced = (blk == 0) | (blk == cur) | (blk == cur - 1)
    score = jnp.where(forced, FORCE_SCORE, jnp.where(blk <= cur, imp, -1.0))
    taken = -2.0
    for _ in range(n_top):
        best = jnp.max(score, axis=0, keepdims=True)
        first = jnp.min(jnp.where(score == best, blk, ns), axis=0, keepdims=True)
        score = jnp.where(blk == first, taken, score)
    sel_ref[...] = (score == taken).astype(F32)

    m_ref[...] = jnp.full_like(m_ref, NEG_INF)
    l_ref[...] = jnp.zeros_like(l_ref)
    acc_ref[...] = jnp.zeros_like(acc_ref)
    krow = lax.broadcasted_iota(jnp.int32, (KC, 1), 0)

    def sel_step(j, carry):
        k0 = pl.multiple_of(j * KC, KC)
        s = _dot(kskw_ref[0, pl.ds(k0, KC), :], qr_lo)
        rows = [jnp.broadcast_to(sel_ref[pl.ds(j * (KC // SEL_BLOCK) + b, 1), :], (SEL_BLOCK, Q))
                for b in range(KC // SEL_BLOCK)]
        keep = jnp.where(k0 + krow <= tq, jnp.concatenate(rows, axis=0), 0.0)
        mask = _tile_heads(keep) > 0.5
        s = jnp.where(mask, s, NEG_INF)
        m_old = m_ref[...]
        m_new = jnp.maximum(m_old, jnp.max(s, axis=0, keepdims=True))
        alpha = jnp.exp(m_old - m_new)
        e = jnp.where(mask, jnp.exp(s - m_new), 0.0)
        l_ref[...] = alpha * l_ref[...] + jnp.sum(e, axis=0, keepdims=True)
        eb = e.astype(BF16)
        pv = jnp.zeros((D, QH), F32)
        for u in range(KC // LANES):
            pv = pv + _dot(vt_ref[0, j * (KC // LANES) + u][0:D], eb[u * LANES:(u + 1) * LANES])
        acc_ref[...] = acc_ref[...] * alpha + pv
        m_ref[...] = m_new
        return carry

    lax.fori_loop(0, (t0 + Q + KC - 1) // KC, sel_step, 0)
    o_s = acc_ref[...] / jnp.maximum(l_ref[...], 1e-30)

    WK = WINDOW + Q
    w0 = pl.multiple_of(jnp.maximum(t0 - WINDOW, 0), LANES)
    s = _dot(kskw_ref[0, pl.ds(w0, WK), :], qr_hi)
    diff = tq4 - (w0 + lax.broadcasted_iota(jnp.int32, (WK, 1), 0))
    mask = (diff >= 0) & (diff < WINDOW)
    s = jnp.where(mask, s, NEG_INF)
    e = jnp.where(mask, jnp.exp(s - jnp.max(s, axis=0, keepdims=True)), 0.0)
    pb = (e / jnp.maximum(jnp.sum(e, axis=0, keepdims=True), 1e-30)).astype(BF16)
    o_w = jnp.zeros((D, QH), F32)
    for u in range(WK // LANES):
        o_w = o_w + _dot(vt_ref[0, w0 // LANES + u][D:2 * D], pb[u * LANES:(u + 1) * LANES])

    gl = _sigmoid(glog_ref[0] + gb_ref[...])
    ys = []
    for h in range(Hn):
        cs = slice(h * Q, (h + 1) * Q)
        o = (gl[3 * h:3 * h + 1] * o_c[:, cs] + gl[3 * h + 1:3 * h + 2] * o_s[:, cs]
             + gl[3 * h + 2:3 * h + 3] * o_w[:, cs])
        o = o * lax.rsqrt(jnp.mean(o * o, axis=0, keepdims=True) + NORM_EPS)
        ys.append(o)
    y = jnp.concatenate(ys, axis=0) * og_ref[...] * _silu(gate_ref[0])
    y_ref[0] = y.T.astype(y_ref.dtype)


def _nsa_attention(pt, rope, cmp_rm, cmp_t, kskw, vt, gate_b, out_g):
    B, _, T = pt.shape
    ng = T // CMP_STRIDE
    ns = T // SEL_BLOCK
    Q, W = Q_BLOCK, NSA_WIDTH
    QH = Q * NSA_HEADS
    c0 = np.arange(ng)[None, :] * CMP_STRIDE
    s0 = np.arange(ns)[:, None] * SEL_BLOCK
    ov = np.clip(np.minimum(c0 + CMP_BLOCK, s0 + SEL_BLOCK) - np.maximum(c0, s0), 0, None) / CMP_BLOCK
    ov[:, ng - 1] = 0.0
    gb = jnp.zeros((16, 1), F32).at[:NSA_HEADS * N_BRANCH, 0].set(gate_b)
    glog_blk = (4 * HEAD_DIM + 4 * W) // 16
    return pl.pallas_call(
        _nsa_attn_kernel,
        grid=(B, T // Q),
        in_specs=[
            pl.BlockSpec((1, W, Q), lambda b, i: (b, 1, i)),
            pl.BlockSpec((1, W, Q), lambda b, i: (b, 2, i)),
            pl.BlockSpec((1, 16, Q), lambda b, i: (b, glog_blk, i)),
            pl.BlockSpec((2 * ROPE_HALF, Q), lambda b, i: (0, i)),
            pl.BlockSpec((1, ng, LANES), lambda b, i: (b, 0, 0)),
            pl.BlockSpec((1, LANES, ng), lambda b, i: (b, 0, 0)),
            pl.BlockSpec((1, T, LANES), lambda b, i: (b, 0, 0)),
            pl.BlockSpec((1, T // LANES, LANES, LANES), lambda b, i: (b, 0, 0, 0)),
            pl.BlockSpec((ns, ng), lambda b, i: (0, 0)),
            pl.BlockSpec((16, 1), lambda b, i: (0, 0)),
            pl.BlockSpec((W, 1), lambda b, i: (0, 0)),
        ],
        out_specs=pl.BlockSpec((1, Q, W), lambda b, i: (b, i, 0)),
        out_shape=jax.ShapeDtypeStruct((B, T, W), BF16),
        scratch_shapes=[pltpu.VMEM((ns, Q), F32), pltpu.VMEM((1, QH), F32),
                        pltpu.VMEM((1, QH), F32), pltpu.VMEM((HEAD_DIM, QH), F32)],
        compiler_params=_params("parallel", "arbitrary"),
        name="nsa_attention",
    )(pt, pt, pt, rope, cmp_rm, cmp_t, kskw, vt, jnp.asarray(ov, BF16), gb, out_g.reshape(W, 1))


def _mem_kv_kernel(mem_ref, g_ref, w_ref, wt_ref, k_ref, vt_ref):
    x = mem_ref[0]
    h = x * lax.rsqrt(jnp.mean(x * x, axis=-1, keepdims=True) + NORM_EPS) * g_ref[...]
    hb = h.astype(BF16)
    k_ref[0] = _dot(hb, w_ref[...]).astype(BF16)
    vt_ref[0] = _dot_nt(wt_ref[...], hb).astype(BF16)


def _mem_kv(mem, g, w_kv):
    B, M, D = mem.shape
    W = MEM_WIDTH
    const = lambda shape: pl.BlockSpec(shape, lambda b: (0,) * len(shape))
    return pl.pallas_call(
        _mem_kv_kernel,
        grid=(B,),
        in_specs=[pl.BlockSpec((1, M, D), lambda b: (b, 0, 0)), const((1, D)), const((D, W)),
                  const((W, D))],
        out_specs=[pl.BlockSpec((1, M, W), lambda b: (b, 0, 0)),
                   pl.BlockSpec((1, W, M), lambda b: (b, 0, 0))],
        out_shape=[jax.ShapeDtypeStruct((B, M, W), BF16), jax.ShapeDtypeStruct((B, W, M), BF16)],
        compiler_params=_params("parallel"),
        name="mem_kv",
    )(mem, g.reshape(1, D), w_kv[:, :W].astype(BF16), w_kv[:, W:].T.astype(BF16))


def _mem_attn_kernel(q_ref, gate_ref, k_ref, vt_ref, og_ref, y_ref):
    D, Hm = HEAD_DIM, MEM_HEADS
    q = q_ref[0] * (D ** -0.5)
    k = k_ref[0]
    vt = vt_ref[0]
    row = lax.broadcasted_iota(jnp.int32, (Hm * D, 1), 0)
    ys = []
    for h in range(Hm):
        qh = jnp.where(row // D == h, q, 0.0).astype(BF16)
        s = _dot(k, qh)
        e = jnp.exp(s - jnp.max(s, axis=0, keepdims=True))
        p = e / jnp.sum(e, axis=0, keepdims=True)
        o = _dot(vt[h * D:(h + 1) * D], p.astype(BF16))
        ys.append(o * lax.rsqrt(jnp.mean(o * o, axis=0, keepdims=True) + NORM_EPS))
    y = jnp.concatenate(ys, axis=0) * og_ref[...] * _silu(gate_ref[0])
    y_ref[0] = y.T.astype(y_ref.dtype)


def _mem_attention(pt, mem_k, mem_vt, out_g, tm):
    B, _, T = pt.shape
    M = mem_k.shape[1]
    W = MEM_WIDTH
    return pl.pallas_call(
        _mem_attn_kernel,
        grid=(B, T // tm),
        in_specs=[
            pl.BlockSpec((1, W, tm), lambda b, i: (b, 3, i)),
            pl.BlockSpec((1, W, tm), lambda b, i: (b, 4, i)),
            pl.BlockSpec((1, M, W), lambda b, i: (b, 0, 0)),
            pl.BlockSpec((1, W, M), lambda b, i: (b, 0, 0)),
            pl.BlockSpec((W, 1), lambda b, i: (0, 0)),
        ],
        out_specs=pl.BlockSpec((1, tm, W), lambda b, i: (b, i, 0)),
        out_shape=jax.ShapeDtypeStruct((B, T, W), BF16),
        compiler_params=_params("parallel", "parallel"),
        name="mem_attention",
    )(pt, pt, mem_k, mem_vt, out_g.reshape(W, 1))


def _outproj_kernel(x_ref, yr_ref, yn_ref, ym_ref, wr_ref, wn_ref, wm_ref, g_ref, o_ref):
    z = (x_ref[0] + _dot(yr_ref[0], wr_ref[...]) + _dot(yn_ref[0], wn_ref[...])
         + _dot(ym_ref[0], wm_ref[...]))
    o_ref[0] = z * lax.rsqrt(jnp.mean(z * z, axis=-1, keepdims=True) + NORM_EPS) * g_ref[...]


def _output_projection(x, y_rwkv, y_nsa, y_mem, w_out, g, tm):
    B, T, D = x.shape
    wb = w_out.astype(BF16)
    w_r, w_n, w_m = wb[:RWKV_WIDTH], wb[RWKV_WIDTH:RWKV_WIDTH + NSA_WIDTH], wb[RWKV_WIDTH + NSA_WIDTH:]
    tile = lambda w: pl.BlockSpec((1, tm, w), lambda b, i: (b, i, 0))
    const = lambda shape: pl.BlockSpec(shape, lambda b, i: (0,) * len(shape))
    return pl.pallas_call(
        _outproj_kernel,
        grid=(B, T // tm),
        in_specs=[tile(D), tile(RWKV_WIDTH), tile(NSA_WIDTH), tile(MEM_WIDTH),
                  const((RWKV_WIDTH, D)), const((NSA_WIDTH, D)), const((MEM_WIDTH, D)), const((1, D))],
        out_specs=tile(D),
        out_shape=jax.ShapeDtypeStruct((B, T, D), F32),
        compiler_params=_params("parallel", "parallel"),
        name="output_projection",
    )(x, y_rwkv, y_nsa, y_mem, w_r, w_n, w_m, g.reshape(1, D))


def _rope_table(T):
    inv_freq = ROPE_THETA ** (-jnp.arange(ROPE_HALF, dtype=F32) / ROPE_HALF)
    ang = inv_freq[:, None] * jnp.arange(T).astype(F32)[None, :]
    return jnp.concatenate([jnp.cos(ang), jnp.sin(ang)], axis=0)


def _split_w_in(w):
    D = HEAD_DIM
    n0 = RWKV_COLS
    q, gate, glog = n0, n0 + NSA_WIDTH, n0 + 2 * NSA_WIDTH
    kc = glog + NSA_HEADS * N_BRANCH
    vc, ks, vs, kw, vw = kc + D, kc + 2 * D, kc + 3 * D, kc + 4 * D, kc + 5 * D
    m0 = vw + D
    cols = lambda a, n: w[:, a:a + n]
    w_row = jnp.concatenate([cols(0, n0), cols(kc, D), cols(vc, D)], axis=1)
    w_t = jnp.concatenate([cols(ks, D), cols(kw, D), cols(vs, D), cols(vw, D), cols(q, NSA_WIDTH),
                           cols(gate, NSA_WIDTH), cols(m0, MEM_WIDTH), cols(m0 + MEM_WIDTH, MEM_WIDTH),
                           cols(glog, NSA_HEADS * N_BRANCH),
                           jnp.zeros((w.shape[0], 16 - NSA_HEADS * N_BRANCH), w.dtype)], axis=1)
    return w_row.astype(BF16), w_t.T.astype(BF16)


def kernel(x, mem, norm_in_g, w_in, rwkv_mu, rwkv_w0, rwkv_w_up, rwkv_a0, rwkv_a_up, rwkv_k_k,
           rwkv_k_a, rwkv_r_k, rwkv_ln_w, rwkv_ln_b, nsa_cmp_pos, nsa_cmp_k_w1, nsa_cmp_k_w2,
           nsa_cmp_v_w1, nsa_cmp_v_w2, nsa_gate_b, nsa_out_g, mem_norm_g, w_mem_kv, mem_out_g, w_out,
           norm_final_g):
    B, T, D = x.shape
    assert w_in.shape[0] == 1, "single-layer stack: the final norm is fused into the output projection"
    rope = _rope_table(T)
    w_row, w_t = _split_w_in(w_in[0])
    p_row, kcvc, pt = _input_projection(x, norm_in_g[0].reshape(1, D), w_row, w_t, tm=256)
    y_rwkv = _rwkv_group(p_row, rwkv_mu[0], rwkv_w0[0], rwkv_w_up[0], rwkv_a0[0], rwkv_a_up[0],
                         rwkv_k_k[0], rwkv_k_a[0], rwkv_r_k[0].reshape(-1), rwkv_ln_w[0],
                         rwkv_ln_b[0])
    kskw, vt = _nsa_keys(pt, rope, tk=512)
    cmp_rm, cmp_t = _nsa_compress(kcvc, nsa_cmp_pos[0], nsa_cmp_k_w1[0], nsa_cmp_k_w2[0],
                                  nsa_cmp_v_w1[0], nsa_cmp_v_w2[0])
    y_nsa = _nsa_attention(pt, rope, cmp_rm, cmp_t, kskw, vt, nsa_gate_b[0], nsa_out_g[0])
    mem_k, mem_vt = _mem_kv(mem, mem_norm_g[0], w_mem_kv[0])
    y_mem = _mem_attention(pt, mem_k, mem_vt, mem_out_g[0], tm=512)
    return _output_projection(x, y_rwkv, y_nsa, y_mem, w_out[0], norm_final_g, tm=512)
```

```python
import functools

import numpy as np
import jax
import jax.numpy as jnp
from jax import lax
from jax.experimental import pallas as pl
from jax.experimental.pallas import tpu as pltpu

F32 = jnp.float32
BF16 = jnp.bfloat16

HEAD_DIM = 64
RWKV_HEADS = 8
RWKV_WIDTH = RWKV_HEADS * HEAD_DIM
LORA = 64
RWKV_COLS = 4 * RWKV_WIDTH + 2 * LORA
RWKV_GN_EPS = 64e-5
NSA_HEADS = 4
NSA_WIDTH = NSA_HEADS * HEAD_DIM
N_BRANCH = 3
CMP_BLOCK = 32
CMP_STRIDE = 16
SEL_BLOCK = 64
SEL_TOPK = 16
WINDOW = 512
MEM_HEADS = 4
MEM_WIDTH = MEM_HEADS * HEAD_DIM
ROPE_THETA = 500000.0
ROPE_HALF = 8
Q_BLOCK = 128
NORM_EPS = 1e-6
NEG_INF = -1e30
FORCE_SCORE = 1e4
LOG2E = 1.4426950408889634

RWKV_CHUNK = 64
RWKV_HEAD_GROUP = 4
RWKV_BATCH_TILE = 4
RWKV_CHAIN_BATCH = 2
SEL_KEY_CHUNK = 512
LANES = 128
ONES_ROWS = 16
VAL_ROWS = HEAD_DIM + ONES_ROWS
VMEM_LIMIT = 48 * 1024 * 1024

def _dot(a, b):
    return jnp.dot(a, b, preferred_element_type=F32)


def _dot_nt(a, b):
    return lax.dot_general(a, b, (((1,), (1,)), ((), ())), preferred_element_type=F32)


def _dot_tn(a, b):
    return lax.dot_general(a, b, (((0,), (0,)), ((), ())), preferred_element_type=F32)


def _bf(a):
    return a.astype(BF16)


def _split_bf16(a, pieces):
    out = []
    for _ in range(pieces):
        part = a.astype(BF16)
        out.append(part)
        a = a - part.astype(F32)
    return out


def _sigmoid(x):
    return 1.0 / (1.0 + jnp.exp(-x))


def _silu(x):
    return x * _sigmoid(x)


def _params(*sem):
    return pltpu.CompilerParams(dimension_semantics=sem, vmem_limit_bytes=VMEM_LIMIT)


def _inproj_kernel(x_ref, g_ref, wrow_ref, wt_ref, prow_ref, kc_ref, pt_ref):
    x = x_ref[0]
    h = x * lax.rsqrt(jnp.mean(x * x, axis=-1, keepdims=True) + NORM_EPS) * g_ref[...]
    hb = h.astype(BF16)
    row = _dot(hb, wrow_ref[...])
    prow_ref[0] = row[:, :RWKV_COLS]
    kc_ref[0] = row[:, RWKV_COLS:]
    pt_ref[0] = _dot_nt(wt_ref[...], hb)


def _input_projection(x, g, w_row, w_t, tm):
    B, T, D = x.shape
    n_row = w_row.shape[1]
    n_t = w_t.shape[0]
    return pl.pallas_call(
        _inproj_kernel,
        grid=(B, T // tm),
        in_specs=[
            pl.BlockSpec((1, tm, D), lambda b, i: (b, i, 0)),
            pl.BlockSpec((1, D), lambda b, i: (0, 0)),
            pl.BlockSpec((D, n_row), lambda b, i: (0, 0)),
            pl.BlockSpec((n_t, D), lambda b, i: (0, 0)),
        ],
        out_specs=[
            pl.BlockSpec((1, tm, RWKV_COLS), lambda b, i: (b, i, 0)),
            pl.BlockSpec((1, tm, n_row - RWKV_COLS), lambda b, i: (b, i, 0)),
            pl.BlockSpec((1, n_t, tm), lambda b, i: (b, 0, i)),
        ],
        out_shape=[
            jax.ShapeDtypeStruct((B, T, RWKV_COLS), F32),
            jax.ShapeDtypeStruct((B, T, n_row - RWKV_COLS), F32),
            jax.ShapeDtypeStruct((B, n_t, T), F32),
        ],
        compiler_params=_params("parallel", "parallel"),
        name="input_projection",
    )(x, g, w_row, w_t)


def _rwkv_kernel(p_ref, prev_ref, mu_ref, w0_ref, wup_ref, a0_ref, aup_ref, kk_ref, ka_ref,
                 rk_ref, lnw_ref, lnb_ref, bd_ref, tri_ref, y_ref, s_ref):
    C, W, N, H = RWKV_CHUNK, RWKV_WIDTH, HEAD_DIM, RWKV_HEADS
    HG = RWKV_HEAD_GROUP
    GW = HG * N
    n = pl.program_id(1)

    @pl.when(n == 0)
    def _():
        s_ref[...] = jnp.zeros_like(s_ref)

    bd = bd_ref[...]
    head_sum = lambda a: jnp.concatenate(
        [_dot(_bf(a[:, g * GW:(g + 1) * GW]), bd) for g in range(H // HG)], axis=1)
    row = lax.broadcasted_iota(jnp.int32, (C, 1), 0)

    def prepare(b):
        p = p_ref[b]
        prev_last = jnp.where(n == 0, 0.0, prev_ref[b][7:8, :])
        prev = jnp.where(row == 0, prev_last, pltpu.roll(p, 1, axis=0))
        pf = p + mu_ref[...] * (prev - p)
        r = pf[:, 0:W]
        k = pf[:, W:2 * W]
        v = pf[:, 2 * W:3 * W]
        gate = pf[:, 3 * W:4 * W]
        wd = pf[:, 4 * W:4 * W + LORA]
        ad = pf[:, 4 * W + LORA:4 * W + 2 * LORA]
        z = w0_ref[...] + _dot(_bf(jnp.tanh(wd)), wup_ref[...])
        w_log = -(jnp.maximum(-z, 0.0) + jnp.log(1.0 + jnp.exp(-jnp.abs(z)))) - 0.5
        logw = -jnp.exp(w_log)
        eta = _sigmoid(a0_ref[...] + _dot(_bf(ad), aup_ref[...]))
        kk = k * kk_ref[...]
        k2 = k * (1.0 + (eta - 1.0) * ka_ref[...])
        kk = kk / jnp.maximum(jnp.sqrt(head_sum(kk * kk)), 1e-12)
        tri = tri_ref[...]
        cs = sum(_dot(tri, part) for part in _split_bf16(logw, 3))
        mid = cs[C // 2 - 1:C // 2, :]
        csm = cs - mid
        e_out = jnp.exp(-csm)
        return dict(
            v=v, gate=gate, bonus=head_sum(r * k2 * rk_ref[...]),
            g_mid=jnp.exp(mid), g_end=jnp.exp(cs[C - 1:C, :]), g_end_mid=jnp.exp(cs[C - 1:C, :] - mid),
            r=_bf(r * jnp.exp(csm)), a=_bf(-kk * jnp.exp(csm - logw)),
            k=_bf(k2 * e_out), b=_bf(kk * eta * e_out), vb=_bf(v))

    li = lax.broadcasted_iota(jnp.int32, (C, GW), 1) % N
    ti = lax.broadcasted_iota(jnp.int32, (C, GW), 0)
    strict, incl = ti > li, ti >= li
    eye = (ti == li).astype(F32)
    same_head = (lax.broadcasted_iota(jnp.int32, (GW, GW), 0) // N
                 == lax.broadcasted_iota(jnp.int32, (GW, GW), 1) // N)

    def blockdiag(a):
        a = _bf(a)
        return jnp.where(same_head, jnp.concatenate([a] * HG, axis=0), jnp.zeros((), BF16))

    def advance(batches):
        pre = {b: prepare(b) for b in batches}
        chains = [(b, g) for b in batches for g in range(H // HG)]
        st = {}
        for c in chains:
            b, g = c
            ls = slice(g * GW, (g + 1) * GW)
            d = pre[b]
            kb = jnp.concatenate([blockdiag(d["k"][:, ls]), blockdiag(d["b"][:, ls])], axis=0)
            ar = jnp.concatenate([d["a"][:, ls], d["r"][:, ls]], axis=0)
            gc = _dot_nt(ar, kb)
            s0 = s_ref[b, g]
            st[c] = dict(
                ls=ls, kb=kb, s0=s0, v_bd=blockdiag(d["vb"][:, ls]),
                a_ak=jnp.where(strict, gc[0:C, 0:GW], 0.0), a_ab=jnp.where(strict, gc[0:C, GW:2 * GW], 0.0),
                a_rk=jnp.where(incl, gc[C:2 * C, 0:GW], 0.0), a_rb=jnp.where(incl, gc[C:2 * C, GW:2 * GW], 0.0),
                uy0=_dot_nt(ar, _bf(s0 * d["g_mid"][:, ls])))
        for c in chains:
            s = st[c]
            s["rhs"] = s["uy0"][0:C] + _dot(_bf(s["a_ak"]), s["v_bd"])
            s["tm"] = eye + s["a_ab"]
            s["pw"] = _dot(_bf(s["a_ab"]), blockdiag(s["a_ab"]))
        span = 2
        while span < C:
            for c in chains:
                s = st[c]
                if 2 * span < C:
                    tp = _dot(_bf(s["pw"]), jnp.concatenate([blockdiag(s["tm"]), blockdiag(s["pw"])], axis=1))
                    s["tm"] = s["tm"] + tp[:, 0:GW]
                    s["pw"] = tp[:, GW:2 * GW]
                else:
                    s["tm"] = s["tm"] + _dot(_bf(s["pw"]), blockdiag(s["tm"]))
            span *= 2
        for c in chains:
            s = st[c]
            s["u"] = _dot(_bf(s["tm"]), blockdiag(s["rhs"]))
        for c in chains:
            s = st[c]
            s["vu"] = jnp.concatenate([s["v_bd"], blockdiag(s["u"])], axis=0)
            s["y"] = s["uy0"][C:2 * C] + _dot(_bf(jnp.concatenate([s["a_rk"], s["a_rb"]], axis=1)), s["vu"])
        for c in chains:
            b, g = c
            s, d = st[c], pre[b]
            s_ref[b, g] = s["s0"] * d["g_end"][:, s["ls"]] + _dot_tn(s["vu"], s["kb"]) * d["g_end_mid"][:, s["ls"]]
        for b in batches:
            d = pre[b]
            y = jnp.concatenate([st[(b, g)]["y"] for g in range(H // HG)], axis=1)
            mean = head_sum(y) * (1.0 / N)
            var = head_sum(jnp.square(y - mean)) * (1.0 / N)
            y = (y - mean) * lax.rsqrt(var + RWKV_GN_EPS) * lnw_ref[...] + lnb_ref[...] + d["bonus"] * d["v"]
            y_ref[b] = (y * _silu(d["gate"])).astype(y_ref.dtype)

    bt = p_ref.shape[0]
    for b0 in range(0, bt, RWKV_CHAIN_BATCH):
        advance(range(b0, min(b0 + RWKV_CHAIN_BATCH, bt)))


def _rwkv_group(p_row, mu, w0, w_up, a0, a_up, k_k, k_a, r_k, ln_w, ln_b):
    B, T, _ = p_row.shape
    C, W = RWKV_CHUNK, RWKV_WIDTH
    heads = np.arange(RWKV_HEAD_GROUP * HEAD_DIM) // HEAD_DIM
    bd = jnp.asarray(heads[:, None] == heads[None, :], BF16)
    tri = jnp.asarray(np.tril(np.ones((C, C), np.float32)), BF16)
    vec = lambda a: a.reshape(1, -1)
    const = lambda shape: pl.BlockSpec(shape, lambda b, n: (0,) * len(shape))
    bt = RWKV_BATCH_TILE if B % RWKV_BATCH_TILE == 0 else 1
    return pl.pallas_call(
        _rwkv_kernel,
        grid=(B // bt, T // C),
        in_specs=[
            pl.BlockSpec((bt, C, RWKV_COLS), lambda b, n: (b, n, 0)),
            pl.BlockSpec((bt, 8, RWKV_COLS), lambda b, n: (b, jnp.maximum(n * (C // 8) - 1, 0), 0)),
            const((1, RWKV_COLS)), const((1, W)), const((LORA, W)), const((1, W)), const((LORA, W)),
            const((1, W)), const((1, W)), const((1, W)), const((1, W)), const((1, W)),
            const(bd.shape), const((C, C)),
        ],
        out_specs=pl.BlockSpec((bt, C, W), lambda b, n: (b, n, 0)),
        out_shape=jax.ShapeDtypeStruct((B, T, W), BF16),
        scratch_shapes=[pltpu.VMEM((bt, RWKV_HEADS // RWKV_HEAD_GROUP) + (RWKV_HEAD_GROUP * HEAD_DIM,) * 2,
                                   F32)],
        compiler_params=_params("parallel", "arbitrary"),
        name="rwkv7_group",
    )(p_row, p_row, vec(mu), vec(w0), _bf(w_up), vec(a0), _bf(a_up), vec(k_k), vec(k_a), vec(r_k),
      vec(ln_w), vec(ln_b), bd, tri)


def _rope_rows(x, cos, sin):
    x1, x2 = x[0:ROPE_HALF], x[ROPE_HALF:2 * ROPE_HALF]
    return jnp.concatenate([x1 * cos - x2 * sin, x2 * cos + x1 * sin, x[2 * ROPE_HALF:]], axis=0)


def _nsa_keys_kernel(kv_ref, rope_ref, kskw_ref, vt_ref):
    kv = kv_ref[0]
    cos, sin = rope_ref[0:ROPE_HALF], rope_ref[ROPE_HALF:2 * ROPE_HALF]
    keys = jnp.concatenate([_rope_rows(kv[0:HEAD_DIM], cos, sin),
                            _rope_rows(kv[HEAD_DIM:2 * HEAD_DIM], cos, sin)], axis=0)
    kskw_ref[0] = keys.T.astype(BF16)
    vals = kv[2 * HEAD_DIM:4 * HEAD_DIM].astype(BF16)
    ones = jnp.ones((ONES_ROWS, LANES), BF16)
    for u in range(vals.shape[1] // LANES):
        blk = vals[:, u * LANES:(u + 1) * LANES]
        vt_ref[0, u] = jnp.concatenate([blk[0:HEAD_DIM], ones, blk[HEAD_DIM:2 * HEAD_DIM], ones], axis=0)


def _nsa_keys(pt, rope, tk):
    B, _, T = pt.shape
    return pl.pallas_call(
        _nsa_keys_kernel,
        grid=(B, T // tk),
        in_specs=[
            pl.BlockSpec((1, 4 * HEAD_DIM, tk), lambda b, i: (b, 0, i)),
            pl.BlockSpec((2 * ROPE_HALF, tk), lambda b, i: (0, i)),
        ],
        out_specs=[
            pl.BlockSpec((1, tk, LANES), lambda b, i: (b, i, 0)),
            pl.BlockSpec((1, tk // LANES, 2 * VAL_ROWS, LANES), lambda b, i: (b, i, 0, 0)),
        ],
        out_shape=[
            jax.ShapeDtypeStruct((B, T, LANES), BF16),
            jax.ShapeDtypeStruct((B, T // LANES, 2 * VAL_ROWS, LANES), BF16),
        ],
        compiler_params=_params("parallel", "parallel"),
        name="nsa_keys",
    )(pt, rope)


def _nsa_compress_kernel(g_ref, wc_ref, pos_ref, w2_ref, w2t_ref, rm_ref, t_ref):
    ng = g_ref.shape[1]
    wc = wc_ref[...]
    m = _dot(g_ref[0].astype(BF16), wc)
    pm = _dot(pos_ref[...], wc)
    pos_term = pm[0:1, 0:LANES] + pm[1:2, LANES:2 * LANES]
    pre = m[:, 0:LANES] + pltpu.roll(m[:, LANES:2 * LANES], ng - 1, axis=0) + pos_term
    act = _silu(pre).astype(BF16)
    row = lax.broadcasted_iota(jnp.int32, (ng, 1), 0)
    col = lax.broadcasted_iota(jnp.int32, (1, ng), 1)
    rm_ref[0] = jnp.where(row < ng - 1, _dot(act, w2_ref[...]), 0.0).astype(BF16)
    vt = jnp.where(col < ng - 1, _dot_nt(w2t_ref[...], act)[HEAD_DIM:2 * HEAD_DIM], 0.0)
    t_ref[0] = jnp.concatenate([vt, jnp.ones((ONES_ROWS, ng), F32)], axis=0).astype(BF16)


def _nsa_compress(kcvc, cmp_pos, k_w1, k_w2, v_w1, v_w2):
    B, T, _ = kcvc.shape
    ng = T // CMP_STRIDE
    half = CMP_BLOCK // 2
    g = kcvc.reshape(B, ng, half * LANES)
    D = HEAD_DIM

    def spread(w, second, is_v):
        blk = w[second * half * D:(second + 1) * half * D].reshape(half, D, D)
        z = jnp.zeros_like(blk)
        return jnp.concatenate([z, blk] if is_v else [blk, z], axis=1).reshape(half * LANES, D)

    wc = jnp.concatenate([spread(k_w1, 0, False), spread(v_w1, 0, True),
                          spread(k_w1, 1, False), spread(v_w1, 1, True)], axis=1).astype(BF16)
    pos2 = jnp.concatenate([cmp_pos, cmp_pos], axis=1)
    pos = jnp.zeros((8, half * LANES), F32)
    pos = pos.at[0].set(pos2[:half].reshape(-1)).at[1].set(pos2[half:].reshape(-1)).astype(BF16)
    z = jnp.zeros((D, D), F32)
    w2 = jnp.block([[k_w2, z], [z, v_w2]])
    const = lambda shape: pl.BlockSpec(shape, lambda b: (0,) * len(shape))
    return pl.pallas_call(
        _nsa_compress_kernel,
        grid=(B,),
        in_specs=[
            pl.BlockSpec((1, ng, half * LANES), lambda b: (b, 0, 0)),
            const((half * LANES, 2 * LANES)), const((8, half * LANES)),
            const((LANES, LANES)), const((LANES, LANES)),
        ],
        out_specs=[
            pl.BlockSpec((1, ng, LANES), lambda b: (b, 0, 0)),
            pl.BlockSpec((1, VAL_ROWS, ng), lambda b: (b, 0, 0)),
        ],
        out_shape=[
            jax.ShapeDtypeStruct((B, ng, LANES), BF16),
            jax.ShapeDtypeStruct((B, VAL_ROWS, ng), BF16),
        ],
        compiler_params=_params("parallel"),
        name="nsa_compress",
    )(g, wc, pos, w2.astype(BF16), w2.T.astype(BF16))


def _tile_heads(x):
    return jnp.concatenate([x] * NSA_HEADS, axis=1)


def _nsa_attn_kernel(q_ref, gate_ref, glog_ref, rope_ref, rm_ref, ct_ref, kskw_ref, vt_ref,
                     ov_ref, oh_ref, gb_ref, og_ref, y_ref, m_ref, acc_ref):
    D, Q, Hn = HEAD_DIM, Q_BLOCK, NSA_HEADS
    KC = SEL_KEY_CHUNK
    ng = rm_ref.shape[1]
    ns = ov_ref.shape[0]
    n_top = min(SEL_TOPK, ns)
    i = pl.program_id(1)
    t0 = i * Q
    tq = t0 + lax.broadcasted_iota(jnp.int32, (1, Q), 1)

    q = q_ref[0] * (D ** -0.5 * LOG2E)
    cos, sin = rope_ref[0:ROPE_HALF], rope_ref[ROPE_HALF:2 * ROPE_HALF]
    qh = [q[h * D:(h + 1) * D] for h in range(Hn)]
    q4 = jnp.concatenate(qh, axis=1)
    q4r = jnp.concatenate([_rope_rows(x, cos, sin) for x in qh], axis=1)
    zero = jnp.zeros_like(q4)
    q_lo = jnp.concatenate([q4, zero], axis=0).astype(BF16)
    qr_lo = jnp.concatenate([q4r, zero], axis=0).astype(BF16)
    qr_hi = jnp.concatenate([zero, q4r], axis=0).astype(BF16)

    def masked(s, bias):
        return jnp.concatenate([s[:, h * Q:(h + 1) * Q] + bias for h in range(Hn)], axis=1)


    cend = lax.broadcasted_iota(jnp.int32, (ng, 1), 0) * CMP_STRIDE + (CMP_BLOCK - 1)
    s = masked(_dot(rm_ref[0], q_lo), jnp.where(cend <= tq, 0.0, NEG_INF))
    eb = jnp.exp2(s - jnp.max(s, axis=0, keepdims=True)).astype(BF16)
    ol = _dot(ct_ref[0], eb)
    seen = _tile_heads((tq >= CMP_BLOCK - 1).astype(F32))
    inv = seen / jnp.maximum(ol[D:D + 1], 1e-30)
    o_c = ol[0:D] * inv
    psum = sum(eb[:, h * Q:(h + 1) * Q] * inv[:, h * Q:(h + 1) * Q] for h in range(Hn))
    imp = _dot(ov_ref[...], psum.astype(BF16))

    blk = lax.broadcasted_iota(jnp.int32, (ns, Q), 0)
    cur = tq // SEL_BLOCK
    forced = (blk == 0) | (blk == cur) | (blk == cur - 1)
    score = jnp.where(forced, FORCE_SCORE, jnp.where(blk <= cur, imp, -1.0))
    taken = -2.0
    for _ in range(n_top):
        best = jnp.max(score, axis=0, keepdims=True)
        first = jnp.min(jnp.where(score == best, blk, ns), axis=0, keepdims=True)
        score = jnp.where(blk == first, taken, score)

    lk = lax.broadcasted_iota(jnp.int32, (Q, 1), 0)
    lq = lax.broadcasted_iota(jnp.int32, (1, Q), 1)
    s = masked(_dot(kskw_ref[0, pl.ds(pl.multiple_of(t0, Q), Q), :], qr_lo),
               jnp.where(lk <= lq, 0.0, NEG_INF))
    m0 = jnp.max(s, axis=0, keepdims=True)
    m_ref[...] = m0
    acc_ref[...] = _dot(vt_ref[0, i][0:VAL_ROWS], jnp.exp2(s - m0).astype(BF16))

    before = (score == taken) & (blk < t0 // SEL_BLOCK)
    sel_bias = _tile_heads(jnp.where(before, 0.0, NEG_INF))
    pad = oh_ref.shape[1] - ns
    if pad:
        sel_bias = jnp.concatenate([sel_bias, jnp.zeros((pad, Hn * Q), F32)], axis=0)
    q_sel = jnp.concatenate([qr_lo, sel_bias.astype(BF16)], axis=0)

    def sel_scores(j):
        k0 = pl.multiple_of(j * KC, KC)
        keys = jnp.concatenate([kskw_ref[0, pl.ds(k0, KC), :], oh_ref[pl.ds(k0, KC), :]], axis=1)
        return _dot(keys, q_sel)

    def sel_update(j, s):
        m_old = m_ref[...]
        m_new = jnp.maximum(m_old, jnp.max(s, axis=0, keepdims=True))
        eb = jnp.exp2(s - m_new).astype(BF16)
        pv = _dot(vt_ref[0, j * (KC // LANES)][0:VAL_ROWS], eb[0:LANES])
        for u in range(1, KC // LANES):
            pv = pv + _dot(vt_ref[0, j * (KC // LANES) + u][0:VAL_ROWS], eb[u * LANES:(u + 1) * LANES])
        acc_ref[...] = acc_ref[...] * jnp.exp2(m_old - m_new) + pv
        m_ref[...] = m_new

    def chunk_pair(j, carry):
        s_a, s_b = sel_scores(2 * j), sel_scores(2 * j + 1)
        sel_update(2 * j, s_a)
        sel_update(2 * j + 1, s_b)
        return carry

    lax.fori_loop(0, ((t0 + KC - 1) // KC + 1) // 2, chunk_pair, 0)
    acc = acc_ref[...]
    o_s = acc[0:D] / jnp.maximum(acc[D:D + 1], 1e-30)

    WK = WINDOW + Q
    w0 = pl.multiple_of(jnp.maximum(t0 - WINDOW, 0), LANES)
    diff = tq - (w0 + lax.broadcasted_iota(jnp.int32, (WK, 1), 0))
    bias = jnp.where((diff >= 0) & (diff < WINDOW), 0.0, NEG_INF)
    s = masked(_dot(kskw_ref[0, pl.ds(w0, WK), :], qr_hi), bias)
    eb = jnp.exp2(s - jnp.max(s, axis=0, keepdims=True)).astype(BF16)
    ol = _dot(vt_ref[0, w0 // LANES][VAL_ROWS:2 * VAL_ROWS], eb[0:LANES])
    for u in range(1, WK // LANES):
        ol = ol + _dot(vt_ref[0, w0 // LANES + u][VAL_ROWS:2 * VAL_ROWS], eb[u * LANES:(u + 1) * LANES])
    o_w = ol[0:D] / jnp.maximum(ol[D:D + 1], 1e-30)

    gl = _sigmoid(glog_ref[0] + gb_ref[...])
    ys = []
    for h in range(Hn):
        cs = slice(h * Q, (h + 1) * Q)
        o = (gl[3 * h:3 * h + 1] * o_c[:, cs] + gl[3 * h + 1:3 * h + 2] * o_s[:, cs]
             + gl[3 * h + 2:3 * h + 3] * o_w[:, cs])
        o = o * lax.rsqrt(jnp.mean(o * o, axis=0, keepdims=True) + NORM_EPS)
        ys.append(o)
    y = jnp.concatenate(ys, axis=0) * og_ref[...] * _silu(gate_ref[0])
    y_ref[0] = y.T.astype(y_ref.dtype)


def _nsa_attention(pt, rope, cmp_rm, cmp_t, kskw, vt, gate_b, out_g):
    B, _, T = pt.shape
    ng = T // CMP_STRIDE
    ns = T // SEL_BLOCK
    Q, W = Q_BLOCK, NSA_WIDTH
    QH = Q * NSA_HEADS
    c0 = np.arange(ng)[None, :] * CMP_STRIDE
    s0 = np.arange(ns)[:, None] * SEL_BLOCK
    ov = np.clip(np.minimum(c0 + CMP_BLOCK, s0 + SEL_BLOCK) - np.maximum(c0, s0), 0, None) / CMP_BLOCK
    ov[:, ng - 1] = 0.0
    oh_lanes = -(-ns // LANES) * LANES
    onehot = (np.arange(T)[:, None] // SEL_BLOCK) == np.arange(oh_lanes)[None, :]
    gb = jnp.zeros((16, 1), F32).at[:NSA_HEADS * N_BRANCH, 0].set(gate_b)
    glog_blk = (4 * HEAD_DIM + 4 * W) // 16
    return pl.pallas_call(
        _nsa_attn_kernel,
        grid=(B, T // Q),
        in_specs=[
            pl.BlockSpec((1, W, Q), lambda b, i: (b, 1, i)),
            pl.BlockSpec((1, W, Q), lambda b, i: (b, 2, i)),
            pl.BlockSpec((1, 16, Q), lambda b, i: (b, glog_blk, i)),
            pl.BlockSpec((2 * ROPE_HALF, Q), lambda b, i: (0, i)),
            pl.BlockSpec((1, ng, LANES), lambda b, i: (b, 0, 0)),
            pl.BlockSpec((1, VAL_ROWS, ng), lambda b, i: (b, 0, 0)),
            pl.BlockSpec((1, T, LANES), lambda b, i: (b, 0, 0)),
            pl.BlockSpec((1, T // LANES, 2 * VAL_ROWS, LANES), lambda b, i: (b, 0, 0, 0)),
            pl.BlockSpec((ns, ng), lambda b, i: (0, 0)),
            pl.BlockSpec((T, oh_lanes), lambda b, i: (0, 0)),
            pl.BlockSpec((16, 1), lambda b, i: (0, 0)),
            pl.BlockSpec((W, 1), lambda b, i: (0, 0)),
        ],
        out_specs=pl.BlockSpec((1, Q, W), lambda b, i: (b, i, 0)),
        out_shape=jax.ShapeDtypeStruct((B, T, W), BF16),
        scratch_shapes=[pltpu.VMEM((1, QH), F32), pltpu.VMEM((VAL_ROWS, QH), F32)],
        compiler_params=_params("parallel", "arbitrary"),
        name="nsa_attention",
    )(pt, pt, pt, rope, cmp_rm, cmp_t, kskw, vt, jnp.asarray(ov, BF16), jnp.asarray(onehot, BF16), gb,
      out_g.reshape(W, 1))


def _mem_kv_kernel(mem_ref, g_ref, w_ref, wt_ref, k_ref, vt_ref):
    x = mem_ref[0]
    h = x * lax.rsqrt(jnp.mean(x * x, axis=-1, keepdims=True) + NORM_EPS) * g_ref[...]
    hb = h.astype(BF16)
    k_ref[0] = _dot(hb, w_ref[...]).astype(BF16)
    vt_ref[0] = _dot_nt(wt_ref[...], hb).astype(BF16)


def _mem_kv(mem, g, w_kv):
    B, M, D = mem.shape
    W = MEM_WIDTH
    const = lambda shape: pl.BlockSpec(shape, lambda b: (0,) * len(shape))
    return pl.pallas_call(
        _mem_kv_kernel,
        grid=(B,),
        in_specs=[pl.BlockSpec((1, M, D), lambda b: (b, 0, 0)), const((1, D)), const((D, W)),
                  const((W, D))],
        out_specs=[pl.BlockSpec((1, M, W), lambda b: (b, 0, 0)),
                   pl.BlockSpec((1, W, M), lambda b: (b, 0, 0))],
        out_shape=[jax.ShapeDtypeStruct((B, M, W), BF16), jax.ShapeDtypeStruct((B, W, M), BF16)],
        compiler_params=_params("parallel"),
        name="mem_kv",
    )(mem, g.reshape(1, D), w_kv[:, :W].astype(BF16), w_kv[:, W:].T.astype(BF16))


def _mem_attn_kernel(q_ref, gate_ref, k_ref, vt_ref, og_ref, y_ref):
    D, Hm = HEAD_DIM, MEM_HEADS
    q = q_ref[0] * (D ** -0.5)
    k = k_ref[0]
    vt = vt_ref[0]
    row = lax.broadcasted_iota(jnp.int32, (Hm * D, 1), 0)
    ys = []
    for h in range(Hm):
        qh = jnp.where(row // D == h, q, 0.0).astype(BF16)
        s = _dot(k, qh)
        e = jnp.exp(s - jnp.max(s, axis=0, keepdims=True))
        p = e / jnp.sum(e, axis=0, keepdims=True)
        o = _dot(vt[h * D:(h + 1) * D], p.astype(BF16))
        ys.append(o * lax.rsqrt(jnp.mean(o * o, axis=0, keepdims=True) + NORM_EPS))
    y = jnp.concatenate(ys, axis=0) * og_ref[...] * _silu(gate_ref[0])
    y_ref[0] = y.T.astype(y_ref.dtype)


def _mem_attention(pt, mem_k, mem_vt, out_g, tm):
    B, _, T = pt.shape
    M = mem_k.shape[1]
    W = MEM_WIDTH
    return pl.pallas_call(
        _mem_attn_kernel,
        grid=(B, T // tm),
        in_specs=[
            pl.BlockSpec((1, W, tm), lambda b, i: (b, 3, i)),
            pl.BlockSpec((1, W, tm), lambda b, i: (b, 4, i)),
            pl.BlockSpec((1, M, W), lambda b, i: (b, 0, 0)),
            pl.BlockSpec((1, W, M), lambda b, i: (b, 0, 0)),
            pl.BlockSpec((W, 1), lambda b, i: (0, 0)),
        ],
        out_specs=pl.BlockSpec((1, tm, W), lambda b, i: (b, i, 0)),
        out_shape=jax.ShapeDtypeStruct((B, T, W), BF16),
        compiler_params=_params("parallel", "parallel"),
        name="mem_attention",
    )(pt, pt, mem_k, mem_vt, out_g.reshape(W, 1))


def _outproj_kernel(x_ref, yr_ref, yn_ref, ym_ref, wr_ref, wn_ref, wm_ref, g_ref, o_ref):
    z = (x_ref[0] + _dot(yr_ref[0], wr_ref[...]) + _dot(yn_ref[0], wn_ref[...])
         + _dot(ym_ref[0], wm_ref[...]))
    o_ref[0] = z * lax.rsqrt(jnp.mean(z * z, axis=-1, keepdims=True) + NORM_EPS) * g_ref[...]


def _output_projection(x, y_rwkv, y_nsa, y_mem, w_out, g, tm):
    B, T, D = x.shape
    wb = w_out.astype(BF16)
    w_r, w_n, w_m = wb[:RWKV_WIDTH], wb[RWKV_WIDTH:RWKV_WIDTH + NSA_WIDTH], wb[RWKV_WIDTH + NSA_WIDTH:]
    tile = lambda w: pl.BlockSpec((1, tm, w), lambda b, i: (b, i, 0))
    const = lambda shape: pl.BlockSpec(shape, lambda b, i: (0,) * len(shape))
    return pl.pallas_call(
        _outproj_kernel,
        grid=(B, T // tm),
        in_specs=[tile(D), tile(RWKV_WIDTH), tile(NSA_WIDTH), tile(MEM_WIDTH),
                  const((RWKV_WIDTH, D)), const((NSA_WIDTH, D)), const((MEM_WIDTH, D)), const((1, D))],
        out_specs=tile(D),
        out_shape=jax.ShapeDtypeStruct((B, T, D), F32),
        compiler_params=_params("parallel", "parallel"),
        name="output_projection",
    )(x, y_rwkv, y_nsa, y_mem, w_r, w_n, w_m, g.reshape(1, D))


def _rope_table(T):
    inv_freq = ROPE_THETA ** (-jnp.arange(ROPE_HALF, dtype=F32) / ROPE_HALF)
    ang = inv_freq[:, None] * jnp.arange(T).astype(F32)[None, :]
    return jnp.concatenate([jnp.cos(ang), jnp.sin(ang)], axis=0)


def _split_w_in(w):
    D = HEAD_DIM
    n0 = RWKV_COLS
    q, gate, glog = n0, n0 + NSA_WIDTH, n0 + 2 * NSA_WIDTH
    kc = glog + NSA_HEADS * N_BRANCH
    vc, ks, vs, kw, vw = kc + D, kc + 2 * D, kc + 3 * D, kc + 4 * D, kc + 5 * D
    m0 = vw + D
    cols = lambda a, n: w[:, a:a + n]
    w_row = jnp.concatenate([cols(0, n0), cols(kc, D), cols(vc, D)], axis=1)
    w_t = jnp.concatenate([cols(ks, D), cols(kw, D), cols(vs, D), cols(vw, D), cols(q, NSA_WIDTH),
                           cols(gate, NSA_WIDTH), cols(m0, MEM_WIDTH), cols(m0 + MEM_WIDTH, MEM_WIDTH),
                           cols(glog, NSA_HEADS * N_BRANCH),
                           jnp.zeros((w.shape[0], 16 - NSA_HEADS * N_BRANCH), w.dtype)], axis=1)
    return w_row.astype(BF16), w_t.T.astype(BF16)


def kernel(x, mem, norm_in_g, w_in, rwkv_mu, rwkv_w0, rwkv_w_up, rwkv_a0, rwkv_a_up, rwkv_k_k,
           rwkv_k_a, rwkv_r_k, rwkv_ln_w, rwkv_ln_b, nsa_cmp_pos, nsa_cmp_k_w1, nsa_cmp_k_w2,
           nsa_cmp_v_w1, nsa_cmp_v_w2, nsa_gate_b, nsa_out_g, mem_norm_g, w_mem_kv, mem_out_g, w_out,
           norm_final_g):
    B, T, D = x.shape
    assert w_in.shape[0] == 1, "single-layer stack: the final norm is fused into the output projection"
    rope = _rope_table(T)
    w_row, w_t = _split_w_in(w_in[0])
    p_row, kcvc, pt = _input_projection(x, norm_in_g[0].reshape(1, D), w_row, w_t, tm=256)
    y_rwkv = _rwkv_group(p_row, rwkv_mu[0], rwkv_w0[0], rwkv_w_up[0], rwkv_a0[0], rwkv_a_up[0],
                         rwkv_k_k[0], rwkv_k_a[0], rwkv_r_k[0].reshape(-1), rwkv_ln_w[0],
                         rwkv_ln_b[0])
    kskw, vt = _nsa_keys(pt, rope, tk=512)
    cmp_rm, cmp_t = _nsa_compress(kcvc, nsa_cmp_pos[0], nsa_cmp_k_w1[0], nsa_cmp_k_w2[0],
                                  nsa_cmp_v_w1[0], nsa_cmp_v_w2[0])
    y_nsa = _nsa_attention(pt, rope, cmp_rm, cmp_t, kskw, vt, nsa_gate_b[0], nsa_out_g[0])
    mem_k, mem_vt = _mem_kv(mem, mem_norm_g[0], w_mem_kv[0])
    y_mem = _mem_attention(pt, mem_k, mem_vt, mem_out_g[0], tm=512)
    return _output_projection(x, y_rwkv, y_nsa, y_mem, w_out[0], norm_final_g, tm=512)
```

```python
import functools

import numpy as np
import jax
import jax.numpy as jnp
from jax import lax
from jax.experimental import pallas as pl
from jax.experimental.pallas import tpu as pltpu

F32 = jnp.float32
BF16 = jnp.bfloat16

HEAD_DIM = 64
RWKV_HEADS = 8
RWKV_WIDTH = RWKV_HEADS * HEAD_DIM
LORA = 64
RWKV_COLS = 4 * RWKV_WIDTH + 2 * LORA
RWKV_GN_EPS = 64e-5
NSA_HEADS = 4
NSA_WIDTH = NSA_HEADS * HEAD_DIM
N_BRANCH = 3
CMP_BLOCK = 32
CMP_STRIDE = 16
SEL_BLOCK = 64
SEL_TOPK = 16
WINDOW = 512
MEM_HEADS = 4
MEM_WIDTH = MEM_HEADS * HEAD_DIM
ROPE_THETA = 500000.0
ROPE_HALF = 8
Q_BLOCK = 128
NORM_EPS = 1e-6
NEG_INF = -1e30
FORCE_SCORE = 1e4
LOG2E = 1.4426950408889634

RWKV_CHUNK = 64
RWKV_HEAD_GROUP = 4
RWKV_BATCH_TILE = 4
RWKV_CHAIN_BATCH = 2
SEL_KEY_CHUNK = 512
LANES = 128
MXU_DEPTH = 256
ONES_ROWS = 16
VAL_ROWS = HEAD_DIM + ONES_ROWS
VMEM_LIMIT = 48 * 1024 * 1024

def _dot(a, b):
    return jnp.dot(a, b, preferred_element_type=F32)


def _dot_nt(a, b):
    return lax.dot_general(a, b, (((1,), (1,)), ((), ())), preferred_element_type=F32)


def _dot_tn(a, b):
    return lax.dot_general(a, b, (((0,), (0,)), ((), ())), preferred_element_type=F32)


def _bf(a):
    return a.astype(BF16)


def _split_bf16(a, pieces):
    out = []
    for _ in range(pieces):
        part = a.astype(BF16)
        out.append(part)
        a = a - part.astype(F32)
    return out


def _sigmoid(x):
    return 1.0 / (1.0 + jnp.exp(-x))


def _silu(x):
    return x * _sigmoid(x)


def _params(*sem):
    return pltpu.CompilerParams(dimension_semantics=sem, vmem_limit_bytes=VMEM_LIMIT)


def _inproj_kernel(x_ref, g_ref, wrow_ref, wt_ref, prow_ref, kc_ref, pt_ref):
    x = x_ref[0]
    h = x * lax.rsqrt(jnp.mean(x * x, axis=-1, keepdims=True) + NORM_EPS) * g_ref[...]
    hb = h.astype(BF16)
    row = _dot(hb, wrow_ref[...])
    prow_ref[0] = row[:, :RWKV_COLS]
    kc_ref[0] = row[:, RWKV_COLS:]
    pt_ref[0] = _dot_nt(wt_ref[...], hb)


def _input_projection(x, g, w_row, w_t, tm):
    B, T, D = x.shape
    n_row = w_row.shape[1]
    n_t = w_t.shape[0]
    return pl.pallas_call(
        _inproj_kernel,
        grid=(B, T // tm),
        in_specs=[
            pl.BlockSpec((1, tm, D), lambda b, i: (b, i, 0)),
            pl.BlockSpec((1, D), lambda b, i: (0, 0)),
            pl.BlockSpec((D, n_row), lambda b, i: (0, 0)),
            pl.BlockSpec((n_t, D), lambda b, i: (0, 0)),
        ],
        out_specs=[
            pl.BlockSpec((1, tm, RWKV_COLS), lambda b, i: (b, i, 0)),
            pl.BlockSpec((1, tm, n_row - RWKV_COLS), lambda b, i: (b, i, 0)),
            pl.BlockSpec((1, n_t, tm), lambda b, i: (b, 0, i)),
        ],
        out_shape=[
            jax.ShapeDtypeStruct((B, T, RWKV_COLS), F32),
            jax.ShapeDtypeStruct((B, T, n_row - RWKV_COLS), F32),
            jax.ShapeDtypeStruct((B, n_t, T), F32),
        ],
        compiler_params=_params("parallel", "parallel"),
        name="input_projection",
    )(x, g, w_row, w_t)


def _rwkv_kernel(p_ref, prev_ref, mu_ref, w0_ref, wup_ref, a0_ref, aup_ref, kk_ref, ka_ref,
                 rk_ref, lnw_ref, lnb_ref, bd_ref, tri_ref, y_ref, s_ref):
    C, W, N, H = RWKV_CHUNK, RWKV_WIDTH, HEAD_DIM, RWKV_HEADS
    HG = RWKV_HEAD_GROUP
    GW = HG * N
    n = pl.program_id(1)

    @pl.when(n == 0)
    def _():
        s_ref[...] = jnp.zeros_like(s_ref)

    bd = bd_ref[...]
    head_sum = lambda a: jnp.concatenate(
        [_dot(_bf(a[:, g * GW:(g + 1) * GW]), bd) for g in range(H // HG)], axis=1)
    row = lax.broadcasted_iota(jnp.int32, (C, 1), 0)

    def prepare(b):
        p = p_ref[b]
        prev_last = jnp.where(n == 0, 0.0, prev_ref[b][7:8, :])
        prev = jnp.where(row == 0, prev_last, pltpu.roll(p, 1, axis=0))
        pf = p + mu_ref[...] * (prev - p)
        r = pf[:, 0:W]
        k = pf[:, W:2 * W]
        v = pf[:, 2 * W:3 * W]
        gate = pf[:, 3 * W:4 * W]
        wd = pf[:, 4 * W:4 * W + LORA]
        ad = pf[:, 4 * W + LORA:4 * W + 2 * LORA]
        z = w0_ref[...] + _dot(_bf(jnp.tanh(wd)), wup_ref[...])
        w_log = -(jnp.maximum(-z, 0.0) + jnp.log(1.0 + jnp.exp(-jnp.abs(z)))) - 0.5
        logw = -jnp.exp(w_log)
        eta = _sigmoid(a0_ref[...] + _dot(_bf(ad), aup_ref[...]))
        kk = k * kk_ref[...]
        k2 = k * (1.0 + (eta - 1.0) * ka_ref[...])
        kk = kk / jnp.maximum(jnp.sqrt(head_sum(kk * kk)), 1e-12)
        tri = tri_ref[...]
        cs = sum(_dot(tri, part) for part in _split_bf16(logw, 3))
        mid = cs[C // 2 - 1:C // 2, :]
        csm = cs - mid
        e_out = jnp.exp(-csm)
        return dict(
            v=v, gate=gate, bonus=head_sum(r * k2 * rk_ref[...]),
            g_mid=jnp.exp(mid), g_end=jnp.exp(cs[C - 1:C, :]), g_end_mid=jnp.exp(cs[C - 1:C, :] - mid),
            r=_bf(r * jnp.exp(csm)), a=_bf(-kk * jnp.exp(csm - logw)),
            k=_bf(k2 * e_out), b=_bf(kk * eta * e_out), vb=_bf(v))

    li = lax.broadcasted_iota(jnp.int32, (C, GW), 1) % N
    ti = lax.broadcasted_iota(jnp.int32, (C, GW), 0)
    strict, incl = ti > li, ti >= li
    eye = (ti == li).astype(F32)
    same_head = (lax.broadcasted_iota(jnp.int32, (GW, GW), 0) // N
                 == lax.broadcasted_iota(jnp.int32, (GW, GW), 1) // N)

    def blockdiag(a):
        a = _bf(a)
        return jnp.where(same_head, jnp.concatenate([a] * HG, axis=0), jnp.zeros((), BF16))

    def advance(batches):
        pre = {b: prepare(b) for b in batches}
        chains = [(b, g) for b in batches for g in range(H // HG)]
        st = {}
        for c in chains:
            b, g = c
            ls = slice(g * GW, (g + 1) * GW)
            d = pre[b]
            kb = jnp.concatenate([blockdiag(d["k"][:, ls]), blockdiag(d["b"][:, ls])], axis=0)
            ar = jnp.concatenate([d["a"][:, ls], d["r"][:, ls]], axis=0)
            gc = _dot_nt(ar, kb)
            s0 = s_ref[b, g]
            st[c] = dict(
                ls=ls, kb=kb, s0=s0, v_bd=blockdiag(d["vb"][:, ls]),
                a_ak=jnp.where(strict, gc[0:C, 0:GW], 0.0), a_ab=jnp.where(strict, gc[0:C, GW:2 * GW], 0.0),
                a_rk=jnp.where(incl, gc[C:2 * C, 0:GW], 0.0), a_rb=jnp.where(incl, gc[C:2 * C, GW:2 * GW], 0.0),
                uy0=_dot_nt(ar, _bf(s0 * d["g_mid"][:, ls])))
        for c in chains:
            s = st[c]
            s["rhs"] = s["uy0"][0:C] + _dot(_bf(s["a_ak"]), s["v_bd"])
            s["tm"] = eye + s["a_ab"]
            s["pw"] = _dot(_bf(s["a_ab"]), blockdiag(s["a_ab"]))
        span = 2
        while span < C:
            for c in chains:
                s = st[c]
                if 2 * span < C:
                    tp = _dot(_bf(s["pw"]), jnp.concatenate([blockdiag(s["tm"]), blockdiag(s["pw"])], axis=1))
                    s["tm"] = s["tm"] + tp[:, 0:GW]
                    s["pw"] = tp[:, GW:2 * GW]
                else:
                    s["tm"] = s["tm"] + _dot(_bf(s["pw"]), blockdiag(s["tm"]))
            span *= 2
        for c in chains:
            s = st[c]
            s["u"] = _dot(_bf(s["tm"]), blockdiag(s["rhs"]))
        for c in chains:
            s = st[c]
            s["vu"] = jnp.concatenate([s["v_bd"], blockdiag(s["u"])], axis=0)
            s["y"] = s["uy0"][C:2 * C] + _dot(_bf(jnp.concatenate([s["a_rk"], s["a_rb"]], axis=1)), s["vu"])
        for c in chains:
            b, g = c
            s, d = st[c], pre[b]
            s_ref[b, g] = s["s0"] * d["g_end"][:, s["ls"]] + _dot_tn(s["vu"], s["kb"]) * d["g_end_mid"][:, s["ls"]]
        for b in batches:
            d = pre[b]
            y = jnp.concatenate([st[(b, g)]["y"] for g in range(H // HG)], axis=1)
            mean = head_sum(y) * (1.0 / N)
            var = head_sum(jnp.square(y - mean)) * (1.0 / N)
            y = (y - mean) * lax.rsqrt(var + RWKV_GN_EPS) * lnw_ref[...] + lnb_ref[...] + d["bonus"] * d["v"]
            y_ref[b] = (y * _silu(d["gate"])).astype(y_ref.dtype)

    bt = p_ref.shape[0]
    for b0 in range(0, bt, RWKV_CHAIN_BATCH):
        advance(range(b0, min(b0 + RWKV_CHAIN_BATCH, bt)))


def _rwkv_group(p_row, mu, w0, w_up, a0, a_up, k_k, k_a, r_k, ln_w, ln_b):
    B, T, _ = p_row.shape
    C, W = RWKV_CHUNK, RWKV_WIDTH
    heads = np.arange(RWKV_HEAD_GROUP * HEAD_DIM) // HEAD_DIM
    bd = jnp.asarray(heads[:, None] == heads[None, :], BF16)
    tri = jnp.asarray(np.tril(np.ones((C, C), np.float32)), BF16)
    vec = lambda a: a.reshape(1, -1)
    const = lambda shape: pl.BlockSpec(shape, lambda b, n: (0,) * len(shape))
    bt = RWKV_BATCH_TILE if B % RWKV_BATCH_TILE == 0 else 1
    return pl.pallas_call(
        _rwkv_kernel,
        grid=(B // bt, T // C),
        in_specs=[
            pl.BlockSpec((bt, C, RWKV_COLS), lambda b, n: (b, n, 0)),
            pl.BlockSpec((bt, 8, RWKV_COLS), lambda b, n: (b, jnp.maximum(n * (C // 8) - 1, 0), 0)),
            const((1, RWKV_COLS)), const((1, W)), const((LORA, W)), const((1, W)), const((LORA, W)),
            const((1, W)), const((1, W)), const((1, W)), const((1, W)), const((1, W)),
            const(bd.shape), const((C, C)),
        ],
        out_specs=pl.BlockSpec((bt, C, W), lambda b, n: (b, n, 0)),
        out_shape=jax.ShapeDtypeStruct((B, T, W), BF16),
        scratch_shapes=[pltpu.VMEM((bt, RWKV_HEADS // RWKV_HEAD_GROUP) + (RWKV_HEAD_GROUP * HEAD_DIM,) * 2,
                                   F32)],
        compiler_params=_params("parallel", "arbitrary"),
        name="rwkv7_group",
    )(p_row, p_row, vec(mu), vec(w0), _bf(w_up), vec(a0), _bf(a_up), vec(k_k), vec(k_a), vec(r_k),
      vec(ln_w), vec(ln_b), bd, tri)


def _rope_rows(x, cos, sin):
    x1, x2 = x[0:ROPE_HALF], x[ROPE_HALF:2 * ROPE_HALF]
    return jnp.concatenate([x1 * cos - x2 * sin, x2 * cos + x1 * sin, x[2 * ROPE_HALF:]], axis=0)


def _nsa_keys_kernel(kv_ref, rope_ref, kskw_ref, vt_ref, vs2_ref):
    kv = kv_ref[0]
    cos, sin = rope_ref[0:ROPE_HALF], rope_ref[ROPE_HALF:2 * ROPE_HALF]
    keys = jnp.concatenate([_rope_rows(kv[0:HEAD_DIM], cos, sin),
                            _rope_rows(kv[HEAD_DIM:2 * HEAD_DIM], cos, sin)], axis=0)
    kskw_ref[0] = keys.T.astype(BF16)
    vals = kv[2 * HEAD_DIM:4 * HEAD_DIM].astype(BF16)
    ones = jnp.ones((ONES_ROWS, LANES), BF16)
    for u in range(vals.shape[1] // LANES):
        blk = vals[:, u * LANES:(u + 1) * LANES]
        vt_ref[0, u] = jnp.concatenate([blk[0:HEAD_DIM], ones, blk[HEAD_DIM:2 * HEAD_DIM], ones], axis=0)
    ones2 = jnp.ones((ONES_ROWS, MXU_DEPTH), BF16)
    for u in range(vals.shape[1] // MXU_DEPTH):
        vs2_ref[0, u] = jnp.concatenate([vals[0:HEAD_DIM, u * MXU_DEPTH:(u + 1) * MXU_DEPTH], ones2], axis=0)


def _nsa_keys(pt, rope, tk):
    B, _, T = pt.shape
    return pl.pallas_call(
        _nsa_keys_kernel,
        grid=(B, T // tk),
        in_specs=[
            pl.BlockSpec((1, 4 * HEAD_DIM, tk), lambda b, i: (b, 0, i)),
            pl.BlockSpec((2 * ROPE_HALF, tk), lambda b, i: (0, i)),
        ],
        out_specs=[
            pl.BlockSpec((1, tk, LANES), lambda b, i: (b, i, 0)),
            pl.BlockSpec((1, tk // LANES, 2 * VAL_ROWS, LANES), lambda b, i: (b, i, 0, 0)),
            pl.BlockSpec((1, tk // MXU_DEPTH, VAL_ROWS, MXU_DEPTH), lambda b, i: (b, i, 0, 0)),
        ],
        out_shape=[
            jax.ShapeDtypeStruct((B, T, LANES), BF16),
            jax.ShapeDtypeStruct((B, T // LANES, 2 * VAL_ROWS, LANES), BF16),
            jax.ShapeDtypeStruct((B, T // MXU_DEPTH, VAL_ROWS, MXU_DEPTH), BF16),
        ],
        compiler_params=_params("parallel", "parallel"),
        name="nsa_keys",
    )(pt, rope)


def _nsa_compress_kernel(g_ref, wc_ref, pos_ref, w2_ref, w2t_ref, rm_ref, t_ref):
    ng = g_ref.shape[1]
    wc = wc_ref[...]
    m = _dot(g_ref[0].astype(BF16), wc)
    pm = _dot(pos_ref[...], wc)
    pos_term = pm[0:1, 0:LANES] + pm[1:2, LANES:2 * LANES]
    pre = m[:, 0:LANES] + pltpu.roll(m[:, LANES:2 * LANES], ng - 1, axis=0) + pos_term
    act = _silu(pre).astype(BF16)
    row = lax.broadcasted_iota(jnp.int32, (ng, 1), 0)
    col = lax.broadcasted_iota(jnp.int32, (1, ng), 1)
    rm_ref[0] = jnp.where(row < ng - 1, _dot(act, w2_ref[...]), 0.0).astype(BF16)
    vt = jnp.where(col < ng - 1, _dot_nt(w2t_ref[...], act)[HEAD_DIM:2 * HEAD_DIM], 0.0)
    t_ref[0] = jnp.concatenate([vt, jnp.ones((ONES_ROWS, ng), F32)], axis=0).astype(BF16)


def _nsa_compress(kcvc, cmp_pos, k_w1, k_w2, v_w1, v_w2):
    B, T, _ = kcvc.shape
    ng = T // CMP_STRIDE
    half = CMP_BLOCK // 2
    g = kcvc.reshape(B, ng, half * LANES)
    D = HEAD_DIM

    def spread(w, second, is_v):
        blk = w[second * half * D:(second + 1) * half * D].reshape(half, D, D)
        z = jnp.zeros_like(blk)
        return jnp.concatenate([z, blk] if is_v else [blk, z], axis=1).reshape(half * LANES, D)

    wc = jnp.concatenate([spread(k_w1, 0, False), spread(v_w1, 0, True),
                          spread(k_w1, 1, False), spread(v_w1, 1, True)], axis=1).astype(BF16)
    pos2 = jnp.concatenate([cmp_pos, cmp_pos], axis=1)
    pos = jnp.zeros((8, half * LANES), F32)
    pos = pos.at[0].set(pos2[:half].reshape(-1)).at[1].set(pos2[half:].reshape(-1)).astype(BF16)
    z = jnp.zeros((D, D), F32)
    w2 = jnp.block([[k_w2, z], [z, v_w2]])
    const = lambda shape: pl.BlockSpec(shape, lambda b: (0,) * len(shape))
    return pl.pallas_call(
        _nsa_compress_kernel,
        grid=(B,),
        in_specs=[
            pl.BlockSpec((1, ng, half * LANES), lambda b: (b, 0, 0)),
            const((half * LANES, 2 * LANES)), const((8, half * LANES)),
            const((LANES, LANES)), const((LANES, LANES)),
        ],
        out_specs=[
            pl.BlockSpec((1, ng, LANES), lambda b: (b, 0, 0)),
            pl.BlockSpec((1, VAL_ROWS, ng), lambda b: (b, 0, 0)),
        ],
        out_shape=[
            jax.ShapeDtypeStruct((B, ng, LANES), BF16),
            jax.ShapeDtypeStruct((B, VAL_ROWS, ng), BF16),
        ],
        compiler_params=_params("parallel"),
        name="nsa_compress",
    )(g, wc, pos, w2.astype(BF16), w2.T.astype(BF16))


def _tile_heads(x):
    return jnp.concatenate([x] * NSA_HEADS, axis=1)


def _nsa_attn_kernel(q_ref, gate_ref, glog_ref, rope_ref, rm_ref, ct_ref, kskw_ref, vt_ref, vs2_ref,
                     ov_ref, oh_ref, gb_ref, og_ref, y_ref, m_ref, acc_ref, sa_ref, sb_ref):
    D, Q, Hn = HEAD_DIM, Q_BLOCK, NSA_HEADS
    KC = SEL_KEY_CHUNK
    ng = rm_ref.shape[1]
    ns = ov_ref.shape[0]
    n_top = min(SEL_TOPK, ns)
    i = pl.program_id(1)
    t0 = i * Q
    tq = t0 + lax.broadcasted_iota(jnp.int32, (1, Q), 1)

    q = q_ref[0] * (D ** -0.5 * LOG2E)
    cos, sin = rope_ref[0:ROPE_HALF], rope_ref[ROPE_HALF:2 * ROPE_HALF]
    qh = [q[h * D:(h + 1) * D] for h in range(Hn)]
    q4 = jnp.concatenate(qh, axis=1)
    q4r = jnp.concatenate([_rope_rows(x, cos, sin) for x in qh], axis=1)
    zero = jnp.zeros_like(q4)
    q_lo = jnp.concatenate([q4, zero], axis=0).astype(BF16)
    qr_lo = jnp.concatenate([q4r, zero], axis=0).astype(BF16)
    qr_hi = jnp.concatenate([zero, q4r], axis=0).astype(BF16)

    def masked(s, bias):
        return jnp.concatenate([s[:, h * Q:(h + 1) * Q] + bias for h in range(Hn)], axis=1)


    cend = lax.broadcasted_iota(jnp.int32, (ng, 1), 0) * CMP_STRIDE + (CMP_BLOCK - 1)
    s = masked(_dot(rm_ref[0], q_lo), jnp.where(cend <= tq, 0.0, NEG_INF))
    eb = jnp.exp2(s - jnp.max(s, axis=0, keepdims=True)).astype(BF16)
    ol = _dot(ct_ref[0], eb)
    seen = _tile_heads((tq >= CMP_BLOCK - 1).astype(F32))
    inv = seen / jnp.maximum(ol[D:D + 1], 1e-30)
    o_c = ol[0:D] * inv
    psum = sum(eb[:, h * Q:(h + 1) * Q] * inv[:, h * Q:(h + 1) * Q] for h in range(Hn))
    imp = _dot(ov_ref[...], psum.astype(BF16))

    WK = WINDOW + Q
    w0 = pl.multiple_of(jnp.maximum(t0 - WINDOW, 0), LANES)
    diff = tq - (w0 + lax.broadcasted_iota(jnp.int32, (WK, 1), 0))
    w_bias = jnp.where((diff >= 0) & (diff < WINDOW), 0.0, NEG_INF)

    def window_head(h):
        s = _dot(kskw_ref[0, pl.ds(w0, WK), :], qr_hi[:, h * Q:(h + 1) * Q]) + w_bias
        eb = jnp.exp2(s - jnp.max(s, axis=0, keepdims=True)).astype(BF16)
        ol = _dot(vt_ref[0, w0 // LANES][VAL_ROWS:2 * VAL_ROWS], eb[0:LANES])
        for u in range(1, WK // LANES):
            ol = ol + _dot(vt_ref[0, w0 // LANES + u][VAL_ROWS:2 * VAL_ROWS], eb[u * LANES:(u + 1) * LANES])
        return ol[0:D] / jnp.maximum(ol[D:D + 1], 1e-30)

    blk = lax.broadcasted_iota(jnp.int32, (ns, Q), 0)
    cur = tq // SEL_BLOCK
    forced = (blk == 0) | (blk == cur) | (blk == cur - 1)
    taken = -2.0
    score = jnp.where(forced, taken, jnp.where(blk <= cur, imp, -1.0))
    picks = max(n_top - 3, 0)
    o_w = []
    for it in range(picks):
        best = jnp.max(score, axis=0, keepdims=True)
        first = jnp.min(jnp.where(score == best, blk, ns), axis=0, keepdims=True)
        score = jnp.where(blk == first, taken, score)
        if it % 3 == 0 and len(o_w) < Hn:
            o_w.append(window_head(len(o_w)))
    while len(o_w) < Hn:
        o_w.append(window_head(len(o_w)))
    o_w = jnp.concatenate(o_w, axis=1)

    lk = lax.broadcasted_iota(jnp.int32, (Q, 1), 0)
    lq = lax.broadcasted_iota(jnp.int32, (1, Q), 1)
    s = masked(_dot(kskw_ref[0, pl.ds(pl.multiple_of(t0, Q), Q), :], qr_lo),
               jnp.where(lk <= lq, 0.0, NEG_INF))
    m0 = jnp.max(s, axis=0, keepdims=True)
    m_ref[...] = m0
    acc_ref[...] = _dot(vt_ref[0, i][0:VAL_ROWS], jnp.exp2(s - m0).astype(BF16))

    before = (score == taken) & (blk < t0 // SEL_BLOCK)
    sel_bias = _tile_heads(jnp.where(before, 0.0, NEG_INF))
    pad = oh_ref.shape[1] - ns
    if pad:
        sel_bias = jnp.concatenate([sel_bias, jnp.zeros((pad, Hn * Q), F32)], axis=0)
    q_sel = jnp.concatenate([qr_lo, sel_bias.astype(BF16)], axis=0)

    def sel_scores(j):
        k0 = pl.multiple_of(j * KC, KC)
        keys = jnp.concatenate([kskw_ref[0, pl.ds(k0, KC), :], oh_ref[pl.ds(k0, KC), :]], axis=1)
        return _dot(keys, q_sel)

    UB = KC // MXU_DEPTH

    def sel_update(j, s):
        m_old = m_ref[...]
        m_new = jnp.maximum(m_old, jnp.max(s, axis=0, keepdims=True))
        eb = jnp.exp2(s - m_new).astype(BF16)
        pv = acc_ref[...] * jnp.exp2(m_old - m_new)
        for u in range(UB):
            pv = pv + _dot(vs2_ref[0, j * UB + u], eb[u * MXU_DEPTH:(u + 1) * MXU_DEPTH])
        acc_ref[...] = pv
        m_ref[...] = m_new

    n_chunks = (t0 + KC - 1) // KC
    last = kskw_ref.shape[1] // KC - 1
    sa_ref[...] = sel_scores(0)

    def chunk_pair(j, carry):
        sb_ref[...] = sel_scores(jnp.minimum(2 * j + 1, last))
        sel_update(2 * j, sa_ref[...])
        sa_ref[...] = sel_scores(jnp.minimum(2 * j + 2, last))
        sel_update(jnp.minimum(2 * j + 1, last), sb_ref[...])
        return carry

    lax.fori_loop(0, (n_chunks + 1) // 2, chunk_pair, 0)
    acc = acc_ref[...]
    o_s = acc[0:D] / jnp.maximum(acc[D:D + 1], 1e-30)

    gl = _sigmoid(glog_ref[0] + gb_ref[...])
    ys = []
    for h in range(Hn):
        cs = slice(h * Q, (h + 1) * Q)
        o = (gl[3 * h:3 * h + 1] * o_c[:, cs] + gl[3 * h + 1:3 * h + 2] * o_s[:, cs]
             + gl[3 * h + 2:3 * h + 3] * o_w[:, cs])
        o = o * lax.rsqrt(jnp.mean(o * o, axis=0, keepdims=True) + NORM_EPS)
        ys.append(o)
    y = jnp.concatenate(ys, axis=0) * og_ref[...] * _silu(gate_ref[0])
    y_ref[0] = y.T.astype(y_ref.dtype)


def _nsa_attention(pt, rope, cmp_rm, cmp_t, kskw, vt, vs2, gate_b, out_g):
    B, _, T = pt.shape
    ng = T // CMP_STRIDE
    ns = T // SEL_BLOCK
    Q, W = Q_BLOCK, NSA_WIDTH
    QH = Q * NSA_HEADS
    c0 = np.arange(ng)[None, :] * CMP_STRIDE
    s0 = np.arange(ns)[:, None] * SEL_BLOCK
    ov = np.clip(np.minimum(c0 + CMP_BLOCK, s0 + SEL_BLOCK) - np.maximum(c0, s0), 0, None) / CMP_BLOCK
    ov[:, ng - 1] = 0.0
    oh_lanes = -(-ns // LANES) * LANES
    onehot = (np.arange(T)[:, None] // SEL_BLOCK) == np.arange(oh_lanes)[None, :]
    gb = jnp.zeros((16, 1), F32).at[:NSA_HEADS * N_BRANCH, 0].set(gate_b)
    glog_blk = (4 * HEAD_DIM + 4 * W) // 16
    return pl.pallas_call(
        _nsa_attn_kernel,
        grid=(B, T // Q),
        in_specs=[
            pl.BlockSpec((1, W, Q), lambda b, i: (b, 1, i)),
            pl.BlockSpec((1, W, Q), lambda b, i: (b, 2, i)),
            pl.BlockSpec((1, 16, Q), lambda b, i: (b, glog_blk, i)),
            pl.BlockSpec((2 * ROPE_HALF, Q), lambda b, i: (0, i)),
            pl.BlockSpec((1, ng, LANES), lambda b, i: (b, 0, 0)),
            pl.BlockSpec((1, VAL_ROWS, ng), lambda b, i: (b, 0, 0)),
            pl.BlockSpec((1, T, LANES), lambda b, i: (b, 0, 0)),
            pl.BlockSpec((1, T // LANES, 2 * VAL_ROWS, LANES), lambda b, i: (b, 0, 0, 0)),
            pl.BlockSpec((1, T // MXU_DEPTH, VAL_ROWS, MXU_DEPTH), lambda b, i: (b, 0, 0, 0)),
            pl.BlockSpec((ns, ng), lambda b, i: (0, 0)),
            pl.BlockSpec((T, oh_lanes), lambda b, i: (0, 0)),
            pl.BlockSpec((16, 1), lambda b, i: (0, 0)),
            pl.BlockSpec((W, 1), lambda b, i: (0, 0)),
        ],
        out_specs=pl.BlockSpec((1, Q, W), lambda b, i: (b, i, 0)),
        out_shape=jax.ShapeDtypeStruct((B, T, W), BF16),
        scratch_shapes=[pltpu.VMEM((1, QH), F32), pltpu.VMEM((VAL_ROWS, QH), F32),
                        pltpu.VMEM((SEL_KEY_CHUNK, QH), F32), pltpu.VMEM((SEL_KEY_CHUNK, QH), F32)],
        compiler_params=_params("parallel", "arbitrary"),
        name="nsa_attention",
    )(pt, pt, pt, rope, cmp_rm, cmp_t, kskw, vt, vs2, jnp.asarray(ov, BF16), jnp.asarray(onehot, BF16), gb,
      out_g.reshape(W, 1))


def _mem_kv_kernel(mem_ref, g_ref, w_ref, wt_ref, k_ref, vt_ref):
    x = mem_ref[0]
    h = x * lax.rsqrt(jnp.mean(x * x, axis=-1, keepdims=True) + NORM_EPS) * g_ref[...]
    hb = h.astype(BF16)
    k_ref[0] = _dot(hb, w_ref[...]).astype(BF16)
    vt = _dot_nt(wt_ref[...], hb)
    ones = jnp.ones((ONES_ROWS, vt.shape[1]), F32)
    vt_ref[0] = jnp.concatenate(
        [part for h in range(MEM_HEADS) for part in (vt[h * HEAD_DIM:(h + 1) * HEAD_DIM], ones)],
        axis=0).astype(BF16)


def _mem_kv(mem, g, w_kv):
    B, M, D = mem.shape
    W = MEM_WIDTH
    const = lambda shape: pl.BlockSpec(shape, lambda b: (0,) * len(shape))
    return pl.pallas_call(
        _mem_kv_kernel,
        grid=(B,),
        in_specs=[pl.BlockSpec((1, M, D), lambda b: (b, 0, 0)), const((1, D)), const((D, W)),
                  const((W, D))],
        out_specs=[pl.BlockSpec((1, M, W), lambda b: (b, 0, 0)),
                   pl.BlockSpec((1, MEM_HEADS * VAL_ROWS, M), lambda b: (b, 0, 0))],
        out_shape=[jax.ShapeDtypeStruct((B, M, W), BF16),
                   jax.ShapeDtypeStruct((B, MEM_HEADS * VAL_ROWS, M), BF16)],
        compiler_params=_params("parallel"),
        name="mem_kv",
    )(mem, g.reshape(1, D), w_kv[:, :W].astype(BF16), w_kv[:, W:].T.astype(BF16))


def _mem_attn_kernel(q_ref, gate_ref, k_ref, vt_ref, og_ref, y_ref):
    D, Hm = HEAD_DIM, MEM_HEADS
    q = q_ref[0] * (D ** -0.5 * LOG2E)
    k = k_ref[0]
    vt = vt_ref[0]
    row = lax.broadcasted_iota(jnp.int32, (Hm * D, 1), 0)
    ys = []
    for h in range(Hm):
        qh = jnp.where(row // D == h, q, 0.0).astype(BF16)
        s = _dot(k, qh)
        eb = jnp.exp2(s - jnp.max(s, axis=0, keepdims=True)).astype(BF16)
        ol = _dot(vt[h * VAL_ROWS:(h + 1) * VAL_ROWS], eb)
        o = ol[0:D] / ol[D:D + 1]
        ys.append(o * lax.rsqrt(jnp.mean(o * o, axis=0, keepdims=True) + NORM_EPS))
    y = jnp.concatenate(ys, axis=0) * og_ref[...] * _silu(gate_ref[0])
    y_ref[0] = y.T.astype(y_ref.dtype)


def _mem_attention(pt, mem_k, mem_vt, out_g, tm):
    B, _, T = pt.shape
    M = mem_k.shape[1]
    W = MEM_WIDTH
    return pl.pallas_call(
        _mem_attn_kernel,
        grid=(B, T // tm),
        in_specs=[
            pl.BlockSpec((1, W, tm), lambda b, i: (b, 3, i)),
            pl.BlockSpec((1, W, tm), lambda b, i: (b, 4, i)),
            pl.BlockSpec((1, M, W), lambda b, i: (b, 0, 0)),
            pl.BlockSpec((1, MEM_HEADS * VAL_ROWS, M), lambda b, i: (b, 0, 0)),
            pl.BlockSpec((W, 1), lambda b, i: (0, 0)),
        ],
        out_specs=pl.BlockSpec((1, tm, W), lambda b, i: (b, i, 0)),
        out_shape=jax.ShapeDtypeStruct((B, T, W), BF16),
        compiler_params=_params("parallel", "parallel"),
        name="mem_attention",
    )(pt, pt, mem_k, mem_vt, out_g.reshape(W, 1))


def _outproj_kernel(x_ref, yr_ref, yn_ref, ym_ref, wr_ref, wn_ref, wm_ref, g_ref, o_ref):
    z = (x_ref[0] + _dot(yr_ref[0], wr_ref[...]) + _dot(yn_ref[0], wn_ref[...])
         + _dot(ym_ref[0], wm_ref[...]))
    o_ref[0] = z * lax.rsqrt(jnp.mean(z * z, axis=-1, keepdims=True) + NORM_EPS) * g_ref[...]


def _output_projection(x, y_rwkv, y_nsa, y_mem, w_out, g, tm):
    B, T, D = x.shape
    wb = w_out.astype(BF16)
    w_r, w_n, w_m = wb[:RWKV_WIDTH], wb[RWKV_WIDTH:RWKV_WIDTH + NSA_WIDTH], wb[RWKV_WIDTH + NSA_WIDTH:]
    tile = lambda w: pl.BlockSpec((1, tm, w), lambda b, i: (b, i, 0))
    const = lambda shape: pl.BlockSpec(shape, lambda b, i: (0,) * len(shape))
    return pl.pallas_call(
        _outproj_kernel,
        grid=(B, T // tm),
        in_specs=[tile(D), tile(RWKV_WIDTH), tile(NSA_WIDTH), tile(MEM_WIDTH),
                  const((RWKV_WIDTH, D)), const((NSA_WIDTH, D)), const((MEM_WIDTH, D)), const((1, D))],
        out_specs=tile(D),
        out_shape=jax.ShapeDtypeStruct((B, T, D), F32),
        compiler_params=_params("parallel", "parallel"),
        name="output_projection",
    )(x, y_rwkv, y_nsa, y_mem, w_r, w_n, w_m, g.reshape(1, D))


def _rope_table(T):
    inv_freq = ROPE_THETA ** (-jnp.arange(ROPE_HALF, dtype=F32) / ROPE_HALF)
    ang = inv_freq[:, None] * jnp.arange(T).astype(F32)[None, :]
    return jnp.concatenate([jnp.cos(ang), jnp.sin(ang)], axis=0)


def _split_w_in(w):
    D = HEAD_DIM
    n0 = RWKV_COLS
    q, gate, glog = n0, n0 + NSA_WIDTH, n0 + 2 * NSA_WIDTH
    kc = glog + NSA_HEADS * N_BRANCH
    vc, ks, vs, kw, vw = kc + D, kc + 2 * D, kc + 3 * D, kc + 4 * D, kc + 5 * D
    m0 = vw + D
    cols = lambda a, n: w[:, a:a + n]
    w_row = jnp.concatenate([cols(0, n0), cols(kc, D), cols(vc, D)], axis=1)
    w_t = jnp.concatenate([cols(ks, D), cols(kw, D), cols(vs, D), cols(vw, D), cols(q, NSA_WIDTH),
                           cols(gate, NSA_WIDTH), cols(m0, MEM_WIDTH), cols(m0 + MEM_WIDTH, MEM_WIDTH),
                           cols(glog, NSA_HEADS * N_BRANCH),
                           jnp.zeros((w.shape[0], 16 - NSA_HEADS * N_BRANCH), w.dtype)], axis=1)
    return w_row.astype(BF16), w_t.T.astype(BF16)


def kernel(x, mem, norm_in_g, w_in, rwkv_mu, rwkv_w0, rwkv_w_up, rwkv_a0, rwkv_a_up, rwkv_k_k,
           rwkv_k_a, rwkv_r_k, rwkv_ln_w, rwkv_ln_b, nsa_cmp_pos, nsa_cmp_k_w1, nsa_cmp_k_w2,
           nsa_cmp_v_w1, nsa_cmp_v_w2, nsa_gate_b, nsa_out_g, mem_norm_g, w_mem_kv, mem_out_g, w_out,
           norm_final_g):
    B, T, D = x.shape
    assert w_in.shape[0] == 1, "single-layer stack: the final norm is fused into the output projection"
    rope = _rope_table(T)
    w_row, w_t = _split_w_in(w_in[0])
    p_row, kcvc, pt = _input_projection(x, norm_in_g[0].reshape(1, D), w_row, w_t, tm=256)
    y_rwkv = _rwkv_group(p_row, rwkv_mu[0], rwkv_w0[0], rwkv_w_up[0], rwkv_a0[0], rwkv_a_up[0],
                         rwkv_k_k[0], rwkv_k_a[0], rwkv_r_k[0].reshape(-1), rwkv_ln_w[0],
                         rwkv_ln_b[0])
    kskw, vt, vs2 = _nsa_keys(pt, rope, tk=512)
    cmp_rm, cmp_t = _nsa_compress(kcvc, nsa_cmp_pos[0], nsa_cmp_k_w1[0], nsa_cmp_k_w2[0],
                                  nsa_cmp_v_w1[0], nsa_cmp_v_w2[0])
    y_nsa = _nsa_attention(pt, rope, cmp_rm, cmp_t, kskw, vt, vs2, nsa_gate_b[0], nsa_out_g[0])
    mem_k, mem_vt = _mem_kv(mem, mem_norm_g[0], w_mem_kv[0])
    y_mem = _mem_attention(pt, mem_k, mem_vt, mem_out_g[0], tm=512)
    return _output_projection(x, y_rwkv, y_nsa, y_mem, w_out[0], norm_final_g, tm=512)
```

```python
import functools

import numpy as np
import jax
import jax.numpy as jnp
from jax import lax
from jax.experimental import pallas as pl
from jax.experimental.pallas import tpu as pltpu

F32 = jnp.float32
BF16 = jnp.bfloat16

HEAD_DIM = 64
RWKV_HEADS = 8
RWKV_WIDTH = RWKV_HEADS * HEAD_DIM
LORA = 64
RWKV_COLS = 4 * RWKV_WIDTH + 2 * LORA
RWKV_GN_EPS = 64e-5
NSA_HEADS = 4
NSA_WIDTH = NSA_HEADS * HEAD_DIM
N_BRANCH = 3
CMP_BLOCK = 32
CMP_STRIDE = 16
SEL_BLOCK = 64
SEL_TOPK = 16
WINDOW = 512
MEM_HEADS = 4
MEM_WIDTH = MEM_HEADS * HEAD_DIM
ROPE_THETA = 500000.0
ROPE_HALF = 8
Q_BLOCK = 128
NORM_EPS = 1e-6
NEG_INF = -1e30
FORCE_SCORE = 1e4
LOG2E = 1.4426950408889634
DECAY_SCALE = 0.6065306597126334

RWKV_CHUNK = 64
RWKV_HEAD_GROUP = 4
RWKV_BATCH_TILE = 4
SEL_KEY_CHUNK = 512
LANES = 128
MXU_DEPTH = 256
ONES_ROWS = 16
VAL_ROWS = HEAD_DIM + ONES_ROWS
VMEM_LIMIT = 48 * 1024 * 1024

def _dot(a, b):
    return jnp.dot(a, b, preferred_element_type=F32)


def _dot_nt(a, b):
    return lax.dot_general(a, b, (((1,), (1,)), ((), ())), preferred_element_type=F32)


def _dot_tn(a, b):
    return lax.dot_general(a, b, (((0,), (0,)), ((), ())), preferred_element_type=F32)


def _bf(a):
    return a.astype(BF16)


def _split_bf16(a, pieces):
    out = []
    for _ in range(pieces):
        part = a.astype(BF16)
        out.append(part)
        a = a - part.astype(F32)
    return out


def _sigmoid(x):
    return 1.0 / (1.0 + jnp.exp(-x))


def _silu(x):
    return x * _sigmoid(x)


def _params(*sem):
    return pltpu.CompilerParams(dimension_semantics=sem, vmem_limit_bytes=VMEM_LIMIT)


def _inproj_kernel(x_ref, g_ref, wrow_ref, wt_ref, prow_ref, kc_ref, pt_ref):
    x = x_ref[0]
    h = x * lax.rsqrt(jnp.mean(x * x, axis=-1, keepdims=True) + NORM_EPS) * g_ref[...]
    hb = h.astype(BF16)
    row = _dot(hb, wrow_ref[...])
    prow_ref[0] = row[:, :RWKV_COLS]
    kc_ref[0] = row[:, RWKV_COLS:]
    pt_ref[0] = _dot_nt(wt_ref[...], hb)


def _input_projection(x, g, w_row, w_t, tm):
    B, T, D = x.shape
    n_row = w_row.shape[1]
    n_t = w_t.shape[0]
    return pl.pallas_call(
        _inproj_kernel,
        grid=(B, T // tm),
        in_specs=[
            pl.BlockSpec((1, tm, D), lambda b, i: (b, i, 0)),
            pl.BlockSpec((1, D), lambda b, i: (0, 0)),
            pl.BlockSpec((D, n_row), lambda b, i: (0, 0)),
            pl.BlockSpec((n_t, D), lambda b, i: (0, 0)),
        ],
        out_specs=[
            pl.BlockSpec((1, tm, RWKV_COLS), lambda b, i: (b, i, 0)),
            pl.BlockSpec((1, tm, n_row - RWKV_COLS), lambda b, i: (b, i, 0)),
            pl.BlockSpec((1, n_t, tm), lambda b, i: (b, 0, i)),
        ],
        out_shape=[
            jax.ShapeDtypeStruct((B, T, RWKV_COLS), F32),
            jax.ShapeDtypeStruct((B, T, n_row - RWKV_COLS), F32),
            jax.ShapeDtypeStruct((B, n_t, T), F32),
        ],
        compiler_params=_params("parallel", "parallel"),
        name="input_projection",
    )(x, g, w_row, w_t)


_X_R, _X_A, _X_K, _X_B, _X_V, _X_BV, _X_SG, _X_G = range(8)


def _rwkv_kernel(podd_ref, peven_ref, prevodd_ref, preveven_ref, mu_ref, w0_ref, wup_ref, a0_ref,
                 aup_ref, kk_ref, ka_ref, rk_ref, lnw_ref, lnb_ref, bd_ref, tri_ref, y_ref,
                 s_ref, xa_ref, xb_ref):
    C, W, N, H = RWKV_CHUNK, RWKV_WIDTH, HEAD_DIM, RWKV_HEADS
    HG = RWKV_HEAD_GROUP
    GW = HG * N
    n = pl.program_id(1)
    n_chunks = 2 * (pl.num_programs(1) - 1)
    bt = podd_ref.shape[0]

    @pl.when(n == 0)
    def _():
        s_ref[...] = jnp.zeros_like(s_ref)
        xa_ref[...] = jnp.zeros_like(xa_ref)
        xb_ref[...] = jnp.zeros_like(xb_ref)

    bd = bd_ref[...]
    head_sum = lambda a: jnp.concatenate(
        [_dot(_bf(a[:, g * GW:(g + 1) * GW]), bd) for g in range(H // HG)], axis=1)
    row = lax.broadcasted_iota(jnp.int32, (C, 1), 0)

    def prepare(b, chunk, p_ref, prev_ref, x_ref):
        p = p_ref[b]
        prev_last = jnp.where(chunk <= 0, 0.0, prev_ref[b][7:8, :])
        prev = jnp.where(row == 0, prev_last, pltpu.roll(p, 1, axis=0))
        pf = p + mu_ref[...] * (prev - p)
        r = pf[:, 0:W]
        k = pf[:, W:2 * W]
        v = pf[:, 2 * W:3 * W]
        gate = pf[:, 3 * W:4 * W]
        wd = _bf(jnp.tanh(pf[:, 4 * W:4 * W + LORA]))
        ad = _bf(pf[:, 4 * W + LORA:4 * W + 2 * LORA])
        yield
        z = w0_ref[...] + _dot(wd, wup_ref[...])
        logw = -DECAY_SCALE * _sigmoid(z)
        eta = _sigmoid(a0_ref[...] + _dot(ad, aup_ref[...]))
        kk = k * kk_ref[...]
        k2 = k * (1.0 + (eta - 1.0) * ka_ref[...])
        kk_sq = _bf(kk * kk)
        rk2 = _bf(r * k2 * rk_ref[...])
        logw_parts = _split_bf16(logw, 3)
        yield
        kk = kk * lax.rsqrt(jnp.maximum(head_sum(kk_sq), 1e-24))
        x_ref[b, _X_BV] = head_sum(rk2) * v
        x_ref[b, _X_SG] = _silu(gate)
        tri = tri_ref[...]
        cs = sum(_dot(tri, part) for part in logw_parts)
        mid = cs[C // 2 - 1:C // 2, :]
        csm = cs - mid
        end = cs[C - 1:C, :]
        x_ref[b, _X_G] = jnp.concatenate([jnp.exp(mid), jnp.exp(end), jnp.exp(end - mid),
                                          jnp.zeros((C - 3, W), F32)], axis=0)
        yield
        e_out = jnp.exp(-csm)
        x_ref[b, _X_R] = r * jnp.exp(csm)
        x_ref[b, _X_A] = -kk * jnp.exp(csm - logw)
        yield
        x_ref[b, _X_K] = k2 * e_out
        x_ref[b, _X_B] = kk * eta * e_out
        x_ref[b, _X_V] = v

    li = lax.broadcasted_iota(jnp.int32, (C, GW), 1) % N
    ti = lax.broadcasted_iota(jnp.int32, (C, GW), 0)
    strict, incl = ti > li, ti >= li
    eye = (ti == li).astype(F32)
    same_head = (lax.broadcasted_iota(jnp.int32, (GW, GW), 0) // N
                 == lax.broadcasted_iota(jnp.int32, (GW, GW), 1) // N)

    def blockdiag(a):
        a = _bf(a)
        return jnp.where(same_head, jnp.concatenate([a] * HG, axis=0), jnp.zeros((), BF16))

    def advance(chunk, x_ref, y_rows, fill):
        live = (chunk >= 0) & (chunk < n_chunks)
        chains = [(b, g) for b in range(bt) for g in range(H // HG)]
        st = {}
        for c in chains:
            b, g = c
            ls = slice(g * GW, (g + 1) * GW)
            kb = jnp.concatenate([blockdiag(x_ref[b, _X_K][:, ls]), blockdiag(x_ref[b, _X_B][:, ls])],
                                 axis=0)
            ar = _bf(jnp.concatenate([x_ref[b, _X_A][:, ls], x_ref[b, _X_R][:, ls]], axis=0))
            gc = _dot_nt(ar, kb)
            s0 = s_ref[b, g]
            gates = x_ref[b, _X_G]
            st[c] = dict(
                ls=ls, kb=kb, s0=s0, v_bd=blockdiag(x_ref[b, _X_V][:, ls]),
                g_end=gates[1:2, ls], g_end_mid=gates[2:3, ls],
                a_ak=jnp.where(strict, gc[0:C, 0:GW], 0.0), a_ab=jnp.where(strict, gc[0:C, GW:2 * GW], 0.0),
                a_rk=jnp.where(incl, gc[C:2 * C, 0:GW], 0.0), a_rb=jnp.where(incl, gc[C:2 * C, GW:2 * GW], 0.0),
                uy0=_dot_nt(ar, _bf(s0 * gates[0:1, ls])))
            fill()
        for c in chains:
            s = st[c]
            s["rhs"] = s["uy0"][0:C] + _dot(_bf(s["a_ak"]), s["v_bd"])
            s["tm"] = eye + s["a_ab"]
            s["pw"] = _dot(_bf(s["a_ab"]), blockdiag(s["a_ab"]))
            fill()
        span = 2
        while span < C:
            for c in chains:
                s = st[c]
                if 2 * span < C:
                    tp = _dot(_bf(s["pw"]), jnp.concatenate([blockdiag(s["tm"]), blockdiag(s["pw"])], axis=1))
                    s["tm"] = s["tm"] + tp[:, 0:GW]
                    s["pw"] = tp[:, GW:2 * GW]
                else:
                    s["tm"] = s["tm"] + _dot(_bf(s["pw"]), blockdiag(s["tm"]))
                fill()
            span *= 2
        for c in chains:
            s = st[c]
            s["u"] = _dot(_bf(s["tm"]), blockdiag(s["rhs"]))
            fill()
        for c in chains:
            s = st[c]
            s["vu"] = jnp.concatenate([s["v_bd"], blockdiag(s["u"])], axis=0)
            s["y"] = s["uy0"][C:2 * C] + _dot(_bf(jnp.concatenate([s["a_rk"], s["a_rb"]], axis=1)), s["vu"])
            fill()
        for c in chains:
            b, g = c
            s = st[c]
            ls = s["ls"]
            vu_rows = jnp.concatenate([_bf(x_ref[b, _X_V][:, ls]), _bf(s["u"])], axis=0)
            kb_rows = _bf(jnp.concatenate([x_ref[b, _X_K][:, ls], x_ref[b, _X_B][:, ls]], axis=0))
            upd = _dot_tn(vu_rows, kb_rows) * s["g_end_mid"]
            s_new = s["s0"] * s["g_end"] + jnp.where(same_head, upd, 0.0)
            s_ref[b, g] = jnp.where(live, s_new, s["s0"])
            fill()
        for b in range(bt):
            y = jnp.concatenate([st[(b, g)]["y"] for g in range(H // HG)], axis=1)
            mean = head_sum(y) * (1.0 / N)
            var = head_sum(jnp.square(y - mean)) * (1.0 / N)
            y = (y - mean) * lax.rsqrt(var + RWKV_GN_EPS) * lnw_ref[...] + lnb_ref[...] + x_ref[b, _X_BV]
            y_ref[b, y_rows] = (y * x_ref[b, _X_SG]).astype(y_ref.dtype)
            fill()

    def filler(gens):
        gens = list(gens)

        def fill():
            while gens:
                try:
                    next(gens[0])
                    return
                except StopIteration:
                    gens.pop(0)

        def drain():
            while gens:
                fill()

        return fill, drain

    fill, drain = filler(prepare(b, 2 * n - 1, podd_ref, prevodd_ref, xb_ref) for b in range(bt))
    advance(2 * n - 2, xa_ref, slice(0, C), fill)
    drain()
    fill, drain = filler(prepare(b, 2 * n, peven_ref, preveven_ref, xa_ref) for b in range(bt))
    advance(2 * n - 1, xb_ref, slice(C, 2 * C), fill)
    drain()


def _rwkv_group(p_row, mu, w0, w_up, a0, a_up, k_k, k_a, r_k, ln_w, ln_b):
    B, T, _ = p_row.shape
    C, W = RWKV_CHUNK, RWKV_WIDTH
    heads = np.arange(RWKV_HEAD_GROUP * HEAD_DIM) // HEAD_DIM
    bd = jnp.asarray(heads[:, None] == heads[None, :], BF16)
    tri = jnp.asarray(np.tril(np.ones((C, C), np.float32)), BF16)
    vec = lambda a: a.reshape(1, -1)
    const = lambda shape: pl.BlockSpec(shape, lambda b, n: (0,) * len(shape))
    bt = RWKV_BATCH_TILE if B % RWKV_BATCH_TILE == 0 else 1
    n_chunks = T // C
    assert n_chunks % 2 == 0
    odd = lambda n: jnp.maximum(2 * n - 1, 0)
    even = lambda n: jnp.minimum(2 * n, n_chunks - 1)
    before = lambda c: jnp.maximum(c * (C // 8) - 1, 0)
    return pl.pallas_call(
        _rwkv_kernel,
        grid=(B // bt, n_chunks // 2 + 1),
        in_specs=[
            pl.BlockSpec((bt, C, RWKV_COLS), lambda b, n: (b, odd(n), 0)),
            pl.BlockSpec((bt, C, RWKV_COLS), lambda b, n: (b, even(n), 0)),
            pl.BlockSpec((bt, 8, RWKV_COLS), lambda b, n: (b, before(odd(n)), 0)),
            pl.BlockSpec((bt, 8, RWKV_COLS), lambda b, n: (b, before(even(n)), 0)),
            const((1, RWKV_COLS)), const((1, W)), const((LORA, W)), const((1, W)), const((LORA, W)),
            const((1, W)), const((1, W)), const((1, W)), const((1, W)), const((1, W)),
            const(bd.shape), const((C, C)),
        ],
        out_specs=pl.BlockSpec((bt, 2 * C, W), lambda b, n: (b, jnp.maximum(n - 1, 0), 0)),
        out_shape=jax.ShapeDtypeStruct((B, T, W), BF16),
        scratch_shapes=[pltpu.VMEM((bt, RWKV_HEADS // RWKV_HEAD_GROUP) + (RWKV_HEAD_GROUP * HEAD_DIM,) * 2,
                                   F32),
                        pltpu.VMEM((bt, 8, C, W), F32), pltpu.VMEM((bt, 8, C, W), F32)],
        compiler_params=_params("parallel", "arbitrary"),
        name="rwkv7_group",
    )(p_row, p_row, p_row, p_row, vec(mu), vec(w0), _bf(w_up), vec(a0), _bf(a_up), vec(k_k), vec(k_a),
      vec(r_k), vec(ln_w), vec(ln_b), bd, tri)


def _rope_rows(x, cos, sin):
    x1, x2 = x[0:ROPE_HALF], x[ROPE_HALF:2 * ROPE_HALF]
    return jnp.concatenate([x1 * cos - x2 * sin, x2 * cos + x1 * sin, x[2 * ROPE_HALF:]], axis=0)


def _nsa_keys_kernel(kv_ref, rope_ref, kskw_ref, vt_ref, vs2_ref):
    kv = kv_ref[0]
    cos, sin = rope_ref[0:ROPE_HALF], rope_ref[ROPE_HALF:2 * ROPE_HALF]
    keys = jnp.concatenate([_rope_rows(kv[0:HEAD_DIM], cos, sin),
                            _rope_rows(kv[HEAD_DIM:2 * HEAD_DIM], cos, sin)], axis=0)
    kskw_ref[0] = keys.T.astype(BF16)
    vals = kv[2 * HEAD_DIM:4 * HEAD_DIM].astype(BF16)
    ones = jnp.ones((ONES_ROWS, LANES), BF16)
    for u in range(vals.shape[1] // LANES):
        blk = vals[:, u * LANES:(u + 1) * LANES]
        vt_ref[0, u] = jnp.concatenate([blk[0:HEAD_DIM], ones, blk[HEAD_DIM:2 * HEAD_DIM], ones], axis=0)
    ones2 = jnp.ones((ONES_ROWS, MXU_DEPTH), BF16)
    for u in range(vals.shape[1] // MXU_DEPTH):
        vs2_ref[0, u] = jnp.concatenate([vals[0:HEAD_DIM, u * MXU_DEPTH:(u + 1) * MXU_DEPTH], ones2], axis=0)


def _nsa_keys(pt, rope, tk):
    B, _, T = pt.shape
    return pl.pallas_call(
        _nsa_keys_kernel,
        grid=(B, T // tk),
        in_specs=[
            pl.BlockSpec((1, 4 * HEAD_DIM, tk), lambda b, i: (b, 0, i)),
            pl.BlockSpec((2 * ROPE_HALF, tk), lambda b, i: (0, i)),
        ],
        out_specs=[
            pl.BlockSpec((1, tk, LANES), lambda b, i: (b, i, 0)),
            pl.BlockSpec((1, tk // LANES, 2 * VAL_ROWS, LANES), lambda b, i: (b, i, 0, 0)),
            pl.BlockSpec((1, tk // MXU_DEPTH, VAL_ROWS, MXU_DEPTH), lambda b, i: (b, i, 0, 0)),
        ],
        out_shape=[
            jax.ShapeDtypeStruct((B, T, LANES), BF16),
            jax.ShapeDtypeStruct((B, T // LANES, 2 * VAL_ROWS, LANES), BF16),
            jax.ShapeDtypeStruct((B, T // MXU_DEPTH, VAL_ROWS, MXU_DEPTH), BF16),
        ],
        compiler_params=_params("parallel", "parallel"),
        name="nsa_keys",
    )(pt, rope)


def _nsa_compress_kernel(g_ref, wc_ref, pos_ref, w2_ref, w2t_ref, rm_ref, t_ref):
    ng = g_ref.shape[1]
    wc = wc_ref[...]
    m = _dot(g_ref[0].astype(BF16), wc)
    pm = _dot(pos_ref[...], wc)
    pos_term = pm[0:1, 0:LANES] + pm[1:2, LANES:2 * LANES]
    pre = m[:, 0:LANES] + pltpu.roll(m[:, LANES:2 * LANES], ng - 1, axis=0) + pos_term
    act = _silu(pre).astype(BF16)
    row = lax.broadcasted_iota(jnp.int32, (ng, 1), 0)
    col = lax.broadcasted_iota(jnp.int32, (1, ng), 1)
    rm_ref[0] = jnp.where(row < ng - 1, _dot(act, w2_ref[...]), 0.0).astype(BF16)
    vt = jnp.where(col < ng - 1, _dot_nt(w2t_ref[...], act)[HEAD_DIM:2 * HEAD_DIM], 0.0)
    t_ref[0] = jnp.concatenate([vt, jnp.ones((ONES_ROWS, ng), F32)], axis=0).astype(BF16)


def _nsa_compress(kcvc, cmp_pos, k_w1, k_w2, v_w1, v_w2):
    B, T, _ = kcvc.shape
    ng = T // CMP_STRIDE
    half = CMP_BLOCK // 2
    g = kcvc.reshape(B, ng, half * LANES)
    D = HEAD_DIM

    def spread(w, second, is_v):
        blk = w[second * half * D:(second + 1) * half * D].reshape(half, D, D)
        z = jnp.zeros_like(blk)
        return jnp.concatenate([z, blk] if is_v else [blk, z], axis=1).reshape(half * LANES, D)

    wc = jnp.concatenate([spread(k_w1, 0, False), spread(v_w1, 0, True),
                          spread(k_w1, 1, False), spread(v_w1, 1, True)], axis=1).astype(BF16)
    pos2 = jnp.concatenate([cmp_pos, cmp_pos], axis=1)
    pos = jnp.zeros((8, half * LANES), F32)
    pos = pos.at[0].set(pos2[:half].reshape(-1)).at[1].set(pos2[half:].reshape(-1)).astype(BF16)
    z = jnp.zeros((D, D), F32)
    w2 = jnp.block([[k_w2, z], [z, v_w2]])
    const = lambda shape: pl.BlockSpec(shape, lambda b: (0,) * len(shape))
    return pl.pallas_call(
        _nsa_compress_kernel,
        grid=(B,),
        in_specs=[
            pl.BlockSpec((1, ng, half * LANES), lambda b: (b, 0, 0)),
            const((half * LANES, 2 * LANES)), const((8, half * LANES)),
            const((LANES, LANES)), const((LANES, LANES)),
        ],
        out_specs=[
            pl.BlockSpec((1, ng, LANES), lambda b: (b, 0, 0)),
            pl.BlockSpec((1, VAL_ROWS, ng), lambda b: (b, 0, 0)),
        ],
        out_shape=[
            jax.ShapeDtypeStruct((B, ng, LANES), BF16),
            jax.ShapeDtypeStruct((B, VAL_ROWS, ng), BF16),
        ],
        compiler_params=_params("parallel"),
        name="nsa_compress",
    )(g, wc, pos, w2.astype(BF16), w2.T.astype(BF16))


def _tile_heads(x):
    return jnp.concatenate([x] * NSA_HEADS, axis=1)


def _nsa_attn_kernel(q_ref, gate_ref, glog_ref, rope_ref, rm_ref, ct_ref, kskw_ref, vt_ref, vs2_ref,
                     ov_ref, oh_ref, gb_ref, og_ref, y_ref, m_ref, acc_ref, sa_ref, sb_ref):
    D, Q, Hn = HEAD_DIM, Q_BLOCK, NSA_HEADS
    KC = SEL_KEY_CHUNK
    ng = rm_ref.shape[1]
    ns = ov_ref.shape[0]
    n_top = min(SEL_TOPK, ns)
    i = pl.program_id(1)
    t0 = i * Q
    tq = t0 + lax.broadcasted_iota(jnp.int32, (1, Q), 1)

    q = q_ref[0] * (D ** -0.5 * LOG2E)
    cos, sin = rope_ref[0:ROPE_HALF], rope_ref[ROPE_HALF:2 * ROPE_HALF]
    qh = [q[h * D:(h + 1) * D] for h in range(Hn)]
    q4 = jnp.concatenate(qh, axis=1)
    q4r = jnp.concatenate([_rope_rows(x, cos, sin) for x in qh], axis=1)
    zero = jnp.zeros_like(q4)
    q_lo = jnp.concatenate([q4, zero], axis=0).astype(BF16)
    qr_lo = jnp.concatenate([q4r, zero], axis=0).astype(BF16)
    qr_hi = jnp.concatenate([zero, q4r], axis=0).astype(BF16)

    def masked(s, bias):
        return jnp.concatenate([s[:, h * Q:(h + 1) * Q] + bias for h in range(Hn)], axis=1)


    cend = lax.broadcasted_iota(jnp.int32, (ng, 1), 0) * CMP_STRIDE + (CMP_BLOCK - 1)
    s = masked(_dot(rm_ref[0], q_lo), jnp.where(cend <= tq, 0.0, NEG_INF))
    eb = jnp.exp2(s - jnp.max(s, axis=0, keepdims=True)).astype(BF16)
    ol = _dot(ct_ref[0], eb)
    seen = _tile_heads((tq >= CMP_BLOCK - 1).astype(F32))
    inv = seen / jnp.maximum(ol[D:D + 1], 1e-30)
    o_c = ol[0:D] * inv
    psum = sum(eb[:, h * Q:(h + 1) * Q] * inv[:, h * Q:(h + 1) * Q] for h in range(Hn))
    imp = _dot(ov_ref[...], psum.astype(BF16))

    WK = WINDOW + Q
    w0 = pl.multiple_of(jnp.maximum(t0 - WINDOW, 0), LANES)
    diff = tq - (w0 + lax.broadcasted_iota(jnp.int32, (WK, 1), 0))
    w_bias = jnp.where((diff >= 0) & (diff < WINDOW), 0.0, NEG_INF)

    def window_head(h):
        s = _dot(kskw_ref[0, pl.ds(w0, WK), :], qr_hi[:, h * Q:(h + 1) * Q]) + w_bias
        eb = jnp.exp2(s - jnp.max(s, axis=0, keepdims=True)).astype(BF16)
        ol = _dot(vt_ref[0, w0 // LANES][VAL_ROWS:2 * VAL_ROWS], eb[0:LANES])
        for u in range(1, WK // LANES):
            ol = ol + _dot(vt_ref[0, w0 // LANES + u][VAL_ROWS:2 * VAL_ROWS], eb[u * LANES:(u + 1) * LANES])
        return ol[0:D] / jnp.maximum(ol[D:D + 1], 1e-30)

    blk = lax.broadcasted_iota(jnp.int32, (ns, Q), 0)
    cur = tq // SEL_BLOCK
    forced = (blk == 0) | (blk == cur) | (blk == cur - 1)
    taken = -2.0
    score = jnp.where(forced, taken, jnp.where(blk <= cur, imp, -1.0))
    picks = max(n_top - 3, 0)
    o_w = []
    for it in range(picks):
        best = jnp.max(score, axis=0, keepdims=True)
        first = jnp.min(jnp.where(score == best, blk, ns), axis=0, keepdims=True)
        score = jnp.where(blk == first, taken, score)
        if it % 3 == 0 and len(o_w) < Hn:
            o_w.append(window_head(len(o_w)))
    while len(o_w) < Hn:
        o_w.append(window_head(len(o_w)))
    o_w = jnp.concatenate(o_w, axis=1)

    lk = lax.broadcasted_iota(jnp.int32, (Q, 1), 0)
    lq = lax.broadcasted_iota(jnp.int32, (1, Q), 1)
    s = masked(_dot(kskw_ref[0, pl.ds(pl.multiple_of(t0, Q), Q), :], qr_lo),
               jnp.where(lk <= lq, 0.0, NEG_INF))
    m0 = jnp.max(s, axis=0, keepdims=True)
    m_ref[...] = m0
    acc_ref[...] = _dot(vt_ref[0, i][0:VAL_ROWS], jnp.exp2(s - m0).astype(BF16))

    before = (score == taken) & (blk < t0 // SEL_BLOCK)
    sel_bias = _tile_heads(jnp.where(before, 0.0, NEG_INF))
    pad = oh_ref.shape[1] - ns
    if pad:
        sel_bias = jnp.concatenate([sel_bias, jnp.zeros((pad, Hn * Q), F32)], axis=0)
    q_sel = jnp.concatenate([qr_lo, sel_bias.astype(BF16)], axis=0)

    def sel_scores(j):
        k0 = pl.multiple_of(j * KC, KC)
        keys = jnp.concatenate([kskw_ref[0, pl.ds(k0, KC), :], oh_ref[pl.ds(k0, KC), :]], axis=1)
        return _dot(keys, q_sel)

    UB = KC // MXU_DEPTH

    def sel_update(j, s):
        m_old = m_ref[...]
        m_new = jnp.maximum(m_old, jnp.max(s, axis=0, keepdims=True))
        eb = jnp.exp2(s - m_new).astype(BF16)
        pv = acc_ref[...] * jnp.exp2(m_old - m_new)
        for u in range(UB):
            pv = pv + _dot(vs2_ref[0, j * UB + u], eb[u * MXU_DEPTH:(u + 1) * MXU_DEPTH])
        acc_ref[...] = pv
        m_ref[...] = m_new

    n_chunks = (t0 + KC - 1) // KC
    last = kskw_ref.shape[1] // KC - 1
    sa_ref[...] = sel_scores(0)

    def chunk_pair(j, carry):
        sb_ref[...] = sel_scores(jnp.minimum(2 * j + 1, last))
        sel_update(2 * j, sa_ref[...])
        sa_ref[...] = sel_scores(jnp.minimum(2 * j + 2, last))
        sel_update(jnp.minimum(2 * j + 1, last), sb_ref[...])
        return carry

    lax.fori_loop(0, (n_chunks + 1) // 2, chunk_pair, 0)
    acc = acc_ref[...]
    o_s = acc[0:D] / jnp.maximum(acc[D:D + 1], 1e-30)

    gl = _sigmoid(glog_ref[0] + gb_ref[...])
    ys = []
    for h in range(Hn):
        cs = slice(h * Q, (h + 1) * Q)
        o = (gl[3 * h:3 * h + 1] * o_c[:, cs] + gl[3 * h + 1:3 * h + 2] * o_s[:, cs]
             + gl[3 * h + 2:3 * h + 3] * o_w[:, cs])
        o = o * lax.rsqrt(jnp.mean(o * o, axis=0, keepdims=True) + NORM_EPS)
        ys.append(o)
    y = jnp.concatenate(ys, axis=0) * og_ref[...] * _silu(gate_ref[0])
    y_ref[0] = y.T.astype(y_ref.dtype)


def _nsa_attention(pt, rope, cmp_rm, cmp_t, kskw, vt, vs2, gate_b, out_g):
    B, _, T = pt.shape
    ng = T // CMP_STRIDE
    ns = T // SEL_BLOCK
    Q, W = Q_BLOCK, NSA_WIDTH
    QH = Q * NSA_HEADS
    c0 = np.arange(ng)[None, :] * CMP_STRIDE
    s0 = np.arange(ns)[:, None] * SEL_BLOCK
    ov = np.clip(np.minimum(c0 + CMP_BLOCK, s0 + SEL_BLOCK) - np.maximum(c0, s0), 0, None) / CMP_BLOCK
    ov[:, ng - 1] = 0.0
    oh_lanes = -(-ns // LANES) * LANES
    onehot = (np.arange(T)[:, None] // SEL_BLOCK) == np.arange(oh_lanes)[None, :]
    gb = jnp.zeros((16, 1), F32).at[:NSA_HEADS * N_BRANCH, 0].set(gate_b)
    glog_blk = (4 * HEAD_DIM + 4 * W) // 16
    return pl.pallas_call(
        _nsa_attn_kernel,
        grid=(B, T // Q),
        in_specs=[
            pl.BlockSpec((1, W, Q), lambda b, i: (b, 1, i)),
            pl.BlockSpec((1, W, Q), lambda b, i: (b, 2, i)),
            pl.BlockSpec((1, 16, Q), lambda b, i: (b, glog_blk, i)),
            pl.BlockSpec((2 * ROPE_HALF, Q), lambda b, i: (0, i)),
            pl.BlockSpec((1, ng, LANES), lambda b, i: (b, 0, 0)),
            pl.BlockSpec((1, VAL_ROWS, ng), lambda b, i: (b, 0, 0)),
            pl.BlockSpec((1, T, LANES), lambda b, i: (b, 0, 0)),
            pl.BlockSpec((1, T // LANES, 2 * VAL_ROWS, LANES), lambda b, i: (b, 0, 0, 0)),
            pl.BlockSpec((1, T // MXU_DEPTH, VAL_ROWS, MXU_DEPTH), lambda b, i: (b, 0, 0, 0)),
            pl.BlockSpec((ns, ng), lambda b, i: (0, 0)),
            pl.BlockSpec((T, oh_lanes), lambda b, i: (0, 0)),
            pl.BlockSpec((16, 1), lambda b, i: (0, 0)),
            pl.BlockSpec((W, 1), lambda b, i: (0, 0)),
        ],
        out_specs=pl.BlockSpec((1, Q, W), lambda b, i: (b, i, 0)),
        out_shape=jax.ShapeDtypeStruct((B, T, W), BF16),
        scratch_shapes=[pltpu.VMEM((1, QH), F32), pltpu.VMEM((VAL_ROWS, QH), F32),
                        pltpu.VMEM((SEL_KEY_CHUNK, QH), F32), pltpu.VMEM((SEL_KEY_CHUNK, QH), F32)],
        compiler_params=_params("parallel", "arbitrary"),
        name="nsa_attention",
    )(pt, pt, pt, rope, cmp_rm, cmp_t, kskw, vt, vs2, jnp.asarray(ov, BF16), jnp.asarray(onehot, BF16), gb,
      out_g.reshape(W, 1))


def _mem_kv_kernel(mem_ref, g_ref, w_ref, wt_ref, k_ref, vt_ref):
    x = mem_ref[0]
    h = x * lax.rsqrt(jnp.mean(x * x, axis=-1, keepdims=True) + NORM_EPS) * g_ref[...]
    hb = h.astype(BF16)
    k_ref[0] = _dot(hb, w_ref[...]).astype(BF16)
    vt = _dot_nt(wt_ref[...], hb)
    ones = jnp.ones((ONES_ROWS, vt.shape[1]), F32)
    vt_ref[0] = jnp.concatenate(
        [part for h in range(MEM_HEADS) for part in (vt[h * HEAD_DIM:(h + 1) * HEAD_DIM], ones)],
        axis=0).astype(BF16)


def _mem_kv(mem, g, w_kv):
    B, M, D = mem.shape
    W = MEM_WIDTH
    const = lambda shape: pl.BlockSpec(shape, lambda b: (0,) * len(shape))
    return pl.pallas_call(
        _mem_kv_kernel,
        grid=(B,),
        in_specs=[pl.BlockSpec((1, M, D), lambda b: (b, 0, 0)), const((1, D)), const((D, W)),
                  const((W, D))],
        out_specs=[pl.BlockSpec((1, M, W), lambda b: (b, 0, 0)),
                   pl.BlockSpec((1, MEM_HEADS * VAL_ROWS, M), lambda b: (b, 0, 0))],
        out_shape=[jax.ShapeDtypeStruct((B, M, W), BF16),
                   jax.ShapeDtypeStruct((B, MEM_HEADS * VAL_ROWS, M), BF16)],
        compiler_params=_params("parallel"),
        name="mem_kv",
    )(mem, g.reshape(1, D), w_kv[:, :W].astype(BF16), w_kv[:, W:].T.astype(BF16))


def _mem_attn_kernel(q_ref, gate_ref, k_ref, vt_ref, og_ref, y_ref):
    D, Hm = HEAD_DIM, MEM_HEADS
    q = q_ref[0] * (D ** -0.5 * LOG2E)
    k = k_ref[0]
    vt = vt_ref[0]
    row = lax.broadcasted_iota(jnp.int32, (Hm * D, 1), 0)
    ys = []
    for h in range(Hm):
        qh = jnp.where(row // D == h, q, 0.0).astype(BF16)
        s = _dot(k, qh)
        eb = jnp.exp2(s - jnp.max(s, axis=0, keepdims=True)).astype(BF16)
        ol = _dot(vt[h * VAL_ROWS:(h + 1) * VAL_ROWS], eb)
        o = ol[0:D] / ol[D:D + 1]
        ys.append(o * lax.rsqrt(jnp.mean(o * o, axis=0, keepdims=True) + NORM_EPS))
    y = jnp.concatenate(ys, axis=0) * og_ref[...] * _silu(gate_ref[0])
    y_ref[0] = y.T.astype(y_ref.dtype)


def _mem_attention(pt, mem_k, mem_vt, out_g, tm):
    B, _, T = pt.shape
    M = mem_k.shape[1]
    W = MEM_WIDTH
    return pl.pallas_call(
        _mem_attn_kernel,
        grid=(B, T // tm),
        in_specs=[
            pl.BlockSpec((1, W, tm), lambda b, i: (b, 3, i)),
            pl.BlockSpec((1, W, tm), lambda b, i: (b, 4, i)),
            pl.BlockSpec((1, M, W), lambda b, i: (b, 0, 0)),
            pl.BlockSpec((1, MEM_HEADS * VAL_ROWS, M), lambda b, i: (b, 0, 0)),
            pl.BlockSpec((W, 1), lambda b, i: (0, 0)),
        ],
        out_specs=pl.BlockSpec((1, tm, W), lambda b, i: (b, i, 0)),
        out_shape=jax.ShapeDtypeStruct((B, T, W), BF16),
        compiler_params=_params("parallel", "parallel"),
        name="mem_attention",
    )(pt, pt, mem_k, mem_vt, out_g.reshape(W, 1))


def _outproj_kernel(x_ref, yr_ref, yn_ref, ym_ref, wr_ref, wn_ref, wm_ref, g_ref, o_ref):
    z = (x_ref[0] + _dot(yr_ref[0], wr_ref[...]) + _dot(yn_ref[0], wn_ref[...])
         + _dot(ym_ref[0], wm_ref[...]))
    o_ref[0] = z * lax.rsqrt(jnp.mean(z * z, axis=-1, keepdims=True) + NORM_EPS) * g_ref[...]


def _output_projection(x, y_rwkv, y_nsa, y_mem, w_out, g, tm):
    B, T, D = x.shape
    wb = w_out.astype(BF16)
    w_r, w_n, w_m = wb[:RWKV_WIDTH], wb[RWKV_WIDTH:RWKV_WIDTH + NSA_WIDTH], wb[RWKV_WIDTH + NSA_WIDTH:]
    tile = lambda w: pl.BlockSpec((1, tm, w), lambda b, i: (b, i, 0))
    const = lambda shape: pl.BlockSpec(shape, lambda b, i: (0,) * len(shape))
    return pl.pallas_call(
        _outproj_kernel,
        grid=(B, T // tm),
        in_specs=[tile(D), tile(RWKV_WIDTH), tile(NSA_WIDTH), tile(MEM_WIDTH),
                  const((RWKV_WIDTH, D)), const((NSA_WIDTH, D)), const((MEM_WIDTH, D)), const((1, D))],
        out_specs=tile(D),
        out_shape=jax.ShapeDtypeStruct((B, T, D), F32),
        compiler_params=_params("parallel", "parallel"),
        name="output_projection",
    )(x, y_rwkv, y_nsa, y_mem, w_r, w_n, w_m, g.reshape(1, D))


def _rope_table(T):
    inv_freq = ROPE_THETA ** (-jnp.arange(ROPE_HALF, dtype=F32) / ROPE_HALF)
    ang = inv_freq[:, None] * jnp.arange(T).astype(F32)[None, :]
    return jnp.concatenate([jnp.cos(ang), jnp.sin(ang)], axis=0)


def _split_w_in(w):
    D = HEAD_DIM
    n0 = RWKV_COLS
    q, gate, glog = n0, n0 + NSA_WIDTH, n0 + 2 * NSA_WIDTH
    kc = glog + NSA_HEADS * N_BRANCH
    vc, ks, vs, kw, vw = kc + D, kc + 2 * D, kc + 3 * D, kc + 4 * D, kc + 5 * D
    m0 = vw + D
    cols = lambda a, n: w[:, a:a + n]
    w_row = jnp.concatenate([cols(0, n0), cols(kc, D), cols(vc, D)], axis=1)
    w_t = jnp.concatenate([cols(ks, D), cols(kw, D), cols(vs, D), cols(vw, D), cols(q, NSA_WIDTH),
                           cols(gate, NSA_WIDTH), cols(m0, MEM_WIDTH), cols(m0 + MEM_WIDTH, MEM_WIDTH),
                           cols(glog, NSA_HEADS * N_BRANCH),
                           jnp.zeros((w.shape[0], 16 - NSA_HEADS * N_BRANCH), w.dtype)], axis=1)
    return w_row.astype(BF16), w_t.T.astype(BF16)


def kernel(x, mem, norm_in_g, w_in, rwkv_mu, rwkv_w0, rwkv_w_up, rwkv_a0, rwkv_a_up, rwkv_k_k,
           rwkv_k_a, rwkv_r_k, rwkv_ln_w, rwkv_ln_b, nsa_cmp_pos, nsa_cmp_k_w1, nsa_cmp_k_w2,
           nsa_cmp_v_w1, nsa_cmp_v_w2, nsa_gate_b, nsa_out_g, mem_norm_g, w_mem_kv, mem_out_g, w_out,
           norm_final_g):
    B, T, D = x.shape
    assert w_in.shape[0] == 1, "single-layer stack: the final norm is fused into the output projection"
    rope = _rope_table(T)
    w_row, w_t = _split_w_in(w_in[0])
    p_row, kcvc, pt = _input_projection(x, norm_in_g[0].reshape(1, D), w_row, w_t, tm=256)
    y_rwkv = _rwkv_group(p_row, rwkv_mu[0], rwkv_w0[0], rwkv_w_up[0], rwkv_a0[0], rwkv_a_up[0],
                         rwkv_k_k[0], rwkv_k_a[0], rwkv_r_k[0].reshape(-1), rwkv_ln_w[0],
                         rwkv_ln_b[0])
    kskw, vt, vs2 = _nsa_keys(pt, rope, tk=512)
    cmp_rm, cmp_t = _nsa_compress(kcvc, nsa_cmp_pos[0], nsa_cmp_k_w1[0], nsa_cmp_k_w2[0],
                                  nsa_cmp_v_w1[0], nsa_cmp_v_w2[0])
    y_nsa = _nsa_attention(pt, rope, cmp_rm, cmp_t, kskw, vt, vs2, nsa_gate_b[0], nsa_out_g[0])
    mem_k, mem_vt = _mem_kv(mem, mem_norm_g[0], w_mem_kv[0])
    y_mem = _mem_attention(pt, mem_k, mem_vt, mem_out_g[0], tm=512)
    return _output_projection(x, y_rwkv, y_nsa, y_mem, w_out[0], norm_final_g, tm=512)
```

```python
import functools

import numpy as np
import jax
import jax.numpy as jnp
from jax import lax
from jax.experimental import pallas as pl
from jax.experimental.pallas import tpu as pltpu

F32 = jnp.float32
BF16 = jnp.bfloat16

HEAD_DIM = 64
RWKV_HEADS = 8
RWKV_WIDTH = RWKV_HEADS * HEAD_DIM
LORA = 64
RWKV_COLS = 4 * RWKV_WIDTH + 2 * LORA
RWKV_GN_EPS = 64e-5
NSA_HEADS = 4
NSA_WIDTH = NSA_HEADS * HEAD_DIM
N_BRANCH = 3
CMP_BLOCK = 32
CMP_STRIDE = 16
SEL_BLOCK = 64
SEL_TOPK = 16
WINDOW = 512
MEM_HEADS = 4
MEM_WIDTH = MEM_HEADS * HEAD_DIM
ROPE_THETA = 500000.0
ROPE_HALF = 8
Q_BLOCK = 256
NORM_EPS = 1e-6
NEG_INF = -1e30
FORCE_SCORE = 1e4
LOG2E = 1.4426950408889634
DECAY_SCALE = 0.6065306597126334

RWKV_CHUNK = 64
RWKV_HEAD_GROUP = 4
RWKV_BATCH_TILE = 4
SEL_KEY_CHUNK = 512
LANES = 128
MXU_DEPTH = 256
ONES_ROWS = 16
VAL_ROWS = HEAD_DIM + ONES_ROWS
VMEM_LIMIT = 48 * 1024 * 1024

def _dot(a, b):
    return jnp.dot(a, b, preferred_element_type=F32)


def _dot_nt(a, b):
    return lax.dot_general(a, b, (((1,), (1,)), ((), ())), preferred_element_type=F32)


def _dot_tn(a, b):
    return lax.dot_general(a, b, (((0,), (0,)), ((), ())), preferred_element_type=F32)


def _bf(a):
    return a.astype(BF16)


def _split_bf16(a, pieces):
    out = []
    for _ in range(pieces):
        part = a.astype(BF16)
        out.append(part)
        a = a - part.astype(F32)
    return out


def _sigmoid(x):
    return 1.0 / (1.0 + jnp.exp(-x))


def _silu(x):
    return x * _sigmoid(x)


def _params(*sem):
    return pltpu.CompilerParams(dimension_semantics=sem, vmem_limit_bytes=VMEM_LIMIT)


def _inproj_kernel(x_ref, g_ref, wrow_ref, wt_ref, prow_ref, kc_ref, pt_ref):
    x = x_ref[0]
    h = x * lax.rsqrt(jnp.mean(x * x, axis=-1, keepdims=True) + NORM_EPS) * g_ref[...]
    hb = h.astype(BF16)
    row = _dot(hb, wrow_ref[...])
    prow_ref[0] = row[:, :RWKV_COLS]
    kc_ref[0] = row[:, RWKV_COLS:]
    pt_ref[0] = _dot_nt(wt_ref[...], hb)


def _input_projection(x, g, w_row, w_t, tm):
    B, T, D = x.shape
    n_row = w_row.shape[1]
    n_t = w_t.shape[0]
    return pl.pallas_call(
        _inproj_kernel,
        grid=(B, T // tm),
        in_specs=[
            pl.BlockSpec((1, tm, D), lambda b, i: (b, i, 0)),
            pl.BlockSpec((1, D), lambda b, i: (0, 0)),
            pl.BlockSpec((D, n_row), lambda b, i: (0, 0)),
            pl.BlockSpec((n_t, D), lambda b, i: (0, 0)),
        ],
        out_specs=[
            pl.BlockSpec((1, tm, RWKV_COLS), lambda b, i: (b, i, 0)),
            pl.BlockSpec((1, tm, n_row - RWKV_COLS), lambda b, i: (b, i, 0)),
            pl.BlockSpec((1, n_t, tm), lambda b, i: (b, 0, i)),
        ],
        out_shape=[
            jax.ShapeDtypeStruct((B, T, RWKV_COLS), F32),
            jax.ShapeDtypeStruct((B, T, n_row - RWKV_COLS), F32),
            jax.ShapeDtypeStruct((B, n_t, T), F32),
        ],
        compiler_params=_params("parallel", "parallel"),
        name="input_projection",
    )(x, g, w_row, w_t)


_X_R, _X_A, _X_K, _X_B, _X_V, _X_BV, _X_SG, _X_G = range(8)


def _rwkv_kernel(podd_ref, peven_ref, prevodd_ref, preveven_ref, mu_ref, w0_ref, wup_ref, a0_ref,
                 aup_ref, kk_ref, ka_ref, rk_ref, lnw_ref, lnb_ref, bd_ref, tri_ref, y_ref,
                 s_ref, xa_ref, xb_ref):
    C, W, N, H = RWKV_CHUNK, RWKV_WIDTH, HEAD_DIM, RWKV_HEADS
    HG = RWKV_HEAD_GROUP
    GW = HG * N
    n = pl.program_id(1)
    n_chunks = 2 * (pl.num_programs(1) - 1)
    bt = podd_ref.shape[0]

    @pl.when(n == 0)
    def _():
        s_ref[...] = jnp.zeros_like(s_ref)
        xa_ref[...] = jnp.zeros_like(xa_ref)
        xb_ref[...] = jnp.zeros_like(xb_ref)

    bd = bd_ref[...]
    head_sum = lambda a: jnp.concatenate(
        [_dot(_bf(a[:, g * GW:(g + 1) * GW]), bd) for g in range(H // HG)], axis=1)
    row = lax.broadcasted_iota(jnp.int32, (C, 1), 0)

    def prepare(b, chunk, p_ref, prev_ref, x_ref):
        p = p_ref[b]
        prev_last = jnp.where(chunk <= 0, 0.0, prev_ref[b][7:8, :])
        prev = jnp.where(row == 0, prev_last, pltpu.roll(p, 1, axis=0))
        pf = p + mu_ref[...] * (prev - p)
        r = pf[:, 0:W]
        k = pf[:, W:2 * W]
        v = pf[:, 2 * W:3 * W]
        gate = pf[:, 3 * W:4 * W]
        wd = _bf(jnp.tanh(pf[:, 4 * W:4 * W + LORA]))
        ad = _bf(pf[:, 4 * W + LORA:4 * W + 2 * LORA])
        yield
        z = w0_ref[...] + _dot(wd, wup_ref[...])
        logw = -DECAY_SCALE * _sigmoid(z)
        eta = _sigmoid(a0_ref[...] + _dot(ad, aup_ref[...]))
        kk = k * kk_ref[...]
        k2 = k * (1.0 + (eta - 1.0) * ka_ref[...])
        kk_sq = _bf(kk * kk)
        rk2 = _bf(r * k2 * rk_ref[...])
        logw_parts = _split_bf16(logw, 3)
        yield
        kk = kk * lax.rsqrt(jnp.maximum(head_sum(kk_sq), 1e-24))
        x_ref[b, _X_BV] = head_sum(rk2) * v
        x_ref[b, _X_SG] = _silu(gate)
        tri = tri_ref[...]
        cs = sum(_dot(tri, part) for part in logw_parts)
        mid = cs[C // 2 - 1:C // 2, :]
        csm = cs - mid
        end = cs[C - 1:C, :]
        x_ref[b, _X_G] = jnp.concatenate([jnp.exp(mid), jnp.exp(end), jnp.exp(end - mid),
                                          jnp.zeros((C - 3, W), F32)], axis=0)
        yield
        e_out = jnp.exp(-csm)
        x_ref[b, _X_R] = r * jnp.exp(csm)
        x_ref[b, _X_A] = -kk * jnp.exp(csm - logw)
        yield
        x_ref[b, _X_K] = k2 * e_out
        x_ref[b, _X_B] = kk * eta * e_out
        x_ref[b, _X_V] = v

    li = lax.broadcasted_iota(jnp.int32, (C, GW), 1) % N
    ti = lax.broadcasted_iota(jnp.int32, (C, GW), 0)
    strict, incl = ti > li, ti >= li
    eye = (ti == li).astype(F32)
    same_head = (lax.broadcasted_iota(jnp.int32, (GW, GW), 0) // N
                 == lax.broadcasted_iota(jnp.int32, (GW, GW), 1) // N)

    def blockdiag(a):
        a = _bf(a)
        return jnp.where(same_head, jnp.concatenate([a] * HG, axis=0), jnp.zeros((), BF16))

    def advance(chunk, x_ref, y_rows, fill):
        live = (chunk >= 0) & (chunk < n_chunks)
        chains = [(b, g) for b in range(bt) for g in range(H // HG)]
        st = {}
        for c in chains:
            b, g = c
            ls = slice(g * GW, (g + 1) * GW)
            kb = jnp.concatenate([blockdiag(x_ref[b, _X_K][:, ls]), blockdiag(x_ref[b, _X_B][:, ls])],
                                 axis=0)
            ar = _bf(jnp.concatenate([x_ref[b, _X_A][:, ls], x_ref[b, _X_R][:, ls]], axis=0))
            gc = _dot_nt(ar, kb)
            s0 = s_ref[b, g]
            gates = x_ref[b, _X_G]
            st[c] = dict(
                ls=ls, kb=kb, s0=s0, v_bd=blockdiag(x_ref[b, _X_V][:, ls]),
                g_end=gates[1:2, ls], g_end_mid=gates[2:3, ls],
                a_ak=jnp.where(strict, gc[0:C, 0:GW], 0.0), a_ab=jnp.where(strict, gc[0:C, GW:2 * GW], 0.0),
                a_rk=jnp.where(incl, gc[C:2 * C, 0:GW], 0.0), a_rb=jnp.where(incl, gc[C:2 * C, GW:2 * GW], 0.0),
                uy0=_dot_nt(ar, _bf(s0 * gates[0:1, ls])))
            fill()
        for c in chains:
            s = st[c]
            s["rhs"] = s["uy0"][0:C] + _dot(_bf(s["a_ak"]), s["v_bd"])
            s["tm"] = eye + s["a_ab"]
            s["pw"] = _dot(_bf(s["a_ab"]), blockdiag(s["a_ab"]))
            fill()
        span = 2
        while span < C:
            for c in chains:
                s = st[c]
                if 2 * span < C:
                    tp = _dot(_bf(s["pw"]), jnp.concatenate([blockdiag(s["tm"]), blockdiag(s["pw"])], axis=1))
                    s["tm"] = s["tm"] + tp[:, 0:GW]
                    s["pw"] = tp[:, GW:2 * GW]
                else:
                    s["tm"] = s["tm"] + _dot(_bf(s["pw"]), blockdiag(s["tm"]))
                fill()
            span *= 2
        for c in chains:
            s = st[c]
            s["u"] = _dot(_bf(s["tm"]), blockdiag(s["rhs"]))
            fill()
        for c in chains:
            s = st[c]
            s["vu"] = jnp.concatenate([s["v_bd"], blockdiag(s["u"])], axis=0)
            s["y"] = s["uy0"][C:2 * C] + _dot(_bf(jnp.concatenate([s["a_rk"], s["a_rb"]], axis=1)), s["vu"])
            fill()
        for c in chains:
            b, g = c
            s = st[c]
            ls = s["ls"]
            vu_rows = jnp.concatenate([_bf(x_ref[b, _X_V][:, ls]), _bf(s["u"])], axis=0)
            kb_rows = _bf(jnp.concatenate([x_ref[b, _X_K][:, ls], x_ref[b, _X_B][:, ls]], axis=0))
            upd = _dot_tn(vu_rows, kb_rows) * s["g_end_mid"]
            s_new = s["s0"] * s["g_end"] + jnp.where(same_head, upd, 0.0)
            s_ref[b, g] = jnp.where(live, s_new, s["s0"])
            fill()
        for b in range(bt):
            y = jnp.concatenate([st[(b, g)]["y"] for g in range(H // HG)], axis=1)
            mean = head_sum(y) * (1.0 / N)
            var = head_sum(jnp.square(y - mean)) * (1.0 / N)
            y = (y - mean) * lax.rsqrt(var + RWKV_GN_EPS) * lnw_ref[...] + lnb_ref[...] + x_ref[b, _X_BV]
            y_ref[b, y_rows] = (y * x_ref[b, _X_SG]).astype(y_ref.dtype)
            fill()

    def filler(gens):
        gens = list(gens)

        def fill():
            while gens:
                try:
                    next(gens[0])
                    return
                except StopIteration:
                    gens.pop(0)

        def drain():
            while gens:
                fill()

        return fill, drain

    fill, drain = filler(prepare(b, 2 * n - 1, podd_ref, prevodd_ref, xb_ref) for b in range(bt))
    advance(2 * n - 2, xa_ref, slice(0, C), fill)
    drain()
    fill, drain = filler(prepare(b, 2 * n, peven_ref, preveven_ref, xa_ref) for b in range(bt))
    advance(2 * n - 1, xb_ref, slice(C, 2 * C), fill)
    drain()


def _rwkv_group(p_row, mu, w0, w_up, a0, a_up, k_k, k_a, r_k, ln_w, ln_b):
    B, T, _ = p_row.shape
    C, W = RWKV_CHUNK, RWKV_WIDTH
    heads = np.arange(RWKV_HEAD_GROUP * HEAD_DIM) // HEAD_DIM
    bd = jnp.asarray(heads[:, None] == heads[None, :], BF16)
    tri = jnp.asarray(np.tril(np.ones((C, C), np.float32)), BF16)
    vec = lambda a: a.reshape(1, -1)
    const = lambda shape: pl.BlockSpec(shape, lambda b, n: (0,) * len(shape))
    bt = RWKV_BATCH_TILE if B % RWKV_BATCH_TILE == 0 else 1
    n_chunks = T // C
    assert n_chunks % 2 == 0
    odd = lambda n: jnp.maximum(2 * n - 1, 0)
    even = lambda n: jnp.minimum(2 * n, n_chunks - 1)
    before = lambda c: jnp.maximum(c * (C // 8) - 1, 0)
    return pl.pallas_call(
        _rwkv_kernel,
        grid=(B // bt, n_chunks // 2 + 1),
        in_specs=[
            pl.BlockSpec((bt, C, RWKV_COLS), lambda b, n: (b, odd(n), 0)),
            pl.BlockSpec((bt, C, RWKV_COLS), lambda b, n: (b, even(n), 0)),
            pl.BlockSpec((bt, 8, RWKV_COLS), lambda b, n: (b, before(odd(n)), 0)),
            pl.BlockSpec((bt, 8, RWKV_COLS), lambda b, n: (b, before(even(n)), 0)),
            const((1, RWKV_COLS)), const((1, W)), const((LORA, W)), const((1, W)), const((LORA, W)),
            const((1, W)), const((1, W)), const((1, W)), const((1, W)), const((1, W)),
            const(bd.shape), const((C, C)),
        ],
        out_specs=pl.BlockSpec((bt, 2 * C, W), lambda b, n: (b, jnp.maximum(n - 1, 0), 0)),
        out_shape=jax.ShapeDtypeStruct((B, T, W), BF16),
        scratch_shapes=[pltpu.VMEM((bt, RWKV_HEADS // RWKV_HEAD_GROUP) + (RWKV_HEAD_GROUP * HEAD_DIM,) * 2,
                                   F32),
                        pltpu.VMEM((bt, 8, C, W), F32), pltpu.VMEM((bt, 8, C, W), F32)],
        compiler_params=_params("parallel", "arbitrary"),
        name="rwkv7_group",
    )(p_row, p_row, p_row, p_row, vec(mu), vec(w0), _bf(w_up), vec(a0), _bf(a_up), vec(k_k), vec(k_a),
      vec(r_k), vec(ln_w), vec(ln_b), bd, tri)


def _rope_rows(x, cos, sin):
    x1, x2 = x[0:ROPE_HALF], x[ROPE_HALF:2 * ROPE_HALF]
    return jnp.concatenate([x1 * cos - x2 * sin, x2 * cos + x1 * sin, x[2 * ROPE_HALF:]], axis=0)


def _nsa_keys_kernel(kv_ref, rope_ref, kskw_ref, vt_ref, vs2_ref):
    kv = kv_ref[0]
    cos, sin = rope_ref[0:ROPE_HALF], rope_ref[ROPE_HALF:2 * ROPE_HALF]
    keys = jnp.concatenate([_rope_rows(kv[0:HEAD_DIM], cos, sin),
                            _rope_rows(kv[HEAD_DIM:2 * HEAD_DIM], cos, sin)], axis=0)
    kskw_ref[0] = keys.T.astype(BF16)
    vals = kv[2 * HEAD_DIM:4 * HEAD_DIM].astype(BF16)
    ones = jnp.ones((ONES_ROWS, LANES), BF16)
    for u in range(vals.shape[1] // LANES):
        blk = vals[:, u * LANES:(u + 1) * LANES]
        vt_ref[0, u] = jnp.concatenate([blk[0:HEAD_DIM], ones, blk[HEAD_DIM:2 * HEAD_DIM], ones], axis=0)
    ones2 = jnp.ones((ONES_ROWS, MXU_DEPTH), BF16)
    for u in range(vals.shape[1] // MXU_DEPTH):
        vs2_ref[0, u] = jnp.concatenate([vals[0:HEAD_DIM, u * MXU_DEPTH:(u + 1) * MXU_DEPTH], ones2], axis=0)


def _nsa_keys(pt, rope, tk):
    B, _, T = pt.shape
    return pl.pallas_call(
        _nsa_keys_kernel,
        grid=(B, T // tk),
        in_specs=[
            pl.BlockSpec((1, 4 * HEAD_DIM, tk), lambda b, i: (b, 0, i)),
            pl.BlockSpec((2 * ROPE_HALF, tk), lambda b, i: (0, i)),
        ],
        out_specs=[
            pl.BlockSpec((1, tk, LANES), lambda b, i: (b, i, 0)),
            pl.BlockSpec((1, tk // LANES, 2 * VAL_ROWS, LANES), lambda b, i: (b, i, 0, 0)),
            pl.BlockSpec((1, tk // MXU_DEPTH, VAL_ROWS, MXU_DEPTH), lambda b, i: (b, i, 0, 0)),
        ],
        out_shape=[
            jax.ShapeDtypeStruct((B, T, LANES), BF16),
            jax.ShapeDtypeStruct((B, T // LANES, 2 * VAL_ROWS, LANES), BF16),
            jax.ShapeDtypeStruct((B, T // MXU_DEPTH, VAL_ROWS, MXU_DEPTH), BF16),
        ],
        compiler_params=_params("parallel", "parallel"),
        name="nsa_keys",
    )(pt, rope)


def _nsa_compress_kernel(g_ref, wc_ref, pos_ref, w2_ref, w2t_ref, rm_ref, t_ref):
    ng = g_ref.shape[1]
    wc = wc_ref[...]
    m = _dot(g_ref[0].astype(BF16), wc)
    pm = _dot(pos_ref[...], wc)
    pos_term = pm[0:1, 0:LANES] + pm[1:2, LANES:2 * LANES]
    pre = m[:, 0:LANES] + pltpu.roll(m[:, LANES:2 * LANES], ng - 1, axis=0) + pos_term
    act = _silu(pre).astype(BF16)
    row = lax.broadcasted_iota(jnp.int32, (ng, 1), 0)
    col = lax.broadcasted_iota(jnp.int32, (1, ng), 1)
    rm_ref[0] = jnp.where(row < ng - 1, _dot(act, w2_ref[...]), 0.0).astype(BF16)
    vt = jnp.where(col < ng - 1, _dot_nt(w2t_ref[...], act)[HEAD_DIM:2 * HEAD_DIM], 0.0)
    t_ref[0] = jnp.concatenate([vt, jnp.ones((ONES_ROWS, ng), F32)], axis=0).astype(BF16)


def _nsa_compress(kcvc, cmp_pos, k_w1, k_w2, v_w1, v_w2):
    B, T, _ = kcvc.shape
    ng = T // CMP_STRIDE
    half = CMP_BLOCK // 2
    g = kcvc.reshape(B, ng, half * LANES)
    D = HEAD_DIM

    def spread(w, second, is_v):
        blk = w[second * half * D:(second + 1) * half * D].reshape(half, D, D)
        z = jnp.zeros_like(blk)
        return jnp.concatenate([z, blk] if is_v else [blk, z], axis=1).reshape(half * LANES, D)

    wc = jnp.concatenate([spread(k_w1, 0, False), spread(v_w1, 0, True),
                          spread(k_w1, 1, False), spread(v_w1, 1, True)], axis=1).astype(BF16)
    pos2 = jnp.concatenate([cmp_pos, cmp_pos], axis=1)
    pos = jnp.zeros((8, half * LANES), F32)
    pos = pos.at[0].set(pos2[:half].reshape(-1)).at[1].set(pos2[half:].reshape(-1)).astype(BF16)
    z = jnp.zeros((D, D), F32)
    w2 = jnp.block([[k_w2, z], [z, v_w2]])
    const = lambda shape: pl.BlockSpec(shape, lambda b: (0,) * len(shape))
    return pl.pallas_call(
        _nsa_compress_kernel,
        grid=(B,),
        in_specs=[
            pl.BlockSpec((1, ng, half * LANES), lambda b: (b, 0, 0)),
            const((half * LANES, 2 * LANES)), const((8, half * LANES)),
            const((LANES, LANES)), const((LANES, LANES)),
        ],
        out_specs=[
            pl.BlockSpec((1, ng, LANES), lambda b: (b, 0, 0)),
            pl.BlockSpec((1, VAL_ROWS, ng), lambda b: (b, 0, 0)),
        ],
        out_shape=[
            jax.ShapeDtypeStruct((B, ng, LANES), BF16),
            jax.ShapeDtypeStruct((B, VAL_ROWS, ng), BF16),
        ],
        compiler_params=_params("parallel"),
        name="nsa_compress",
    )(g, wc, pos, w2.astype(BF16), w2.T.astype(BF16))


def _tile_heads(x):
    return jnp.concatenate([x] * NSA_HEADS, axis=1)


def _nsa_attn_kernel(q_ref, gate_ref, glog_ref, rope_ref, rm_ref, ct_ref, kskw_ref, vt_ref, vs2_ref,
                     ov_ref, oh_ref, gb_ref, og_ref, y_ref, m_ref, acc_ref, sa_ref, sb_ref, ma_ref,
                     mb_ref):
    D, Q, Hn = HEAD_DIM, Q_BLOCK, NSA_HEADS
    KC = SEL_KEY_CHUNK
    ng = rm_ref.shape[1]
    ns = ov_ref.shape[0]
    n_top = min(SEL_TOPK, ns)
    i = pl.program_id(1)
    t0 = i * Q
    tq = t0 + lax.broadcasted_iota(jnp.int32, (1, Q), 1)

    q = q_ref[0] * (D ** -0.5 * LOG2E)
    cos, sin = rope_ref[0:ROPE_HALF], rope_ref[ROPE_HALF:2 * ROPE_HALF]
    qh = [q[h * D:(h + 1) * D] for h in range(Hn)]
    q4 = jnp.concatenate(qh, axis=1)
    q4r = jnp.concatenate([_rope_rows(x, cos, sin) for x in qh], axis=1)
    zero = jnp.zeros_like(q4)
    q_lo = jnp.concatenate([q4, zero], axis=0).astype(BF16)
    qr_lo = jnp.concatenate([q4r, zero], axis=0).astype(BF16)
    qr_hi = jnp.concatenate([zero, q4r], axis=0).astype(BF16)

    def masked(s, bias):
        return jnp.concatenate([s[:, h * Q:(h + 1) * Q] + bias for h in range(Hn)], axis=1)


    cend = lax.broadcasted_iota(jnp.int32, (ng, 1), 0) * CMP_STRIDE + (CMP_BLOCK - 1)
    s = masked(_dot(rm_ref[0], q_lo), jnp.where(cend <= tq, 0.0, NEG_INF))
    eb = jnp.exp2(s - jnp.max(s, axis=0, keepdims=True)).astype(BF16)
    ol = _dot(ct_ref[0], eb)
    seen = _tile_heads((tq >= CMP_BLOCK - 1).astype(F32))
    inv = seen / jnp.maximum(ol[D:D + 1], 1e-30)
    o_c = ol[0:D] * inv
    psum = sum(eb[:, h * Q:(h + 1) * Q] * inv[:, h * Q:(h + 1) * Q] for h in range(Hn))
    imp = _dot(ov_ref[...], psum.astype(BF16))

    WK = WINDOW + Q
    w0 = pl.multiple_of(jnp.maximum(t0 - WINDOW, 0), LANES)
    diff = tq - (w0 + lax.broadcasted_iota(jnp.int32, (WK, 1), 0))
    w_bias = jnp.where((diff >= 0) & (diff < WINDOW), 0.0, NEG_INF)

    def window_head(h):
        s = _dot(kskw_ref[0, pl.ds(w0, WK), :], qr_hi[:, h * Q:(h + 1) * Q]) + w_bias
        eb = jnp.exp2(s - jnp.max(s, axis=0, keepdims=True)).astype(BF16)
        ol = _dot(vt_ref[0, w0 // LANES][VAL_ROWS:2 * VAL_ROWS], eb[0:LANES])
        for u in range(1, WK // LANES):
            ol = ol + _dot(vt_ref[0, w0 // LANES + u][VAL_ROWS:2 * VAL_ROWS], eb[u * LANES:(u + 1) * LANES])
        return ol[0:D] / jnp.maximum(ol[D:D + 1], 1e-30)

    blk = lax.broadcasted_iota(jnp.int32, (ns, Q), 0)
    cur = tq // SEL_BLOCK
    forced = (blk == 0) | (blk == cur) | (blk == cur - 1)
    taken = -2.0
    score = jnp.where(forced, taken, jnp.where(blk <= cur, imp, -1.0))
    picks = max(n_top - 3, 0)
    o_w = []
    for it in range(picks):
        best = jnp.max(score, axis=0, keepdims=True)
        first = jnp.min(jnp.where(score == best, blk, ns), axis=0, keepdims=True)
        score = jnp.where(blk == first, taken, score)
        if it % 3 == 0 and len(o_w) < Hn:
            o_w.append(window_head(len(o_w)))
    while len(o_w) < Hn:
        o_w.append(window_head(len(o_w)))
    o_w = jnp.concatenate(o_w, axis=1)

    def with_block_mask(keep):
        bias = _tile_heads(jnp.where(keep, 0.0, NEG_INF))
        pad = oh_ref.shape[1] - ns
        if pad:
            bias = jnp.concatenate([bias, jnp.zeros((pad, Hn * Q), F32)], axis=0)
        return jnp.concatenate([qr_lo, bias.astype(BF16)], axis=0)

    chosen = (score == taken) & (blk <= cur)
    lk = lax.broadcasted_iota(jnp.int32, (Q, 1), 0)
    lq = lax.broadcasted_iota(jnp.int32, (1, Q), 1)
    own = pl.ds(pl.multiple_of(t0, Q), Q)
    s = masked(_dot(jnp.concatenate([kskw_ref[0, own, :], oh_ref[own, :]], axis=1), with_block_mask(chosen)),
               jnp.where(lk <= lq, 0.0, NEG_INF))
    m0 = jnp.max(s, axis=0, keepdims=True)
    m_ref[...] = m0
    acc_ref[...] = _dot(vs2_ref[0, i], jnp.exp2(s - m0).astype(BF16))

    q_sel = with_block_mask(chosen & (blk < t0 // SEL_BLOCK))

    def sel_scores(j, s_ref, cmax_ref):
        k0 = pl.multiple_of(j * KC, KC)
        keys = jnp.concatenate([kskw_ref[0, pl.ds(k0, KC), :], oh_ref[pl.ds(k0, KC), :]], axis=1)
        s = _dot(keys, q_sel)
        s_ref[...] = s
        cmax_ref[...] = jnp.max(s, axis=0, keepdims=True)

    UB = KC // MXU_DEPTH

    def sel_update(j, s_ref, cmax_ref):
        m_old = m_ref[...]
        m_new = jnp.maximum(m_old, cmax_ref[...])
        eb = jnp.exp2(s_ref[...] - m_new).astype(BF16)
        pv = acc_ref[...] * jnp.exp2(m_old - m_new)
        for u in range(UB):
            pv = pv + _dot(vs2_ref[0, j * UB + u], eb[u * MXU_DEPTH:(u + 1) * MXU_DEPTH])
        acc_ref[...] = pv
        m_ref[...] = m_new

    n_chunks = (t0 + KC - 1) // KC
    last = kskw_ref.shape[1] // KC - 1
    sel_scores(0, sa_ref, ma_ref)

    def chunk_pair(j, carry):
        sel_scores(jnp.minimum(2 * j + 1, last), sb_ref, mb_ref)
        sel_update(2 * j, sa_ref, ma_ref)
        sel_scores(jnp.minimum(2 * j + 2, last), sa_ref, ma_ref)
        sel_update(jnp.minimum(2 * j + 1, last), sb_ref, mb_ref)
        return carry

    lax.fori_loop(0, (n_chunks + 1) // 2, chunk_pair, 0)
    acc = acc_ref[...]
    o_s = acc[0:D] / jnp.maximum(acc[D:D + 1], 1e-30)

    gl = _sigmoid(glog_ref[0] + gb_ref[...])
    ys = []
    for h in range(Hn):
        cs = slice(h * Q, (h + 1) * Q)
        o = (gl[3 * h:3 * h + 1] * o_c[:, cs] + gl[3 * h + 1:3 * h + 2] * o_s[:, cs]
             + gl[3 * h + 2:3 * h + 3] * o_w[:, cs])
        o = o * lax.rsqrt(jnp.mean(o * o, axis=0, keepdims=True) + NORM_EPS)
        ys.append(o)
    y = jnp.concatenate(ys, axis=0) * og_ref[...] * _silu(gate_ref[0])
    y_ref[0] = y.T.astype(y_ref.dtype)


def _nsa_attention(pt, rope, cmp_rm, cmp_t, kskw, vt, vs2, gate_b, out_g):
    B, _, T = pt.shape
    ng = T // CMP_STRIDE
    ns = T // SEL_BLOCK
    Q, W = Q_BLOCK, NSA_WIDTH
    QH = Q * NSA_HEADS
    c0 = np.arange(ng)[None, :] * CMP_STRIDE
    s0 = np.arange(ns)[:, None] * SEL_BLOCK
    ov = np.clip(np.minimum(c0 + CMP_BLOCK, s0 + SEL_BLOCK) - np.maximum(c0, s0), 0, None) / CMP_BLOCK
    ov[:, ng - 1] = 0.0
    oh_lanes = -(-ns // LANES) * LANES
    onehot = (np.arange(T)[:, None] // SEL_BLOCK) == np.arange(oh_lanes)[None, :]
    gb = jnp.zeros((16, 1), F32).at[:NSA_HEADS * N_BRANCH, 0].set(gate_b)
    glog_blk = (4 * HEAD_DIM + 4 * W) // 16
    return pl.pallas_call(
        _nsa_attn_kernel,
        grid=(B, T // Q),
        in_specs=[
            pl.BlockSpec((1, W, Q), lambda b, i: (b, 1, i)),
            pl.BlockSpec((1, W, Q), lambda b, i: (b, 2, i)),
            pl.BlockSpec((1, 16, Q), lambda b, i: (b, glog_blk, i)),
            pl.BlockSpec((2 * ROPE_HALF, Q), lambda b, i: (0, i)),
            pl.BlockSpec((1, ng, LANES), lambda b, i: (b, 0, 0)),
            pl.BlockSpec((1, VAL_ROWS, ng), lambda b, i: (b, 0, 0)),
            pl.BlockSpec((1, T, LANES), lambda b, i: (b, 0, 0)),
            pl.BlockSpec((1, T // LANES, 2 * VAL_ROWS, LANES), lambda b, i: (b, 0, 0, 0)),
            pl.BlockSpec((1, T // MXU_DEPTH, VAL_ROWS, MXU_DEPTH), lambda b, i: (b, 0, 0, 0)),
            pl.BlockSpec((ns, ng), lambda b, i: (0, 0)),
            pl.BlockSpec((T, oh_lanes), lambda b, i: (0, 0)),
            pl.BlockSpec((16, 1), lambda b, i: (0, 0)),
            pl.BlockSpec((W, 1), lambda b, i: (0, 0)),
        ],
        out_specs=pl.BlockSpec((1, Q, W), lambda b, i: (b, i, 0)),
        out_shape=jax.ShapeDtypeStruct((B, T, W), BF16),
        scratch_shapes=[pltpu.VMEM((1, QH), F32), pltpu.VMEM((VAL_ROWS, QH), F32),
                        pltpu.VMEM((SEL_KEY_CHUNK, QH), F32), pltpu.VMEM((SEL_KEY_CHUNK, QH), F32),
                        pltpu.VMEM((1, QH), F32), pltpu.VMEM((1, QH), F32)],
        compiler_params=_params("parallel", "arbitrary"),
        name="nsa_attention",
    )(pt, pt, pt, rope, cmp_rm, cmp_t, kskw, vt, vs2, jnp.asarray(ov, BF16), jnp.asarray(onehot, BF16), gb,
      out_g.reshape(W, 1))


def _mem_kv_kernel(mem_ref, g_ref, w_ref, wt_ref, k_ref, vt_ref):
    x = mem_ref[0]
    h = x * lax.rsqrt(jnp.mean(x * x, axis=-1, keepdims=True) + NORM_EPS) * g_ref[...]
    hb = h.astype(BF16)
    k_ref[0] = _dot(hb, w_ref[...]).astype(BF16)
    vt = _dot_nt(wt_ref[...], hb)
    ones = jnp.ones((ONES_ROWS, vt.shape[1]), F32)
    vt_ref[0] = jnp.concatenate(
        [part for h in range(MEM_HEADS) for part in (vt[h * HEAD_DIM:(h + 1) * HEAD_DIM], ones)],
        axis=0).astype(BF16)


def _mem_kv(mem, g, w_kv):
    B, M, D = mem.shape
    W = MEM_WIDTH
    const = lambda shape: pl.BlockSpec(shape, lambda b: (0,) * len(shape))
    return pl.pallas_call(
        _mem_kv_kernel,
        grid=(B,),
        in_specs=[pl.BlockSpec((1, M, D), lambda b: (b, 0, 0)), const((1, D)), const((D, W)),
                  const((W, D))],
        out_specs=[pl.BlockSpec((1, M, W), lambda b: (b, 0, 0)),
                   pl.BlockSpec((1, MEM_HEADS * VAL_ROWS, M), lambda b: (b, 0, 0))],
        out_shape=[jax.ShapeDtypeStruct((B, M, W), BF16),
                   jax.ShapeDtypeStruct((B, MEM_HEADS * VAL_ROWS, M), BF16)],
        compiler_params=_params("parallel"),
        name="mem_kv",
    )(mem, g.reshape(1, D), w_kv[:, :W].astype(BF16), w_kv[:, W:].T.astype(BF16))


def _mem_attn_kernel(q_ref, gate_ref, k_ref, vt_ref, og_ref, y_ref):
    D, Hm = HEAD_DIM, MEM_HEADS
    q = q_ref[0] * (D ** -0.5 * LOG2E)
    k = k_ref[0]
    vt = vt_ref[0]
    row = lax.broadcasted_iota(jnp.int32, (Hm * D, 1), 0)
    ys = []
    for h in range(Hm):
        qh = jnp.where(row // D == h, q, 0.0).astype(BF16)
        s = _dot(k, qh)
        eb = jnp.exp2(s - jnp.max(s, axis=0, keepdims=True)).astype(BF16)
        ol = _dot(vt[h * VAL_ROWS:(h + 1) * VAL_ROWS], eb)
        o = ol[0:D] / ol[D:D + 1]
        ys.append(o * lax.rsqrt(jnp.mean(o * o, axis=0, keepdims=True) + NORM_EPS))
    y = jnp.concatenate(ys, axis=0) * og_ref[...] * _silu(gate_ref[0])
    y_ref[0] = y.T.astype(y_ref.dtype)


def _mem_attention(pt, mem_k, mem_vt, out_g, tm):
    B, _, T = pt.shape
    M = mem_k.shape[1]
    W = MEM_WIDTH
    return pl.pallas_call(
        _mem_attn_kernel,
        grid=(B, T // tm),
        in_specs=[
            pl.BlockSpec((1, W, tm), lambda b, i: (b, 3, i)),
            pl.BlockSpec((1, W, tm), lambda b, i: (b, 4, i)),
            pl.BlockSpec((1, M, W), lambda b, i: (b, 0, 0)),
            pl.BlockSpec((1, MEM_HEADS * VAL_ROWS, M), lambda b, i: (b, 0, 0)),
            pl.BlockSpec((W, 1), lambda b, i: (0, 0)),
        ],
        out_specs=pl.BlockSpec((1, tm, W), lambda b, i: (b, i, 0)),
        out_shape=jax.ShapeDtypeStruct((B, T, W), BF16),
        compiler_params=_params("parallel", "parallel"),
        name="mem_attention",
    )(pt, pt, mem_k, mem_vt, out_g.reshape(W, 1))


def _outproj_kernel(x_ref, yr_ref, yn_ref, ym_ref, wr_ref, wn_ref, wm_ref, g_ref, o_ref):
    z = (x_ref[0] + _dot(yr_ref[0], wr_ref[...]) + _dot(yn_ref[0], wn_ref[...])
         + _dot(ym_ref[0], wm_ref[...]))
    o_ref[0] = z * lax.rsqrt(jnp.mean(z * z, axis=-1, keepdims=True) + NORM_EPS) * g_ref[...]


def _output_projection(x, y_rwkv, y_nsa, y_mem, w_out, g, tm):
    B, T, D = x.shape
    wb = w_out.astype(BF16)
    w_r, w_n, w_m = wb[:RWKV_WIDTH], wb[RWKV_WIDTH:RWKV_WIDTH + NSA_WIDTH], wb[RWKV_WIDTH + NSA_WIDTH:]
    tile = lambda w: pl.BlockSpec((1, tm, w), lambda b, i: (b, i, 0))
    const = lambda shape: pl.BlockSpec(shape, lambda b, i: (0,) * len(shape))
    return pl.pallas_call(
        _outproj_kernel,
        grid=(B, T // tm),
        in_specs=[tile(D), tile(RWKV_WIDTH), tile(NSA_WIDTH), tile(MEM_WIDTH),
                  const((RWKV_WIDTH, D)), const((NSA_WIDTH, D)), const((MEM_WIDTH, D)), const((1, D))],
        out_specs=tile(D),
        out_shape=jax.ShapeDtypeStruct((B, T, D), F32),
        compiler_params=_params("parallel", "parallel"),
        name="output_projection",
    )(x, y_rwkv, y_nsa, y_mem, w_r, w_n, w_m, g.reshape(1, D))


def _rope_table(T):
    inv_freq = ROPE_THETA ** (-jnp.arange(ROPE_HALF, dtype=F32) / ROPE_HALF)
    ang = inv_freq[:, None] * jnp.arange(T).astype(F32)[None, :]
    return jnp.concatenate([jnp.cos(ang), jnp.sin(ang)], axis=0)


def _split_w_in(w):
    D = HEAD_DIM
    n0 = RWKV_COLS
    q, gate, glog = n0, n0 + NSA_WIDTH, n0 + 2 * NSA_WIDTH
    kc = glog + NSA_HEADS * N_BRANCH
    vc, ks, vs, kw, vw = kc + D, kc + 2 * D, kc + 3 * D, kc + 4 * D, kc + 5 * D
    m0 = vw + D
    cols = lambda a, n: w[:, a:a + n]
    w_row = jnp.concatenate([cols(0, n0), cols(kc, D), cols(vc, D)], axis=1)
    w_t = jnp.concatenate([cols(ks, D), cols(kw, D), cols(vs, D), cols(vw, D), cols(q, NSA_WIDTH),
                           cols(gate, NSA_WIDTH), cols(m0, MEM_WIDTH), cols(m0 + MEM_WIDTH, MEM_WIDTH),
                           cols(glog, NSA_HEADS * N_BRANCH),
                           jnp.zeros((w.shape[0], 16 - NSA_HEADS * N_BRANCH), w.dtype)], axis=1)
    return w_row.astype(BF16), w_t.T.astype(BF16)


def kernel(x, mem, norm_in_g, w_in, rwkv_mu, rwkv_w0, rwkv_w_up, rwkv_a0, rwkv_a_up, rwkv_k_k,
           rwkv_k_a, rwkv_r_k, rwkv_ln_w, rwkv_ln_b, nsa_cmp_pos, nsa_cmp_k_w1, nsa_cmp_k_w2,
           nsa_cmp_v_w1, nsa_cmp_v_w2, nsa_gate_b, nsa_out_g, mem_norm_g, w_mem_kv, mem_out_g, w_out,
           norm_final_g):
    B, T, D = x.shape
    assert w_in.shape[0] == 1, "single-layer stack: the final norm is fused into the output projection"
    rope = _rope_table(T)
    w_row, w_t = _split_w_in(w_in[0])
    p_row, kcvc, pt = _input_projection(x, norm_in_g[0].reshape(1, D), w_row, w_t, tm=256)
    y_rwkv = _rwkv_group(p_row, rwkv_mu[0], rwkv_w0[0], rwkv_w_up[0], rwkv_a0[0], rwkv_a_up[0],
                         rwkv_k_k[0], rwkv_k_a[0], rwkv_r_k[0].reshape(-1), rwkv_ln_w[0],
                         rwkv_ln_b[0])
    kskw, vt, vs2 = _nsa_keys(pt, rope, tk=512)
    cmp_rm, cmp_t = _nsa_compress(kcvc, nsa_cmp_pos[0], nsa_cmp_k_w1[0], nsa_cmp_k_w2[0],
                                  nsa_cmp_v_w1[0], nsa_cmp_v_w2[0])
    y_nsa = _nsa_attention(pt, rope, cmp_rm, cmp_t, kskw, vt, vs2, nsa_gate_b[0], nsa_out_g[0])
    mem_k, mem_vt = _mem_kv(mem, mem_norm_g[0], w_mem_kv[0])
    y_mem = _mem_attention(pt, mem_k, mem_vt, mem_out_g[0], tm=512)
    return _output_projection(x, y_rwkv, y_nsa, y_mem, w_out[0], norm_final_g, tm=512)
```

```python
import functools

import numpy as np
import jax
import jax.numpy as jnp
from jax import lax
from jax.experimental import pallas as pl
from jax.experimental.pallas import tpu as pltpu

F32 = jnp.float32
BF16 = jnp.bfloat16

HEAD_DIM = 64
RWKV_HEADS = 8
RWKV_WIDTH = RWKV_HEADS * HEAD_DIM
LORA = 64
RWKV_COLS = 4 * RWKV_WIDTH + 2 * LORA
RWKV_GN_EPS = 64e-5
NSA_HEADS = 4
NSA_WIDTH = NSA_HEADS * HEAD_DIM
N_BRANCH = 3
CMP_BLOCK = 32
CMP_STRIDE = 16
SEL_BLOCK = 64
SEL_TOPK = 16
WINDOW = 512
MEM_HEADS = 4
MEM_WIDTH = MEM_HEADS * HEAD_DIM
ROPE_THETA = 500000.0
ROPE_HALF = 8
Q_BLOCK = 256
NORM_EPS = 1e-6
NEG_INF = -1e30
FORCE_SCORE = 1e4
LOG2E = 1.4426950408889634
DECAY_SCALE = 0.6065306597126334

RWKV_CHUNK = 64
RWKV_HEAD_GROUP = 4
RWKV_BATCH_TILE = 4
SEL_KEY_CHUNK = 512
LANES = 128
MXU_DEPTH = 256
ONES_ROWS = 16
VAL_ROWS = HEAD_DIM + ONES_ROWS
VMEM_LIMIT = 48 * 1024 * 1024

def _dot(a, b):
    return jnp.dot(a, b, preferred_element_type=F32)


def _dot_nt(a, b):
    return lax.dot_general(a, b, (((1,), (1,)), ((), ())), preferred_element_type=F32)


def _dot_tn(a, b):
    return lax.dot_general(a, b, (((0,), (0,)), ((), ())), preferred_element_type=F32)


def _bf(a):
    return a.astype(BF16)


def _split_bf16(a, pieces):
    out = []
    for _ in range(pieces):
        part = a.astype(BF16)
        out.append(part)
        a = a - part.astype(F32)
    return out


def _sigmoid(x):
    return 1.0 / (1.0 + jnp.exp(-x))


def _silu(x):
    return x * _sigmoid(x)


def _params(*sem):
    return pltpu.CompilerParams(dimension_semantics=sem, vmem_limit_bytes=VMEM_LIMIT)


def _inproj_kernel(x_ref, g_ref, wrow_ref, wt_ref, prow_ref, kc_ref, pt_ref):
    x = x_ref[0]
    h = x * lax.rsqrt(jnp.mean(x * x, axis=-1, keepdims=True) + NORM_EPS) * g_ref[...]
    hb = h.astype(BF16)
    row = _dot(hb, wrow_ref[...])
    prow_ref[0] = row[:, :RWKV_COLS]
    kc_ref[0] = row[:, RWKV_COLS:]
    pt_ref[0] = _dot_nt(wt_ref[...], hb)


def _input_projection(x, g, w_row, w_t, tm):
    B, T, D = x.shape
    n_row = w_row.shape[1]
    n_t = w_t.shape[0]
    return pl.pallas_call(
        _inproj_kernel,
        grid=(B, T // tm),
        in_specs=[
            pl.BlockSpec((1, tm, D), lambda b, i: (b, i, 0)),
            pl.BlockSpec((1, D), lambda b, i: (0, 0)),
            pl.BlockSpec((D, n_row), lambda b, i: (0, 0)),
            pl.BlockSpec((n_t, D), lambda b, i: (0, 0)),
        ],
        out_specs=[
            pl.BlockSpec((1, tm, RWKV_COLS), lambda b, i: (b, i, 0)),
            pl.BlockSpec((1, tm, n_row - RWKV_COLS), lambda b, i: (b, i, 0)),
            pl.BlockSpec((1, n_t, tm), lambda b, i: (b, 0, i)),
        ],
        out_shape=[
            jax.ShapeDtypeStruct((B, T, RWKV_COLS), F32),
            jax.ShapeDtypeStruct((B, T, n_row - RWKV_COLS), F32),
            jax.ShapeDtypeStruct((B, n_t, T), F32),
        ],
        compiler_params=_params("parallel", "parallel"),
        name="input_projection",
    )(x, g, w_row, w_t)


_X_R, _X_A, _X_K, _X_B, _X_V, _X_BV, _X_SG, _X_G = range(8)


def _rwkv_kernel(podd_ref, peven_ref, prevodd_ref, preveven_ref, mu_ref, w0_ref, wup_ref, a0_ref,
                 aup_ref, kk_ref, ka_ref, rk_ref, lnw_ref, lnb_ref, bd_ref, tri_ref, y_ref,
                 s_ref, xa_ref, xb_ref):
    C, W, N, H = RWKV_CHUNK, RWKV_WIDTH, HEAD_DIM, RWKV_HEADS
    HG = RWKV_HEAD_GROUP
    GW = HG * N
    n = pl.program_id(1)
    n_chunks = 2 * (pl.num_programs(1) - 1)
    bt = podd_ref.shape[0]

    @pl.when(n == 0)
    def _():
        s_ref[...] = jnp.zeros_like(s_ref)
        xa_ref[...] = jnp.zeros_like(xa_ref)
        xb_ref[...] = jnp.zeros_like(xb_ref)

    bd = bd_ref[...]
    head_sum = lambda a: jnp.concatenate(
        [_dot(_bf(a[:, g * GW:(g + 1) * GW]), bd) for g in range(H // HG)], axis=1)
    row = lax.broadcasted_iota(jnp.int32, (C, 1), 0)

    def prepare(b, chunk, p_ref, prev_ref, x_ref):
        p = p_ref[b]
        prev_last = jnp.where(chunk <= 0, 0.0, prev_ref[b][7:8, :])
        prev = jnp.where(row == 0, prev_last, pltpu.roll(p, 1, axis=0))
        pf = p + mu_ref[...] * (prev - p)
        r = pf[:, 0:W]
        k = pf[:, W:2 * W]
        v = pf[:, 2 * W:3 * W]
        gate = pf[:, 3 * W:4 * W]
        wd = _bf(jnp.tanh(pf[:, 4 * W:4 * W + LORA]))
        ad = _bf(pf[:, 4 * W + LORA:4 * W + 2 * LORA])
        yield
        z = w0_ref[...] + _dot(wd, wup_ref[...])
        logw = -DECAY_SCALE * _sigmoid(z)
        eta = _sigmoid(a0_ref[...] + _dot(ad, aup_ref[...]))
        kk = k * kk_ref[...]
        k2 = k * (1.0 + (eta - 1.0) * ka_ref[...])
        kk_sq = _bf(kk * kk)
        rk2 = _bf(r * k2 * rk_ref[...])
        logw_parts = _split_bf16(logw, 3)
        yield
        kk = kk * lax.rsqrt(jnp.maximum(head_sum(kk_sq), 1e-24))
        x_ref[b, _X_BV] = head_sum(rk2) * v
        x_ref[b, _X_SG] = _silu(gate)
        tri = tri_ref[...]
        cs = sum(_dot(tri, part) for part in logw_parts)
        mid = cs[C // 2 - 1:C // 2, :]
        csm = cs - mid
        end = cs[C - 1:C, :]
        x_ref[b, _X_G] = jnp.concatenate([jnp.exp(mid), jnp.exp(end), jnp.exp(end - mid),
                                          jnp.zeros((C - 3, W), F32)], axis=0)
        yield
        e_out = jnp.exp(-csm)
        x_ref[b, _X_R] = r * jnp.exp(csm)
        x_ref[b, _X_A] = -kk * jnp.exp(csm - logw)
        yield
        x_ref[b, _X_K] = k2 * e_out
        x_ref[b, _X_B] = kk * eta * e_out
        x_ref[b, _X_V] = v

    li = lax.broadcasted_iota(jnp.int32, (C, GW), 1) % N
    ti = lax.broadcasted_iota(jnp.int32, (C, GW), 0)
    strict, incl = ti > li, ti >= li
    eye = (ti == li).astype(F32)
    same_head = (lax.broadcasted_iota(jnp.int32, (GW, GW), 0) // N
                 == lax.broadcasted_iota(jnp.int32, (GW, GW), 1) // N)

    def blockdiag(a):
        a = _bf(a)
        return jnp.where(same_head, jnp.concatenate([a] * HG, axis=0), jnp.zeros((), BF16))

    def advance(chunk, x_ref, y_rows, fill):
        live = (chunk >= 0) & (chunk < n_chunks)
        chains = [(b, g) for b in range(bt) for g in range(H // HG)]
        st = {}
        for c in chains:
            b, g = c
            ls = slice(g * GW, (g + 1) * GW)
            kb = jnp.concatenate([blockdiag(x_ref[b, _X_K][:, ls]), blockdiag(x_ref[b, _X_B][:, ls])],
                                 axis=0)
            ar = _bf(jnp.concatenate([x_ref[b, _X_A][:, ls], x_ref[b, _X_R][:, ls]], axis=0))
            gc = _dot_nt(ar, kb)
            s0 = s_ref[b, g]
            gates = x_ref[b, _X_G]
            st[c] = dict(
                ls=ls, kb=kb, s0=s0, v_bd=blockdiag(x_ref[b, _X_V][:, ls]),
                g_end=gates[1:2, ls], g_end_mid=gates[2:3, ls],
                a_ak=jnp.where(strict, gc[0:C, 0:GW], 0.0), a_ab=jnp.where(strict, gc[0:C, GW:2 * GW], 0.0),
                a_rk=jnp.where(incl, gc[C:2 * C, 0:GW], 0.0), a_rb=jnp.where(incl, gc[C:2 * C, GW:2 * GW], 0.0),
                uy0=_dot_nt(ar, _bf(s0 * gates[0:1, ls])))
            fill()
        for c in chains:
            s = st[c]
            s["rhs"] = s["uy0"][0:C] + _dot(_bf(s["a_ak"]), s["v_bd"])
            s["tm"] = eye + s["a_ab"]
            s["pw"] = _dot(_bf(s["a_ab"]), blockdiag(s["a_ab"]))
            fill()
        span = 2
        while span < C:
            for c in chains:
                s = st[c]
                if 2 * span < C:
                    tp = _dot(_bf(jnp.concatenate([s["tm"], s["pw"]], axis=0)), blockdiag(s["pw"]))
                    s["tm"] = s["tm"] + tp[0:C]
                    s["pw"] = tp[C:2 * C]
                else:
                    s["tm"] = s["tm"] + _dot(_bf(s["tm"]), blockdiag(s["pw"]))
                fill()
            span *= 2
        for c in chains:
            s = st[c]
            s["u"] = _dot(_bf(s["tm"]), blockdiag(s["rhs"]))
            fill()
        for c in chains:
            s = st[c]
            s["vu"] = jnp.concatenate([s["v_bd"], blockdiag(s["u"])], axis=0)
            s["y"] = s["uy0"][C:2 * C] + _dot(_bf(jnp.concatenate([s["a_rk"], s["a_rb"]], axis=1)), s["vu"])
            fill()
        for c in chains:
            b, g = c
            s = st[c]
            ls = s["ls"]
            vu_rows = jnp.concatenate([_bf(x_ref[b, _X_V][:, ls]), _bf(s["u"])], axis=0)
            kb_rows = _bf(jnp.concatenate([x_ref[b, _X_K][:, ls], x_ref[b, _X_B][:, ls]], axis=0))
            keep = jnp.where(live, s["g_end"], 1.0)
            gain = jnp.where(live, s["g_end_mid"], 0.0)
            s_ref[b, g] = s["s0"] * keep + jnp.where(same_head, _dot_tn(vu_rows, kb_rows) * gain, 0.0)
            fill()
        for b in range(bt):
            y = jnp.concatenate([st[(b, g)]["y"] for g in range(H // HG)], axis=1)
            mean = head_sum(y) * (1.0 / N)
            var = head_sum(jnp.square(y - mean)) * (1.0 / N)
            y = (y - mean) * lax.rsqrt(var + RWKV_GN_EPS) * lnw_ref[...] + lnb_ref[...] + x_ref[b, _X_BV]
            y_ref[b, y_rows] = (y * x_ref[b, _X_SG]).astype(y_ref.dtype)
            fill()

    def filler(gens):
        gens = list(gens)

        def fill():
            while gens:
                try:
                    next(gens[0])
                    return
                except StopIteration:
                    gens.pop(0)

        def drain():
            while gens:
                fill()

        return fill, drain

    fill, drain = filler(prepare(b, 2 * n - 1, podd_ref, prevodd_ref, xb_ref) for b in range(bt))
    advance(2 * n - 2, xa_ref, slice(0, C), fill)
    drain()
    fill, drain = filler(prepare(b, 2 * n, peven_ref, preveven_ref, xa_ref) for b in range(bt))
    advance(2 * n - 1, xb_ref, slice(C, 2 * C), fill)
    drain()


def _rwkv_group(p_row, mu, w0, w_up, a0, a_up, k_k, k_a, r_k, ln_w, ln_b):
    B, T, _ = p_row.shape
    C, W = RWKV_CHUNK, RWKV_WIDTH
    heads = np.arange(RWKV_HEAD_GROUP * HEAD_DIM) // HEAD_DIM
    bd = jnp.asarray(heads[:, None] == heads[None, :], BF16)
    tri = jnp.asarray(np.tril(np.ones((C, C), np.float32)), BF16)
    vec = lambda a: a.reshape(1, -1)
    const = lambda shape: pl.BlockSpec(shape, lambda b, n: (0,) * len(shape))
    bt = RWKV_BATCH_TILE if B % RWKV_BATCH_TILE == 0 else 1
    n_chunks = T // C
    assert n_chunks % 2 == 0
    odd = lambda n: jnp.maximum(2 * n - 1, 0)
    even = lambda n: jnp.minimum(2 * n, n_chunks - 1)
    before = lambda c: jnp.maximum(c * (C // 8) - 1, 0)
    return pl.pallas_call(
        _rwkv_kernel,
        grid=(B // bt, n_chunks // 2 + 1),
        in_specs=[
            pl.BlockSpec((bt, C, RWKV_COLS), lambda b, n: (b, odd(n), 0)),
            pl.BlockSpec((bt, C, RWKV_COLS), lambda b, n: (b, even(n), 0)),
            pl.BlockSpec((bt, 8, RWKV_COLS), lambda b, n: (b, before(odd(n)), 0)),
            pl.BlockSpec((bt, 8, RWKV_COLS), lambda b, n: (b, before(even(n)), 0)),
            const((1, RWKV_COLS)), const((1, W)), const((LORA, W)), const((1, W)), const((LORA, W)),
            const((1, W)), const((1, W)), const((1, W)), const((1, W)), const((1, W)),
            const(bd.shape), const((C, C)),
        ],
        out_specs=pl.BlockSpec((bt, 2 * C, W), lambda b, n: (b, jnp.maximum(n - 1, 0), 0)),
        out_shape=jax.ShapeDtypeStruct((B, T, W), BF16),
        scratch_shapes=[pltpu.VMEM((bt, RWKV_HEADS // RWKV_HEAD_GROUP) + (RWKV_HEAD_GROUP * HEAD_DIM,) * 2,
                                   F32),
                        pltpu.VMEM((bt, 8, C, W), F32), pltpu.VMEM((bt, 8, C, W), F32)],
        compiler_params=_params("parallel", "arbitrary"),
        name="rwkv7_group",
    )(p_row, p_row, p_row, p_row, vec(mu), vec(w0), _bf(w_up), vec(a0), _bf(a_up), vec(k_k), vec(k_a),
      vec(r_k), vec(ln_w), vec(ln_b), bd, tri)


def _rope_rows(x, cos, sin):
    x1, x2 = x[0:ROPE_HALF], x[ROPE_HALF:2 * ROPE_HALF]
    return jnp.concatenate([x1 * cos - x2 * sin, x2 * cos + x1 * sin, x[2 * ROPE_HALF:]], axis=0)


def _nsa_keys_kernel(kv_ref, rope_ref, kskw_ref, vt_ref, vs2_ref):
    kv = kv_ref[0]
    cos, sin = rope_ref[0:ROPE_HALF], rope_ref[ROPE_HALF:2 * ROPE_HALF]
    keys = jnp.concatenate([_rope_rows(kv[0:HEAD_DIM], cos, sin),
                            _rope_rows(kv[HEAD_DIM:2 * HEAD_DIM], cos, sin)], axis=0)
    kskw_ref[0] = keys.T.astype(BF16)
    vals = kv[2 * HEAD_DIM:4 * HEAD_DIM].astype(BF16)
    ones = jnp.ones((ONES_ROWS, LANES), BF16)
    for u in range(vals.shape[1] // LANES):
        blk = vals[:, u * LANES:(u + 1) * LANES]
        vt_ref[0, u] = jnp.concatenate([blk[0:HEAD_DIM], ones, blk[HEAD_DIM:2 * HEAD_DIM], ones], axis=0)
    ones2 = jnp.ones((ONES_ROWS, MXU_DEPTH), BF16)
    for u in range(vals.shape[1] // MXU_DEPTH):
        vs2_ref[0, u] = jnp.concatenate([vals[0:HEAD_DIM, u * MXU_DEPTH:(u + 1) * MXU_DEPTH], ones2], axis=0)


def _nsa_keys(pt, rope, tk):
    B, _, T = pt.shape
    return pl.pallas_call(
        _nsa_keys_kernel,
        grid=(B, T // tk),
        in_specs=[
            pl.BlockSpec((1, 4 * HEAD_DIM, tk), lambda b, i: (b, 0, i)),
            pl.BlockSpec((2 * ROPE_HALF, tk), lambda b, i: (0, i)),
        ],
        out_specs=[
            pl.BlockSpec((1, tk, LANES), lambda b, i: (b, i, 0)),
            pl.BlockSpec((1, tk // LANES, 2 * VAL_ROWS, LANES), lambda b, i: (b, i, 0, 0)),
            pl.BlockSpec((1, tk // MXU_DEPTH, VAL_ROWS, MXU_DEPTH), lambda b, i: (b, i, 0, 0)),
        ],
        out_shape=[
            jax.ShapeDtypeStruct((B, T, LANES), BF16),
            jax.ShapeDtypeStruct((B, T // LANES, 2 * VAL_ROWS, LANES), BF16),
            jax.ShapeDtypeStruct((B, T // MXU_DEPTH, VAL_ROWS, MXU_DEPTH), BF16),
        ],
        compiler_params=_params("parallel", "parallel"),
        name="nsa_keys",
    )(pt, rope)


def _nsa_compress_kernel(g_ref, wc_ref, pos_ref, w2_ref, w2t_ref, rm_ref, t_ref):
    ng = g_ref.shape[1]
    wc = wc_ref[...]
    m = _dot(g_ref[0].astype(BF16), wc)
    pm = _dot(pos_ref[...], wc)
    pos_term = pm[0:1, 0:LANES] + pm[1:2, LANES:2 * LANES]
    pre = m[:, 0:LANES] + pltpu.roll(m[:, LANES:2 * LANES], ng - 1, axis=0) + pos_term
    act = _silu(pre).astype(BF16)
    row = lax.broadcasted_iota(jnp.int32, (ng, 1), 0)
    col = lax.broadcasted_iota(jnp.int32, (1, ng), 1)
    rm_ref[0] = jnp.where(row < ng - 1, _dot(act, w2_ref[...]), 0.0).astype(BF16)
    vt = jnp.where(col < ng - 1, _dot_nt(w2t_ref[...], act)[HEAD_DIM:2 * HEAD_DIM], 0.0)
    t_ref[0] = jnp.concatenate([vt, jnp.ones((ONES_ROWS, ng), F32)], axis=0).astype(BF16)


def _nsa_compress(kcvc, cmp_pos, k_w1, k_w2, v_w1, v_w2):
    B, T, _ = kcvc.shape
    ng = T // CMP_STRIDE
    half = CMP_BLOCK // 2
    g = kcvc.reshape(B, ng, half * LANES)
    D = HEAD_DIM

    def spread(w, second, is_v):
        blk = w[second * half * D:(second + 1) * half * D].reshape(half, D, D)
        z = jnp.zeros_like(blk)
        return jnp.concatenate([z, blk] if is_v else [blk, z], axis=1).reshape(half * LANES, D)

    wc = jnp.concatenate([spread(k_w1, 0, False), spread(v_w1, 0, True),
                          spread(k_w1, 1, False), spread(v_w1, 1, True)], axis=1).astype(BF16)
    pos2 = jnp.concatenate([cmp_pos, cmp_pos], axis=1)
    pos = jnp.zeros((8, half * LANES), F32)
    pos = pos.at[0].set(pos2[:half].reshape(-1)).at[1].set(pos2[half:].reshape(-1)).astype(BF16)
    z = jnp.zeros((D, D), F32)
    w2 = jnp.block([[k_w2, z], [z, v_w2]])
    const = lambda shape: pl.BlockSpec(shape, lambda b: (0,) * len(shape))
    return pl.pallas_call(
        _nsa_compress_kernel,
        grid=(B,),
        in_specs=[
            pl.BlockSpec((1, ng, half * LANES), lambda b: (b, 0, 0)),
            const((half * LANES, 2 * LANES)), const((8, half * LANES)),
            const((LANES, LANES)), const((LANES, LANES)),
        ],
        out_specs=[
            pl.BlockSpec((1, ng, LANES), lambda b: (b, 0, 0)),
            pl.BlockSpec((1, VAL_ROWS, ng), lambda b: (b, 0, 0)),
        ],
        out_shape=[
            jax.ShapeDtypeStruct((B, ng, LANES), BF16),
            jax.ShapeDtypeStruct((B, VAL_ROWS, ng), BF16),
        ],
        compiler_params=_params("parallel"),
        name="nsa_compress",
    )(g, wc, pos, w2.astype(BF16), w2.T.astype(BF16))


def _tile_heads(x):
    return jnp.concatenate([x] * NSA_HEADS, axis=1)


def _nsa_attn_kernel(q_ref, gate_ref, glog_ref, rope_ref, rm_ref, ct_ref, kskw_ref, vt_ref, vs2_ref,
                     ov_ref, oh_ref, gb_ref, og_ref, y_ref, m_ref, acc_ref, sa_ref, sb_ref, ma_ref,
                     mb_ref):
    D, Q, Hn = HEAD_DIM, Q_BLOCK, NSA_HEADS
    KC = SEL_KEY_CHUNK
    ng = rm_ref.shape[1]
    ns = ov_ref.shape[0]
    n_top = min(SEL_TOPK, ns)
    i = pl.program_id(1)
    t0 = i * Q
    tq = t0 + lax.broadcasted_iota(jnp.int32, (1, Q), 1)

    q = q_ref[0] * (D ** -0.5 * LOG2E)
    cos, sin = rope_ref[0:ROPE_HALF], rope_ref[ROPE_HALF:2 * ROPE_HALF]
    qh = [q[h * D:(h + 1) * D] for h in range(Hn)]
    q4 = jnp.concatenate(qh, axis=1)
    q4r = jnp.concatenate([_rope_rows(x, cos, sin) for x in qh], axis=1)
    zero = jnp.zeros_like(q4)
    q_lo = jnp.concatenate([q4, zero], axis=0).astype(BF16)
    qr_lo = jnp.concatenate([q4r, zero], axis=0).astype(BF16)
    qr_hi = jnp.concatenate([zero, q4r], axis=0).astype(BF16)

    def masked(s, bias):
        return jnp.concatenate([s[:, h * Q:(h + 1) * Q] + bias for h in range(Hn)], axis=1)


    cend = lax.broadcasted_iota(jnp.int32, (ng, 1), 0) * CMP_STRIDE + (CMP_BLOCK - 1)
    s = masked(_dot(rm_ref[0], q_lo), jnp.where(cend <= tq, 0.0, NEG_INF))
    eb = jnp.exp2(s - jnp.max(s, axis=0, keepdims=True)).astype(BF16)
    ol = _dot(ct_ref[0], eb)
    seen = _tile_heads((tq >= CMP_BLOCK - 1).astype(F32))
    inv = seen / jnp.maximum(ol[D:D + 1], 1e-30)
    o_c = ol[0:D] * inv
    oe = _dot(ov_ref[...], eb) * inv
    imp = sum(oe[:, h * Q:(h + 1) * Q] for h in range(Hn))

    WK = WINDOW + Q
    w0 = pl.multiple_of(jnp.maximum(t0 - WINDOW, 0), LANES)
    diff = tq - (w0 + lax.broadcasted_iota(jnp.int32, (WK, 1), 0))
    w_bias = jnp.where((diff >= 0) & (diff < WINDOW), 0.0, NEG_INF)

    def window_head(h):
        s = _dot(kskw_ref[0, pl.ds(w0, WK), :], qr_hi[:, h * Q:(h + 1) * Q]) + w_bias
        eb = jnp.exp2(s - jnp.max(s, axis=0, keepdims=True)).astype(BF16)
        ol = _dot(vt_ref[0, w0 // LANES][VAL_ROWS:2 * VAL_ROWS], eb[0:LANES])
        for u in range(1, WK // LANES):
            ol = ol + _dot(vt_ref[0, w0 // LANES + u][VAL_ROWS:2 * VAL_ROWS], eb[u * LANES:(u + 1) * LANES])
        return ol[0:D] / jnp.maximum(ol[D:D + 1], 1e-30)

    blk = lax.broadcasted_iota(jnp.int32, (ns, Q), 0)
    cur = tq // SEL_BLOCK
    forced = (blk == 0) | (blk == cur) | (blk == cur - 1)
    taken = -2.0
    score = jnp.where(forced, taken, jnp.where(blk <= cur, imp, -1.0))
    picks = max(n_top - 3, 0)
    o_w = []
    for it in range(picks):
        best = jnp.max(score, axis=0, keepdims=True)
        first = jnp.min(jnp.where(score == best, blk, ns), axis=0, keepdims=True)
        score = jnp.where(blk == first, taken, score)
        if it % 3 == 0 and len(o_w) < Hn:
            o_w.append(window_head(len(o_w)))
    while len(o_w) < Hn:
        o_w.append(window_head(len(o_w)))
    o_w = jnp.concatenate(o_w, axis=1)

    def with_block_mask(keep):
        bias = _tile_heads(jnp.where(keep, 0.0, NEG_INF))
        pad = oh_ref.shape[1] - ns
        if pad:
            bias = jnp.concatenate([bias, jnp.zeros((pad, Hn * Q), F32)], axis=0)
        return jnp.concatenate([qr_lo, bias.astype(BF16)], axis=0)

    chosen = (score == taken) & (blk <= cur)
    lk = lax.broadcasted_iota(jnp.int32, (Q, 1), 0)
    lq = lax.broadcasted_iota(jnp.int32, (1, Q), 1)
    own = pl.ds(pl.multiple_of(t0, Q), Q)
    s = masked(_dot(jnp.concatenate([kskw_ref[0, own, :], oh_ref[own, :]], axis=1), with_block_mask(chosen)),
               jnp.where(lk <= lq, 0.0, NEG_INF))
    m0 = jnp.max(s, axis=0, keepdims=True)
    m_ref[...] = m0
    e0 = jnp.exp2(s - m0).astype(BF16)
    QB = Q // MXU_DEPTH
    acc_ref[...] = sum(_dot(vs2_ref[0, i * QB + u], e0[u * MXU_DEPTH:(u + 1) * MXU_DEPTH]) for u in range(QB))

    q_sel = with_block_mask(chosen & (blk < t0 // SEL_BLOCK))

    def sel_scores(j, s_ref, cmax_ref):
        k0 = pl.multiple_of(j * KC, KC)
        keys = jnp.concatenate([kskw_ref[0, pl.ds(k0, KC), :], oh_ref[pl.ds(k0, KC), :]], axis=1)
        s = _dot(keys, q_sel)
        s_ref[...] = s
        cmax_ref[...] = jnp.max(s, axis=0, keepdims=True)

    UB = KC // MXU_DEPTH

    def sel_update(j, s_ref, cmax_ref):
        m_old = m_ref[...]
        m_new = jnp.maximum(m_old, cmax_ref[...])
        eb = jnp.exp2(s_ref[...] - m_new).astype(BF16)
        pv = acc_ref[...] * jnp.exp2(m_old - m_new)
        for u in range(UB):
            pv = pv + _dot(vs2_ref[0, j * UB + u], eb[u * MXU_DEPTH:(u + 1) * MXU_DEPTH])
        acc_ref[...] = pv
        m_ref[...] = m_new

    n_chunks = (t0 + KC - 1) // KC
    last = kskw_ref.shape[1] // KC - 1
    sel_scores(0, sa_ref, ma_ref)

    def chunk_pair(j, carry):
        sel_scores(jnp.minimum(2 * j + 1, last), sb_ref, mb_ref)
        sel_update(2 * j, sa_ref, ma_ref)
        sel_scores(jnp.minimum(2 * j + 2, last), sa_ref, ma_ref)
        sel_update(jnp.minimum(2 * j + 1, last), sb_ref, mb_ref)
        return carry

    lax.fori_loop(0, (n_chunks + 1) // 2, chunk_pair, 0)
    acc = acc_ref[...]
    o_s = acc[0:D] / jnp.maximum(acc[D:D + 1], 1e-30)

    gl = _sigmoid(glog_ref[0] + gb_ref[...])
    ys = []
    for h in range(Hn):
        cs = slice(h * Q, (h + 1) * Q)
        o = (gl[3 * h:3 * h + 1] * o_c[:, cs] + gl[3 * h + 1:3 * h + 2] * o_s[:, cs]
             + gl[3 * h + 2:3 * h + 3] * o_w[:, cs])
        o = o * lax.rsqrt(jnp.mean(o * o, axis=0, keepdims=True) + NORM_EPS)
        ys.append(o)
    y = jnp.concatenate(ys, axis=0) * og_ref[...] * _silu(gate_ref[0])
    y_ref[0] = y.T.astype(y_ref.dtype)


def _nsa_attention(pt, rope, cmp_rm, cmp_t, kskw, vt, vs2, gate_b, out_g):
    B, _, T = pt.shape
    ng = T // CMP_STRIDE
    ns = T // SEL_BLOCK
    Q, W = Q_BLOCK, NSA_WIDTH
    QH = Q * NSA_HEADS
    c0 = np.arange(ng)[None, :] * CMP_STRIDE
    s0 = np.arange(ns)[:, None] * SEL_BLOCK
    ov = np.clip(np.minimum(c0 + CMP_BLOCK, s0 + SEL_BLOCK) - np.maximum(c0, s0), 0, None) / CMP_BLOCK
    ov[:, ng - 1] = 0.0
    oh_lanes = -(-ns // LANES) * LANES
    onehot = (np.arange(T)[:, None] // SEL_BLOCK) == np.arange(oh_lanes)[None, :]
    gb = jnp.zeros((16, 1), F32).at[:NSA_HEADS * N_BRANCH, 0].set(gate_b)
    glog_blk = (4 * HEAD_DIM + 4 * W) // 16
    return pl.pallas_call(
        _nsa_attn_kernel,
        grid=(B, T // Q),
        in_specs=[
            pl.BlockSpec((1, W, Q), lambda b, i: (b, 1, i)),
            pl.BlockSpec((1, W, Q), lambda b, i: (b, 2, i)),
            pl.BlockSpec((1, 16, Q), lambda b, i: (b, glog_blk, i)),
            pl.BlockSpec((2 * ROPE_HALF, Q), lambda b, i: (0, i)),
            pl.BlockSpec((1, ng, LANES), lambda b, i: (b, 0, 0)),
            pl.BlockSpec((1, VAL_ROWS, ng), lambda b, i: (b, 0, 0)),
            pl.BlockSpec((1, T, LANES), lambda b, i: (b, 0, 0)),
            pl.BlockSpec((1, T // LANES, 2 * VAL_ROWS, LANES), lambda b, i: (b, 0, 0, 0)),
            pl.BlockSpec((1, T // MXU_DEPTH, VAL_ROWS, MXU_DEPTH), lambda b, i: (b, 0, 0, 0)),
            pl.BlockSpec((ns, ng), lambda b, i: (0, 0)),
            pl.BlockSpec((T, oh_lanes), lambda b, i: (0, 0)),
            pl.BlockSpec((16, 1), lambda b, i: (0, 0)),
            pl.BlockSpec((W, 1), lambda b, i: (0, 0)),
        ],
        out_specs=pl.BlockSpec((1, Q, W), lambda b, i: (b, i, 0)),
        out_shape=jax.ShapeDtypeStruct((B, T, W), BF16),
        scratch_shapes=[pltpu.VMEM((1, QH), F32), pltpu.VMEM((VAL_ROWS, QH), F32),
                        pltpu.VMEM((SEL_KEY_CHUNK, QH), F32), pltpu.VMEM((SEL_KEY_CHUNK, QH), F32),
                        pltpu.VMEM((1, QH), F32), pltpu.VMEM((1, QH), F32)],
        compiler_params=_params("parallel", "arbitrary"),
        name="nsa_attention",
    )(pt, pt, pt, rope, cmp_rm, cmp_t, kskw, vt, vs2, jnp.asarray(ov, BF16), jnp.asarray(onehot, BF16), gb,
      out_g.reshape(W, 1))


def _mem_kv_kernel(mem_ref, g_ref, w_ref, wt_ref, k_ref, vt_ref):
    x = mem_ref[0]
    h = x * lax.rsqrt(jnp.mean(x * x, axis=-1, keepdims=True) + NORM_EPS) * g_ref[...]
    hb = h.astype(BF16)
    k_ref[0] = _dot(hb, w_ref[...]).astype(BF16)
    vt = _dot_nt(wt_ref[...], hb)
    ones = jnp.ones((ONES_ROWS, vt.shape[1]), F32)
    vt_ref[0] = jnp.concatenate(
        [part for h in range(MEM_HEADS) for part in (vt[h * HEAD_DIM:(h + 1) * HEAD_DIM], ones)],
        axis=0).astype(BF16)


def _mem_kv(mem, g, w_kv):
    B, M, D = mem.shape
    W = MEM_WIDTH
    const = lambda shape: pl.BlockSpec(shape, lambda b: (0,) * len(shape))
    return pl.pallas_call(
        _mem_kv_kernel,
        grid=(B,),
        in_specs=[pl.BlockSpec((1, M, D), lambda b: (b, 0, 0)), const((1, D)), const((D, W)),
                  const((W, D))],
        out_specs=[pl.BlockSpec((1, M, W), lambda b: (b, 0, 0)),
                   pl.BlockSpec((1, MEM_HEADS * VAL_ROWS, M), lambda b: (b, 0, 0))],
        out_shape=[jax.ShapeDtypeStruct((B, M, W), BF16),
                   jax.ShapeDtypeStruct((B, MEM_HEADS * VAL_ROWS, M), BF16)],
        compiler_params=_params("parallel"),
        name="mem_kv",
    )(mem, g.reshape(1, D), w_kv[:, :W].astype(BF16), w_kv[:, W:].T.astype(BF16))


def _mem_attn_kernel(q_ref, gate_ref, k_ref, vt_ref, og_ref, y_ref):
    D, Hm = HEAD_DIM, MEM_HEADS
    q = q_ref[0] * (D ** -0.5 * LOG2E)
    k = k_ref[0]
    vt = vt_ref[0]
    row = lax.broadcasted_iota(jnp.int32, (Hm * D, 1), 0)
    ys = []
    for h in range(Hm):
        qh = jnp.where(row // D == h, q, 0.0).astype(BF16)
        s = _dot(k, qh)
        eb = jnp.exp2(s - jnp.max(s, axis=0, keepdims=True)).astype(BF16)
        ol = _dot(vt[h * VAL_ROWS:(h + 1) * VAL_ROWS], eb)
        o = ol[0:D] / ol[D:D + 1]
        ys.append(o * lax.rsqrt(jnp.mean(o * o, axis=0, keepdims=True) + NORM_EPS))
    y = jnp.concatenate(ys, axis=0) * og_ref[...] * _silu(gate_ref[0])
    y_ref[0] = y.T.astype(y_ref.dtype)


def _mem_attention(pt, mem_k, mem_vt, out_g, tm):
    B, _, T = pt.shape
    M = mem_k.shape[1]
    W = MEM_WIDTH
    return pl.pallas_call(
        _mem_attn_kernel,
        grid=(B, T // tm),
        in_specs=[
            pl.BlockSpec((1, W, tm), lambda b, i: (b, 3, i)),
            pl.BlockSpec((1, W, tm), lambda b, i: (b, 4, i)),
            pl.BlockSpec((1, M, W), lambda b, i: (b, 0, 0)),
            pl.BlockSpec((1, MEM_HEADS * VAL_ROWS, M), lambda b, i: (b, 0, 0)),
            pl.BlockSpec((W, 1), lambda b, i: (0, 0)),
        ],
        out_specs=pl.BlockSpec((1, tm, W), lambda b, i: (b, i, 0)),
        out_shape=jax.ShapeDtypeStruct((B, T, W), BF16),
        compiler_params=_params("parallel", "parallel"),
        name="mem_attention",
    )(pt, pt, mem_k, mem_vt, out_g.reshape(W, 1))


def _outproj_kernel(x_ref, yr_ref, yn_ref, ym_ref, wr_ref, wn_ref, wm_ref, g_ref, o_ref):
    z = (x_ref[0] + _dot(yr_ref[0], wr_ref[...]) + _dot(yn_ref[0], wn_ref[...])
         + _dot(ym_ref[0], wm_ref[...]))
    o_ref[0] = z * lax.rsqrt(jnp.mean(z * z, axis=-1, keepdims=True) + NORM_EPS) * g_ref[...]


def _output_projection(x, y_rwkv, y_nsa, y_mem, w_out, g, tm):
    B, T, D = x.shape
    wb = w_out.astype(BF16)
    w_r, w_n, w_m = wb[:RWKV_WIDTH], wb[RWKV_WIDTH:RWKV_WIDTH + NSA_WIDTH], wb[RWKV_WIDTH + NSA_WIDTH:]
    tile = lambda w: pl.BlockSpec((1, tm, w), lambda b, i: (b, i, 0))
    const = lambda shape: pl.BlockSpec(shape, lambda b, i: (0,) * len(shape))
    return pl.pallas_call(
        _outproj_kernel,
        grid=(B, T // tm),
        in_specs=[tile(D), tile(RWKV_WIDTH), tile(NSA_WIDTH), tile(MEM_WIDTH),
                  const((RWKV_WIDTH, D)), const((NSA_WIDTH, D)), const((MEM_WIDTH, D)), const((1, D))],
        out_specs=tile(D),
        out_shape=jax.ShapeDtypeStruct((B, T, D), F32),
        compiler_params=_params("parallel", "parallel"),
        name="output_projection",
    )(x, y_rwkv, y_nsa, y_mem, w_r, w_n, w_m, g.reshape(1, D))


def _rope_table(T):
    inv_freq = ROPE_THETA ** (-jnp.arange(ROPE_HALF, dtype=F32) / ROPE_HALF)
    ang = inv_freq[:, None] * jnp.arange(T).astype(F32)[None, :]
    return jnp.concatenate([jnp.cos(ang), jnp.sin(ang)], axis=0)


def _split_w_in(w):
    D = HEAD_DIM
    n0 = RWKV_COLS
    q, gate, glog = n0, n0 + NSA_WIDTH, n0 + 2 * NSA_WIDTH
    kc = glog + NSA_HEADS * N_BRANCH
    vc, ks, vs, kw, vw = kc + D, kc + 2 * D, kc + 3 * D, kc + 4 * D, kc + 5 * D
    m0 = vw + D
    cols = lambda a, n: w[:, a:a + n]
    w_row = jnp.concatenate([cols(0, n0), cols(kc, D), cols(vc, D)], axis=1)
    w_t = jnp.concatenate([cols(ks, D), cols(kw, D), cols(vs, D), cols(vw, D), cols(q, NSA_WIDTH),
                           cols(gate, NSA_WIDTH), cols(m0, MEM_WIDTH), cols(m0 + MEM_WIDTH, MEM_WIDTH),
                           cols(glog, NSA_HEADS * N_BRANCH),
                           jnp.zeros((w.shape[0], 16 - NSA_HEADS * N_BRANCH), w.dtype)], axis=1)
    return w_row.astype(BF16), w_t.T.astype(BF16)


def kernel(x, mem, norm_in_g, w_in, rwkv_mu, rwkv_w0, rwkv_w_up, rwkv_a0, rwkv_a_up, rwkv_k_k,
           rwkv_k_a, rwkv_r_k, rwkv_ln_w, rwkv_ln_b, nsa_cmp_pos, nsa_cmp_k_w1, nsa_cmp_k_w2,
           nsa_cmp_v_w1, nsa_cmp_v_w2, nsa_gate_b, nsa_out_g, mem_norm_g, w_mem_kv, mem_out_g, w_out,
           norm_final_g):
    B, T, D = x.shape
    assert w_in.shape[0] == 1, "single-layer stack: the final norm is fused into the output projection"
    rope = _rope_table(T)
    w_row, w_t = _split_w_in(w_in[0])
    p_row, kcvc, pt = _input_projection(x, norm_in_g[0].reshape(1, D), w_row, w_t, tm=256)
    y_rwkv = _rwkv_group(p_row, rwkv_mu[0], rwkv_w0[0], rwkv_w_up[0], rwkv_a0[0], rwkv_a_up[0],
                         rwkv_k_k[0], rwkv_k_a[0], rwkv_r_k[0].reshape(-1), rwkv_ln_w[0],
                         rwkv_ln_b[0])
    kskw, vt, vs2 = _nsa_keys(pt, rope, tk=512)
    cmp_rm, cmp_t = _nsa_compress(kcvc, nsa_cmp_pos[0], nsa_cmp_k_w1[0], nsa_cmp_k_w2[0],
                                  nsa_cmp_v_w1[0], nsa_cmp_v_w2[0])
    y_nsa = _nsa_attention(pt, rope, cmp_rm, cmp_t, kskw, vt, vs2, nsa_gate_b[0], nsa_out_g[0])
    mem_k, mem_vt = _mem_kv(mem, mem_norm_g[0], w_mem_kv[0])
    y_mem = _mem_attention(pt, mem_k, mem_vt, mem_out_g[0], tm=512)
    return _output_projection(x, y_rwkv, y_nsa, y_mem, w_out[0], norm_final_g, tm=512)
```

```python
import functools

import numpy as np
import jax
import jax.numpy as jnp
from jax import lax
from jax.experimental import pallas as pl
from jax.experimental.pallas import tpu as pltpu

F32 = jnp.float32
BF16 = jnp.bfloat16

HEAD_DIM = 64
RWKV_HEADS = 8
RWKV_WIDTH = RWKV_HEADS * HEAD_DIM
LORA = 64
RWKV_COLS = 4 * RWKV_WIDTH + 2 * LORA
RWKV_GN_EPS = 64e-5
NSA_HEADS = 4
NSA_WIDTH = NSA_HEADS * HEAD_DIM
N_BRANCH = 3
CMP_BLOCK = 32
CMP_STRIDE = 16
SEL_BLOCK = 64
SEL_TOPK = 16
WINDOW = 512
MEM_HEADS = 4
MEM_WIDTH = MEM_HEADS * HEAD_DIM
ROPE_THETA = 500000.0
ROPE_HALF = 8
Q_BLOCK = 256
NORM_EPS = 1e-6
NEG_INF = -1e30
FORCE_SCORE = 1e4
LOG2E = 1.4426950408889634
DECAY_SCALE = 0.6065306597126334

RWKV_CHUNK = 64
RWKV_HEAD_GROUP = 4
RWKV_BATCH_TILE = 4
SEL_KEY_CHUNK = 512
INPROJ_SUBTILE = 256
LANES = 128
MXU_DEPTH = 256
ONES_ROWS = 16
VAL_ROWS = HEAD_DIM + ONES_ROWS
VMEM_LIMIT = 48 * 1024 * 1024

def _dot(a, b):
    return jnp.dot(a, b, preferred_element_type=F32)


def _dot_nt(a, b):
    return lax.dot_general(a, b, (((1,), (1,)), ((), ())), preferred_element_type=F32)


def _dot_tn(a, b):
    return lax.dot_general(a, b, (((0,), (0,)), ((), ())), preferred_element_type=F32)


def _bf(a):
    return a.astype(BF16)


def _split_bf16(a, pieces):
    out = []
    for _ in range(pieces):
        part = a.astype(BF16)
        out.append(part)
        a = a - part.astype(F32)
    return out


def _sigmoid(x):
    return 1.0 / (1.0 + jnp.exp(-x))


def _silu(x):
    return x * _sigmoid(x)


def _params(*sem):
    return pltpu.CompilerParams(dimension_semantics=sem, vmem_limit_bytes=VMEM_LIMIT)


def _inproj_kernel(x_ref, g_ref, wrow_ref, wt_ref, prow_ref, kc_ref, pt_ref):
    tm = x_ref.shape[1]
    hbs = []
    for r0 in range(0, tm, INPROJ_SUBTILE):
        x = x_ref[0, r0:r0 + INPROJ_SUBTILE, :]
        h = x * lax.rsqrt(jnp.mean(x * x, axis=-1, keepdims=True) + NORM_EPS) * g_ref[...]
        hbs.append(h.astype(BF16))
    for k, hb in enumerate(hbs):
        rows = slice(k * INPROJ_SUBTILE, (k + 1) * INPROJ_SUBTILE)
        row = _dot(hb, wrow_ref[...])
        prow_ref[0, rows, :] = row[:, :RWKV_COLS]
        kc_ref[0, rows, :] = row[:, RWKV_COLS:]
        pt_ref[0, :, rows] = _dot_nt(wt_ref[...], hb).astype(pt_ref.dtype)


def _input_projection(x, g, w_row, w_t, tm):
    B, T, D = x.shape
    n_row = w_row.shape[1]
    n_t = w_t.shape[0]
    return pl.pallas_call(
        _inproj_kernel,
        grid=(B, T // tm),
        in_specs=[
            pl.BlockSpec((1, tm, D), lambda b, i: (b, i, 0)),
            pl.BlockSpec((1, D), lambda b, i: (0, 0)),
            pl.BlockSpec((D, n_row), lambda b, i: (0, 0)),
            pl.BlockSpec((n_t, D), lambda b, i: (0, 0)),
        ],
        out_specs=[
            pl.BlockSpec((1, tm, RWKV_COLS), lambda b, i: (b, i, 0)),
            pl.BlockSpec((1, tm, n_row - RWKV_COLS), lambda b, i: (b, i, 0)),
            pl.BlockSpec((1, n_t, tm), lambda b, i: (b, 0, i)),
        ],
        out_shape=[
            jax.ShapeDtypeStruct((B, T, RWKV_COLS), F32),
            jax.ShapeDtypeStruct((B, T, n_row - RWKV_COLS), F32),
            jax.ShapeDtypeStruct((B, n_t, T), BF16),
        ],
        compiler_params=_params("parallel", "parallel"),
        name="input_projection",
    )(x, g, w_row, w_t)


_X_R, _X_A, _X_K, _X_B, _X_V, _X_BV, _X_SG, _X_G = range(8)


def _rwkv_kernel(podd_ref, peven_ref, prevodd_ref, preveven_ref, mu_ref, w0_ref, wup_ref, a0_ref,
                 aup_ref, kk_ref, ka_ref, rk_ref, lnw_ref, lnb_ref, bd_ref, tri_ref, y_ref,
                 s_ref, xa_ref, xb_ref):
    C, W, N, H = RWKV_CHUNK, RWKV_WIDTH, HEAD_DIM, RWKV_HEADS
    HG = RWKV_HEAD_GROUP
    GW = HG * N
    n = pl.program_id(1)
    n_chunks = 2 * (pl.num_programs(1) - 1)
    bt = podd_ref.shape[0]

    @pl.when(n == 0)
    def _():
        s_ref[...] = jnp.zeros_like(s_ref)
        xa_ref[...] = jnp.zeros_like(xa_ref)
        xb_ref[...] = jnp.zeros_like(xb_ref)

    bd = bd_ref[...]
    head_sum = lambda a: jnp.concatenate(
        [_dot(_bf(a[:, g * GW:(g + 1) * GW]), bd) for g in range(H // HG)], axis=1)
    row = lax.broadcasted_iota(jnp.int32, (C, 1), 0)

    def prepare(b, chunk, p_ref, prev_ref, x_ref):
        p = p_ref[b]
        prev_last = jnp.where(chunk <= 0, 0.0, prev_ref[b][7:8, :])
        prev = jnp.where(row == 0, prev_last, pltpu.roll(p, 1, axis=0))
        pf = p + mu_ref[...] * (prev - p)
        r = pf[:, 0:W]
        k = pf[:, W:2 * W]
        v = pf[:, 2 * W:3 * W]
        gate = pf[:, 3 * W:4 * W]
        wd = _bf(jnp.tanh(pf[:, 4 * W:4 * W + LORA]))
        ad = _bf(pf[:, 4 * W + LORA:4 * W + 2 * LORA])
        yield
        z = w0_ref[...] + _dot(wd, wup_ref[...])
        logw = -DECAY_SCALE * _sigmoid(z)
        eta = _sigmoid(a0_ref[...] + _dot(ad, aup_ref[...]))
        kk = k * kk_ref[...]
        k2 = k * (1.0 + (eta - 1.0) * ka_ref[...])
        kk_sq = _bf(kk * kk)
        rk2 = _bf(r * k2 * rk_ref[...])
        logw_parts = _split_bf16(logw, 3)
        yield
        kk = kk * lax.rsqrt(jnp.maximum(head_sum(kk_sq), 1e-24))
        x_ref[b, _X_BV] = head_sum(rk2) * v
        x_ref[b, _X_SG] = _silu(gate)
        tri = tri_ref[...]
        cs = sum(_dot(tri, part) for part in logw_parts)
        mid = cs[C // 2 - 1:C // 2, :]
        csm = cs - mid
        end = cs[C - 1:C, :]
        x_ref[b, _X_G] = jnp.concatenate([jnp.exp(mid), jnp.exp(end), jnp.exp(end - mid),
                                          jnp.zeros((C - 3, W), F32)], axis=0)
        yield
        e_out = jnp.exp(-csm)
        x_ref[b, _X_R] = r * jnp.exp(csm)
        x_ref[b, _X_A] = -kk * jnp.exp(csm - logw)
        yield
        x_ref[b, _X_K] = k2 * e_out
        x_ref[b, _X_B] = kk * eta * e_out
        x_ref[b, _X_V] = v

    li = lax.broadcasted_iota(jnp.int32, (C, GW), 1) % N
    ti = lax.broadcasted_iota(jnp.int32, (C, GW), 0)
    strict, incl = ti > li, ti >= li
    eye = (ti == li).astype(F32)
    same_head = (lax.broadcasted_iota(jnp.int32, (GW, GW), 0) // N
                 == lax.broadcasted_iota(jnp.int32, (GW, GW), 1) // N)

    def blockdiag(a):
        a = _bf(a)
        return jnp.where(same_head, jnp.concatenate([a] * HG, axis=0), jnp.zeros((), BF16))

    def advance(chunk, x_ref, y_rows, fill):
        live = (chunk >= 0) & (chunk < n_chunks)
        chains = [(b, g) for b in range(bt) for g in range(H // HG)]
        st = {}
        for c in chains:
            b, g = c
            ls = slice(g * GW, (g + 1) * GW)
            kb = jnp.concatenate([blockdiag(x_ref[b, _X_K][:, ls]), blockdiag(x_ref[b, _X_B][:, ls])],
                                 axis=0)
            ar = _bf(jnp.concatenate([x_ref[b, _X_A][:, ls], x_ref[b, _X_R][:, ls]], axis=0))
            gc = _dot_nt(ar, kb)
            s0 = s_ref[b, g]
            gates = x_ref[b, _X_G]
            st[c] = dict(
                ls=ls, kb=kb, s0=s0, v_bd=blockdiag(x_ref[b, _X_V][:, ls]),
                g_end=gates[1:2, ls], g_end_mid=gates[2:3, ls],
                a_ak=jnp.where(strict, gc[0:C, 0:GW], 0.0), a_ab=jnp.where(strict, gc[0:C, GW:2 * GW], 0.0),
                a_rk=jnp.where(incl, gc[C:2 * C, 0:GW], 0.0), a_rb=jnp.where(incl, gc[C:2 * C, GW:2 * GW], 0.0),
                uy0=_dot_nt(ar, _bf(s0 * gates[0:1, ls])))
            fill()
        for c in chains:
            s = st[c]
            s["rhs"] = s["uy0"][0:C] + _dot(_bf(s["a_ak"]), s["v_bd"])
            s["tm"] = eye + s["a_ab"]
            s["pw"] = _dot(_bf(s["a_ab"]), blockdiag(s["a_ab"]))
            fill()
        span = 2
        while span < C:
            for c in chains:
                s = st[c]
                if 2 * span < C:
                    tp = _dot(_bf(jnp.concatenate([s["tm"], s["pw"]], axis=0)), blockdiag(s["pw"]))
                    s["tm"] = s["tm"] + tp[0:C]
                    s["pw"] = tp[C:2 * C]
                else:
                    s["tm"] = s["tm"] + _dot(_bf(s["tm"]), blockdiag(s["pw"]))
                fill()
            span *= 2
        for c in chains:
            s = st[c]
            s["u"] = _dot(_bf(s["tm"]), blockdiag(s["rhs"]))
            fill()
        for c in chains:
            s = st[c]
            s["vu"] = jnp.concatenate([s["v_bd"], blockdiag(s["u"])], axis=0)
            s["y"] = s["uy0"][C:2 * C] + _dot(_bf(jnp.concatenate([s["a_rk"], s["a_rb"]], axis=1)), s["vu"])
            fill()
        for c in chains:
            b, g = c
            s = st[c]
            ls = s["ls"]
            vu_rows = jnp.concatenate([_bf(x_ref[b, _X_V][:, ls]), _bf(s["u"])], axis=0)
            kb_rows = _bf(jnp.concatenate([x_ref[b, _X_K][:, ls], x_ref[b, _X_B][:, ls]], axis=0))
            keep = jnp.where(live, s["g_end"], 1.0)
            gain = jnp.where(live, s["g_end_mid"], 0.0)
            s_ref[b, g] = s["s0"] * keep + jnp.where(same_head, _dot_tn(vu_rows, kb_rows) * gain, 0.0)
            fill()
        for b in range(bt):
            y = jnp.concatenate([st[(b, g)]["y"] for g in range(H // HG)], axis=1)
            mean = head_sum(y) * (1.0 / N)
            var = head_sum(jnp.square(y - mean)) * (1.0 / N)
            y = (y - mean) * lax.rsqrt(var + RWKV_GN_EPS) * lnw_ref[...] + lnb_ref[...] + x_ref[b, _X_BV]
            y_ref[b, y_rows] = (y * x_ref[b, _X_SG]).astype(y_ref.dtype)
            fill()

    def filler(gens):
        gens = list(gens)

        def fill():
            while gens:
                try:
                    next(gens[0])
                    return
                except StopIteration:
                    gens.pop(0)

        def drain():
            while gens:
                fill()

        return fill, drain

    fill, drain = filler(prepare(b, 2 * n - 1, podd_ref, prevodd_ref, xb_ref) for b in range(bt))
    advance(2 * n - 2, xa_ref, slice(0, C), fill)
    drain()
    fill, drain = filler(prepare(b, 2 * n, peven_ref, preveven_ref, xa_ref) for b in range(bt))
    advance(2 * n - 1, xb_ref, slice(C, 2 * C), fill)
    drain()


def _rwkv_group(p_row, mu, w0, w_up, a0, a_up, k_k, k_a, r_k, ln_w, ln_b):
    B, T, _ = p_row.shape
    C, W = RWKV_CHUNK, RWKV_WIDTH
    heads = np.arange(RWKV_HEAD_GROUP * HEAD_DIM) // HEAD_DIM
    bd = jnp.asarray(heads[:, None] == heads[None, :], BF16)
    tri = jnp.asarray(np.tril(np.ones((C, C), np.float32)), BF16)
    vec = lambda a: a.reshape(1, -1)
    const = lambda shape: pl.BlockSpec(shape, lambda b, n: (0,) * len(shape))
    bt = RWKV_BATCH_TILE if B % RWKV_BATCH_TILE == 0 else 1
    n_chunks = T // C
    assert n_chunks % 2 == 0
    odd = lambda n: jnp.maximum(2 * n - 1, 0)
    even = lambda n: jnp.minimum(2 * n, n_chunks - 1)
    before = lambda c: jnp.maximum(c * (C // 8) - 1, 0)
    return pl.pallas_call(
        _rwkv_kernel,
        grid=(B // bt, n_chunks // 2 + 1),
        in_specs=[
            pl.BlockSpec((bt, C, RWKV_COLS), lambda b, n: (b, odd(n), 0)),
            pl.BlockSpec((bt, C, RWKV_COLS), lambda b, n: (b, even(n), 0)),
            pl.BlockSpec((bt, 8, RWKV_COLS), lambda b, n: (b, before(odd(n)), 0)),
            pl.BlockSpec((bt, 8, RWKV_COLS), lambda b, n: (b, before(even(n)), 0)),
            const((1, RWKV_COLS)), const((1, W)), const((LORA, W)), const((1, W)), const((LORA, W)),
            const((1, W)), const((1, W)), const((1, W)), const((1, W)), const((1, W)),
            const(bd.shape), const((C, C)),
        ],
        out_specs=pl.BlockSpec((bt, 2 * C, W), lambda b, n: (b, jnp.maximum(n - 1, 0), 0)),
        out_shape=jax.ShapeDtypeStruct((B, T, W), BF16),
        scratch_shapes=[pltpu.VMEM((bt, RWKV_HEADS // RWKV_HEAD_GROUP) + (RWKV_HEAD_GROUP * HEAD_DIM,) * 2,
                                   F32),
                        pltpu.VMEM((bt, 8, C, W), F32), pltpu.VMEM((bt, 8, C, W), F32)],
        compiler_params=_params("parallel", "arbitrary"),
        name="rwkv7_group",
    )(p_row, p_row, p_row, p_row, vec(mu), vec(w0), _bf(w_up), vec(a0), _bf(a_up), vec(k_k), vec(k_a),
      vec(r_k), vec(ln_w), vec(ln_b), bd, tri)


def _rope_rows(x, cos, sin):
    x1, x2 = x[0:ROPE_HALF], x[ROPE_HALF:2 * ROPE_HALF]
    return jnp.concatenate([x1 * cos - x2 * sin, x2 * cos + x1 * sin, x[2 * ROPE_HALF:]], axis=0)


def _nsa_keys_kernel(kv_ref, rope_ref, kskw_ref, vt_ref, vs2_ref):
    kv = kv_ref[0].astype(F32)
    cos, sin = rope_ref[0:ROPE_HALF], rope_ref[ROPE_HALF:2 * ROPE_HALF]
    keys = jnp.concatenate([_rope_rows(kv[0:HEAD_DIM], cos, sin),
                            _rope_rows(kv[HEAD_DIM:2 * HEAD_DIM], cos, sin)], axis=0)
    kskw_ref[0] = keys.T.astype(BF16)
    vals = kv[2 * HEAD_DIM:4 * HEAD_DIM].astype(BF16)
    ones = jnp.ones((ONES_ROWS, LANES), BF16)
    for u in range(vals.shape[1] // LANES):
        blk = vals[:, u * LANES:(u + 1) * LANES]
        vt_ref[0, u] = jnp.concatenate([blk[0:HEAD_DIM], ones, blk[HEAD_DIM:2 * HEAD_DIM], ones], axis=0)
    ones2 = jnp.ones((ONES_ROWS, MXU_DEPTH), BF16)
    for u in range(vals.shape[1] // MXU_DEPTH):
        vs2_ref[0, u] = jnp.concatenate([vals[0:HEAD_DIM, u * MXU_DEPTH:(u + 1) * MXU_DEPTH], ones2], axis=0)


def _nsa_keys(pt, rope, tk):
    B, _, T = pt.shape
    return pl.pallas_call(
        _nsa_keys_kernel,
        grid=(B, T // tk),
        in_specs=[
            pl.BlockSpec((1, 4 * HEAD_DIM, tk), lambda b, i: (b, 0, i)),
            pl.BlockSpec((2 * ROPE_HALF, tk), lambda b, i: (0, i)),
        ],
        out_specs=[
            pl.BlockSpec((1, tk, LANES), lambda b, i: (b, i, 0)),
            pl.BlockSpec((1, tk // LANES, 2 * VAL_ROWS, LANES), lambda b, i: (b, i, 0, 0)),
            pl.BlockSpec((1, tk // MXU_DEPTH, VAL_ROWS, MXU_DEPTH), lambda b, i: (b, i, 0, 0)),
        ],
        out_shape=[
            jax.ShapeDtypeStruct((B, T, LANES), BF16),
            jax.ShapeDtypeStruct((B, T // LANES, 2 * VAL_ROWS, LANES), BF16),
            jax.ShapeDtypeStruct((B, T // MXU_DEPTH, VAL_ROWS, MXU_DEPTH), BF16),
        ],
        compiler_params=_params("parallel", "parallel"),
        name="nsa_keys",
    )(pt, rope)


def _nsa_compress_kernel(g_ref, wc_ref, pos_ref, w2_ref, w2t_ref, rm_ref, t_ref):
    ng = g_ref.shape[1]
    wc = wc_ref[...]
    m = _dot(g_ref[0].astype(BF16), wc)
    pm = _dot(pos_ref[...], wc)
    pos_term = pm[0:1, 0:LANES] + pm[1:2, LANES:2 * LANES]
    pre = m[:, 0:LANES] + pltpu.roll(m[:, LANES:2 * LANES], ng - 1, axis=0) + pos_term
    act = _silu(pre).astype(BF16)
    row = lax.broadcasted_iota(jnp.int32, (ng, 1), 0)
    col = lax.broadcasted_iota(jnp.int32, (1, ng), 1)
    rm_ref[0] = jnp.where(row < ng - 1, _dot(act, w2_ref[...]), 0.0).astype(BF16)
    vt = jnp.where(col < ng - 1, _dot_nt(w2t_ref[...], act)[HEAD_DIM:2 * HEAD_DIM], 0.0)
    t_ref[0] = jnp.concatenate([vt, jnp.ones((ONES_ROWS, ng), F32)], axis=0).astype(BF16)


def _nsa_compress(kcvc, cmp_pos, k_w1, k_w2, v_w1, v_w2):
    B, T, _ = kcvc.shape
    ng = T // CMP_STRIDE
    half = CMP_BLOCK // 2
    g = kcvc.reshape(B, ng, half * LANES)
    D = HEAD_DIM

    def spread(w, second, is_v):
        blk = w[second * half * D:(second + 1) * half * D].reshape(half, D, D)
        z = jnp.zeros_like(blk)
        return jnp.concatenate([z, blk] if is_v else [blk, z], axis=1).reshape(half * LANES, D)

    wc = jnp.concatenate([spread(k_w1, 0, False), spread(v_w1, 0, True),
                          spread(k_w1, 1, False), spread(v_w1, 1, True)], axis=1).astype(BF16)
    pos2 = jnp.concatenate([cmp_pos, cmp_pos], axis=1)
    pos = jnp.zeros((8, half * LANES), F32)
    pos = pos.at[0].set(pos2[:half].reshape(-1)).at[1].set(pos2[half:].reshape(-1)).astype(BF16)
    z = jnp.zeros((D, D), F32)
    w2 = jnp.block([[k_w2, z], [z, v_w2]])
    const = lambda shape: pl.BlockSpec(shape, lambda b: (0,) * len(shape))
    return pl.pallas_call(
        _nsa_compress_kernel,
        grid=(B,),
        in_specs=[
            pl.BlockSpec((1, ng, half * LANES), lambda b: (b, 0, 0)),
            const((half * LANES, 2 * LANES)), const((8, half * LANES)),
            const((LANES, LANES)), const((LANES, LANES)),
        ],
        out_specs=[
            pl.BlockSpec((1, ng, LANES), lambda b: (b, 0, 0)),
            pl.BlockSpec((1, VAL_ROWS, ng), lambda b: (b, 0, 0)),
        ],
        out_shape=[
            jax.ShapeDtypeStruct((B, ng, LANES), BF16),
            jax.ShapeDtypeStruct((B, VAL_ROWS, ng), BF16),
        ],
        compiler_params=_params("parallel"),
        name="nsa_compress",
    )(g, wc, pos, w2.astype(BF16), w2.T.astype(BF16))


def _tile_heads(x):
    return jnp.concatenate([x] * NSA_HEADS, axis=1)


def _nsa_attn_kernel(q_ref, gate_ref, glog_ref, rope_ref, rm_ref, ct_ref, kskw_ref, vt_ref, vs2_ref,
                     ov_ref, oh_ref, gb_ref, og_ref, y_ref, m_ref, acc_ref, sa_ref, sb_ref, ma_ref,
                     mb_ref):
    D, Q, Hn = HEAD_DIM, Q_BLOCK, NSA_HEADS
    KC = SEL_KEY_CHUNK
    ng = rm_ref.shape[1]
    ns = ov_ref.shape[0]
    n_top = min(SEL_TOPK, ns)
    i = pl.program_id(1)
    t0 = i * Q
    tq = t0 + lax.broadcasted_iota(jnp.int32, (1, Q), 1)

    q = q_ref[0].astype(F32) * (D ** -0.5 * LOG2E)
    cos, sin = rope_ref[0:ROPE_HALF], rope_ref[ROPE_HALF:2 * ROPE_HALF]
    qh = [q[h * D:(h + 1) * D] for h in range(Hn)]
    q4 = jnp.concatenate(qh, axis=1)
    q4r = jnp.concatenate([_rope_rows(x, cos, sin) for x in qh], axis=1)
    zero = jnp.zeros_like(q4)
    q_lo = jnp.concatenate([q4, zero], axis=0).astype(BF16)
    qr_lo = jnp.concatenate([q4r, zero], axis=0).astype(BF16)
    qr_hi = jnp.concatenate([zero, q4r], axis=0).astype(BF16)

    def masked(s, bias):
        return jnp.concatenate([s[:, h * Q:(h + 1) * Q] + bias for h in range(Hn)], axis=1)


    cend = lax.broadcasted_iota(jnp.int32, (ng, 1), 0) * CMP_STRIDE + (CMP_BLOCK - 1)
    s = masked(_dot(rm_ref[0], q_lo), jnp.where(cend <= tq, 0.0, NEG_INF))
    eb = jnp.exp2(s - jnp.max(s, axis=0, keepdims=True)).astype(BF16)
    ol = _dot(ct_ref[0], eb)
    seen = _tile_heads((tq >= CMP_BLOCK - 1).astype(F32))
    inv = seen / jnp.maximum(ol[D:D + 1], 1e-30)
    o_c = ol[0:D] * inv
    psum = sum(eb[:, h * Q:(h + 1) * Q] * inv[:, h * Q:(h + 1) * Q] for h in range(Hn))
    imp = _dot(ov_ref[...], psum.astype(BF16))

    WK = WINDOW + Q
    w0 = pl.multiple_of(jnp.maximum(t0 - WINDOW, 0), LANES)
    diff = tq - (w0 + lax.broadcasted_iota(jnp.int32, (WK, 1), 0))
    w_bias = jnp.where((diff >= 0) & (diff < WINDOW), 0.0, NEG_INF)

    def window_head(h):
        s = _dot(kskw_ref[0, pl.ds(w0, WK), :], qr_hi[:, h * Q:(h + 1) * Q]) + w_bias
        eb = jnp.exp2(s - jnp.max(s, axis=0, keepdims=True)).astype(BF16)
        ol = _dot(vt_ref[0, w0 // LANES][VAL_ROWS:2 * VAL_ROWS], eb[0:LANES])
        for u in range(1, WK // LANES):
            ol = ol + _dot(vt_ref[0, w0 // LANES + u][VAL_ROWS:2 * VAL_ROWS], eb[u * LANES:(u + 1) * LANES])
        return ol[0:D] / jnp.maximum(ol[D:D + 1], 1e-30)

    blk = lax.broadcasted_iota(jnp.int32, (ns, Q), 0)
    cur = tq // SEL_BLOCK
    forced = (blk == 0) | (blk == cur) | (blk == cur - 1)
    taken = -2.0
    score = jnp.where(forced, taken, jnp.where(blk <= cur, imp, -1.0))
    picks = max(n_top - 3, 0)
    o_w = []
    for it in range(picks):
        best = jnp.max(score, axis=0, keepdims=True)
        first = jnp.min(jnp.where(score == best, blk, ns), axis=0, keepdims=True)
        score = jnp.where(blk == first, taken, score)
        if it % 3 == 0 and len(o_w) < Hn:
            o_w.append(window_head(len(o_w)))
    while len(o_w) < Hn:
        o_w.append(window_head(len(o_w)))
    o_w = jnp.concatenate(o_w, axis=1)

    def with_block_mask(keep):
        bias = _tile_heads(jnp.where(keep, 0.0, NEG_INF))
        pad = oh_ref.shape[1] - ns
        if pad:
            bias = jnp.concatenate([bias, jnp.zeros((pad, Hn * Q), F32)], axis=0)
        return jnp.concatenate([qr_lo, bias.astype(BF16)], axis=0)

    chosen = (score == taken) & (blk <= cur)
    lk = lax.broadcasted_iota(jnp.int32, (Q, 1), 0)
    lq = lax.broadcasted_iota(jnp.int32, (1, Q), 1)
    own = pl.ds(pl.multiple_of(t0, Q), Q)
    s = masked(_dot(jnp.concatenate([kskw_ref[0, own, :], oh_ref[own, :]], axis=1), with_block_mask(chosen)),
               jnp.where(lk <= lq, 0.0, NEG_INF))
    m0 = jnp.max(s, axis=0, keepdims=True)
    m_ref[...] = m0
    e0 = jnp.exp2(s - m0).astype(BF16)
    QB = Q // MXU_DEPTH
    acc_ref[...] = sum(_dot(vs2_ref[0, i * QB + u], e0[u * MXU_DEPTH:(u + 1) * MXU_DEPTH]) for u in range(QB))

    q_sel = with_block_mask(chosen & (blk < t0 // SEL_BLOCK))

    def sel_scores(j, s_ref, cmax_ref):
        k0 = pl.multiple_of(j * KC, KC)
        keys = jnp.concatenate([kskw_ref[0, pl.ds(k0, KC), :], oh_ref[pl.ds(k0, KC), :]], axis=1)
        s = _dot(keys, q_sel)
        s_ref[...] = s
        cmax_ref[...] = jnp.max(s, axis=0, keepdims=True)

    UB = KC // MXU_DEPTH

    def sel_update(j, s_ref, cmax_ref):
        m_old = m_ref[...]
        m_new = jnp.maximum(m_old, cmax_ref[...])
        eb = jnp.exp2(s_ref[...] - m_new).astype(BF16)
        pv = acc_ref[...] * jnp.exp2(m_old - m_new)
        for u in range(UB):
            pv = pv + _dot(vs2_ref[0, j * UB + u], eb[u * MXU_DEPTH:(u + 1) * MXU_DEPTH])
        acc_ref[...] = pv
        m_ref[...] = m_new

    n_chunks = (t0 + KC - 1) // KC
    last = kskw_ref.shape[1] // KC - 1
    sel_scores(0, sa_ref, ma_ref)

    def chunk_pair(j, carry):
        sel_scores(jnp.minimum(2 * j + 1, last), sb_ref, mb_ref)
        sel_update(2 * j, sa_ref, ma_ref)
        sel_scores(jnp.minimum(2 * j + 2, last), sa_ref, ma_ref)
        sel_update(jnp.minimum(2 * j + 1, last), sb_ref, mb_ref)
        return carry

    lax.fori_loop(0, (n_chunks + 1) // 2, chunk_pair, 0)
    acc = acc_ref[...]
    o_s = acc[0:D] / jnp.maximum(acc[D:D + 1], 1e-30)

    gl = _sigmoid(glog_ref[0].astype(F32) + gb_ref[...])
    ys = []
    for h in range(Hn):
        cs = slice(h * Q, (h + 1) * Q)
        o = (gl[3 * h:3 * h + 1] * o_c[:, cs] + gl[3 * h + 1:3 * h + 2] * o_s[:, cs]
             + gl[3 * h + 2:3 * h + 3] * o_w[:, cs])
        o = o * lax.rsqrt(jnp.mean(o * o, axis=0, keepdims=True) + NORM_EPS)
        ys.append(o)
    y = jnp.concatenate(ys, axis=0) * og_ref[...] * _silu(gate_ref[0].astype(F32))
    y_ref[0] = y.T.astype(y_ref.dtype)


def _nsa_attention(pt, rope, cmp_rm, cmp_t, kskw, vt, vs2, gate_b, out_g):
    B, _, T = pt.shape
    ng = T // CMP_STRIDE
    ns = T // SEL_BLOCK
    Q, W = Q_BLOCK, NSA_WIDTH
    QH = Q * NSA_HEADS
    c0 = np.arange(ng)[None, :] * CMP_STRIDE
    s0 = np.arange(ns)[:, None] * SEL_BLOCK
    ov = np.clip(np.minimum(c0 + CMP_BLOCK, s0 + SEL_BLOCK) - np.maximum(c0, s0), 0, None) / CMP_BLOCK
    ov[:, ng - 1] = 0.0
    oh_lanes = -(-ns // LANES) * LANES
    onehot = (np.arange(T)[:, None] // SEL_BLOCK) == np.arange(oh_lanes)[None, :]
    gb = jnp.zeros((16, 1), F32).at[:NSA_HEADS * N_BRANCH, 0].set(gate_b)
    glog_blk = (4 * HEAD_DIM + 4 * W) // 16
    return pl.pallas_call(
        _nsa_attn_kernel,
        grid=(B, T // Q),
        in_specs=[
            pl.BlockSpec((1, W, Q), lambda b, i: (b, 1, i)),
            pl.BlockSpec((1, W, Q), lambda b, i: (b, 2, i)),
            pl.BlockSpec((1, 16, Q), lambda b, i: (b, glog_blk, i)),
            pl.BlockSpec((2 * ROPE_HALF, Q), lambda b, i: (0, i)),
            pl.BlockSpec((1, ng, LANES), lambda b, i: (b, 0, 0)),
            pl.BlockSpec((1, VAL_ROWS, ng), lambda b, i: (b, 0, 0)),
            pl.BlockSpec((1, T, LANES), lambda b, i: (b, 0, 0)),
            pl.BlockSpec((1, T // LANES, 2 * VAL_ROWS, LANES), lambda b, i: (b, 0, 0, 0)),
            pl.BlockSpec((1, T // MXU_DEPTH, VAL_ROWS, MXU_DEPTH), lambda b, i: (b, 0, 0, 0)),
            pl.BlockSpec((ns, ng), lambda b, i: (0, 0)),
            pl.BlockSpec((T, oh_lanes), lambda b, i: (0, 0)),
            pl.BlockSpec((16, 1), lambda b, i: (0, 0)),
            pl.BlockSpec((W, 1), lambda b, i: (0, 0)),
        ],
        out_specs=pl.BlockSpec((1, Q, W), lambda b, i: (b, i, 0)),
        out_shape=jax.ShapeDtypeStruct((B, T, W), BF16),
        scratch_shapes=[pltpu.VMEM((1, QH), F32), pltpu.VMEM((VAL_ROWS, QH), F32),
                        pltpu.VMEM((SEL_KEY_CHUNK, QH), F32), pltpu.VMEM((SEL_KEY_CHUNK, QH), F32),
                        pltpu.VMEM((1, QH), F32), pltpu.VMEM((1, QH), F32)],
        compiler_params=_params("parallel", "arbitrary"),
        name="nsa_attention",
    )(pt, pt, pt, rope, cmp_rm, cmp_t, kskw, vt, vs2, jnp.asarray(ov, BF16), jnp.asarray(onehot, BF16), gb,
      out_g.reshape(W, 1))


def _mem_kv_kernel(mem_ref, g_ref, w_ref, wt_ref, k_ref, vt_ref):
    x = mem_ref[0]
    h = x * lax.rsqrt(jnp.mean(x * x, axis=-1, keepdims=True) + NORM_EPS) * g_ref[...]
    hb = h.astype(BF16)
    k_ref[0] = _dot(hb, w_ref[...]).astype(BF16)
    vt = _dot_nt(wt_ref[...], hb)
    ones = jnp.ones((ONES_ROWS, vt.shape[1]), F32)
    vt_ref[0] = jnp.concatenate(
        [part for h in range(MEM_HEADS) for part in (vt[h * HEAD_DIM:(h + 1) * HEAD_DIM], ones)],
        axis=0).astype(BF16)


def _mem_kv(mem, g, w_kv):
    B, M, D = mem.shape
    W = MEM_WIDTH
    const = lambda shape: pl.BlockSpec(shape, lambda b: (0,) * len(shape))
    return pl.pallas_call(
        _mem_kv_kernel,
        grid=(B,),
        in_specs=[pl.BlockSpec((1, M, D), lambda b: (b, 0, 0)), const((1, D)), const((D, W)),
                  const((W, D))],
        out_specs=[pl.BlockSpec((1, M, W), lambda b: (b, 0, 0)),
                   pl.BlockSpec((1, MEM_HEADS * VAL_ROWS, M), lambda b: (b, 0, 0))],
        out_shape=[jax.ShapeDtypeStruct((B, M, W), BF16),
                   jax.ShapeDtypeStruct((B, MEM_HEADS * VAL_ROWS, M), BF16)],
        compiler_params=_params("parallel"),
        name="mem_kv",
    )(mem, g.reshape(1, D), w_kv[:, :W].astype(BF16), w_kv[:, W:].T.astype(BF16))


def _mem_attn_kernel(q_ref, gate_ref, k_ref, vt_ref, og_ref, y_ref):
    D, Hm = HEAD_DIM, MEM_HEADS
    q = q_ref[0].astype(F32) * (D ** -0.5 * LOG2E)
    k = k_ref[0]
    vt = vt_ref[0]
    row = lax.broadcasted_iota(jnp.int32, (Hm * D, 1), 0)
    ys = []
    for h in range(Hm):
        qh = jnp.where(row // D == h, q, 0.0).astype(BF16)
        s = _dot(k, qh)
        eb = jnp.exp2(s - jnp.max(s, axis=0, keepdims=True)).astype(BF16)
        ol = _dot(vt[h * VAL_ROWS:(h + 1) * VAL_ROWS], eb)
        o = ol[0:D] / ol[D:D + 1]
        ys.append(o * lax.rsqrt(jnp.mean(o * o, axis=0, keepdims=True) + NORM_EPS))
    y = jnp.concatenate(ys, axis=0) * og_ref[...] * _silu(gate_ref[0].astype(F32))
    y_ref[0] = y.T.astype(y_ref.dtype)


def _mem_attention(pt, mem_k, mem_vt, out_g, tm):
    B, _, T = pt.shape
    M = mem_k.shape[1]
    W = MEM_WIDTH
    return pl.pallas_call(
        _mem_attn_kernel,
        grid=(B, T // tm),
        in_specs=[
            pl.BlockSpec((1, W, tm), lambda b, i: (b, 3, i)),
            pl.BlockSpec((1, W, tm), lambda b, i: (b, 4, i)),
            pl.BlockSpec((1, M, W), lambda b, i: (b, 0, 0)),
            pl.BlockSpec((1, MEM_HEADS * VAL_ROWS, M), lambda b, i: (b, 0, 0)),
            pl.BlockSpec((W, 1), lambda b, i: (0, 0)),
        ],
        out_specs=pl.BlockSpec((1, tm, W), lambda b, i: (b, i, 0)),
        out_shape=jax.ShapeDtypeStruct((B, T, W), BF16),
        compiler_params=_params("parallel", "parallel"),
        name="mem_attention",
    )(pt, pt, mem_k, mem_vt, out_g.reshape(W, 1))


def _outproj_kernel(x_ref, yr_ref, yn_ref, ym_ref, wr_ref, wn_ref, wm_ref, g_ref, o_ref):
    z = (x_ref[0] + _dot(yr_ref[0], wr_ref[...]) + _dot(yn_ref[0], wn_ref[...])
         + _dot(ym_ref[0], wm_ref[...]))
    o_ref[0] = z * lax.rsqrt(jnp.mean(z * z, axis=-1, keepdims=True) + NORM_EPS) * g_ref[...]


def _output_projection(x, y_rwkv, y_nsa, y_mem, w_out, g, tm):
    B, T, D = x.shape
    wb = w_out.astype(BF16)
    w_r, w_n, w_m = wb[:RWKV_WIDTH], wb[RWKV_WIDTH:RWKV_WIDTH + NSA_WIDTH], wb[RWKV_WIDTH + NSA_WIDTH:]
    tile = lambda w: pl.BlockSpec((1, tm, w), lambda b, i: (b, i, 0))
    const = lambda shape: pl.BlockSpec(shape, lambda b, i: (0,) * len(shape))
    return pl.pallas_call(
        _outproj_kernel,
        grid=(B, T // tm),
        in_specs=[tile(D), tile(RWKV_WIDTH), tile(NSA_WIDTH), tile(MEM_WIDTH),
                  const((RWKV_WIDTH, D)), const((NSA_WIDTH, D)), const((MEM_WIDTH, D)), const((1, D))],
        out_specs=tile(D),
        out_shape=jax.ShapeDtypeStruct((B, T, D), F32),
        compiler_params=_params("parallel", "parallel"),
        name="output_projection",
    )(x, y_rwkv, y_nsa, y_mem, w_r, w_n, w_m, g.reshape(1, D))


def _rope_table(T):
    inv_freq = ROPE_THETA ** (-jnp.arange(ROPE_HALF, dtype=F32) / ROPE_HALF)
    ang = inv_freq[:, None] * jnp.arange(T).astype(F32)[None, :]
    return jnp.concatenate([jnp.cos(ang), jnp.sin(ang)], axis=0)


def _split_w_in(w):
    D = HEAD_DIM
    n0 = RWKV_COLS
    q, gate, glog = n0, n0 + NSA_WIDTH, n0 + 2 * NSA_WIDTH
    kc = glog + NSA_HEADS * N_BRANCH
    vc, ks, vs, kw, vw = kc + D, kc + 2 * D, kc + 3 * D, kc + 4 * D, kc + 5 * D
    m0 = vw + D
    cols = lambda a, n: w[:, a:a + n]
    w_row = jnp.concatenate([cols(0, n0), cols(kc, D), cols(vc, D)], axis=1)
    w_t = jnp.concatenate([cols(ks, D), cols(kw, D), cols(vs, D), cols(vw, D), cols(q, NSA_WIDTH),
                           cols(gate, NSA_WIDTH), cols(m0, MEM_WIDTH), cols(m0 + MEM_WIDTH, MEM_WIDTH),
                           cols(glog, NSA_HEADS * N_BRANCH),
                           jnp.zeros((w.shape[0], 16 - NSA_HEADS * N_BRANCH), w.dtype)], axis=1)
    return w_row.astype(BF16), w_t.T.astype(BF16)


def kernel(x, mem, norm_in_g, w_in, rwkv_mu, rwkv_w0, rwkv_w_up, rwkv_a0, rwkv_a_up, rwkv_k_k,
           rwkv_k_a, rwkv_r_k, rwkv_ln_w, rwkv_ln_b, nsa_cmp_pos, nsa_cmp_k_w1, nsa_cmp_k_w2,
           nsa_cmp_v_w1, nsa_cmp_v_w2, nsa_gate_b, nsa_out_g, mem_norm_g, w_mem_kv, mem_out_g, w_out,
           norm_final_g):
    B, T, D = x.shape
    assert w_in.shape[0] == 1, "single-layer stack: the final norm is fused into the output projection"
    rope = _rope_table(T)
    w_row, w_t = _split_w_in(w_in[0])
    p_row, kcvc, pt = _input_projection(x, norm_in_g[0].reshape(1, D), w_row, w_t, tm=512)
    y_rwkv = _rwkv_group(p_row, rwkv_mu[0], rwkv_w0[0], rwkv_w_up[0], rwkv_a0[0], rwkv_a_up[0],
                         rwkv_k_k[0], rwkv_k_a[0], rwkv_r_k[0].reshape(-1), rwkv_ln_w[0],
                         rwkv_ln_b[0])
    kskw, vt, vs2 = _nsa_keys(pt, rope, tk=512)
    cmp_rm, cmp_t = _nsa_compress(kcvc, nsa_cmp_pos[0], nsa_cmp_k_w1[0], nsa_cmp_k_w2[0],
                                  nsa_cmp_v_w1[0], nsa_cmp_v_w2[0])
    y_nsa = _nsa_attention(pt, rope, cmp_rm, cmp_t, kskw, vt, vs2, nsa_gate_b[0], nsa_out_g[0])
    mem_k, mem_vt = _mem_kv(mem, mem_norm_g[0], w_mem_kv[0])
    y_mem = _mem_attention(pt, mem_k, mem_vt, mem_out_g[0], tm=512)
    return _output_projection(x, y_rwkv, y_nsa, y_mem, w_out[0], norm_final_g, tm=512)
```

```python
import numpy as np
import jax
import jax.numpy as jnp
from jax import lax
from jax.experimental import pallas as pl
from jax.experimental.pallas import tpu as pltpu

F32 = jnp.float32
BF16 = jnp.bfloat16

HEAD_DIM = 64
RWKV_HEADS = 8
RWKV_WIDTH = RWKV_HEADS * HEAD_DIM
LORA = 64
RWKV_COLS = 4 * RWKV_WIDTH + 2 * LORA
RWKV_GN_EPS = 64e-5
NSA_HEADS = 4
NSA_WIDTH = NSA_HEADS * HEAD_DIM
N_BRANCH = 3
CMP_BLOCK = 32
CMP_STRIDE = 16
SEL_BLOCK = 64
SEL_TOPK = 16
WINDOW = 512
MEM_HEADS = 4
MEM_WIDTH = MEM_HEADS * HEAD_DIM
ROPE_THETA = 500000.0
ROPE_HALF = 8
Q_BLOCK = 256
NORM_EPS = 1e-6
NEG_INF = -1e30
LOG2E = 1.4426950408889634
DECAY_SCALE = 0.6065306597126334

RWKV_CHUNK = 64
RWKV_HEAD_GROUP = 4
RWKV_BATCH_TILE = 4
SEL_KEY_CHUNK = 512
INPROJ_SUBTILE = 256
LANES = 128
MXU_DEPTH = 256
ONES_ROWS = 16
VAL_ROWS = HEAD_DIM + ONES_ROWS
VMEM_LIMIT = 48 * 1024 * 1024


def _dot(a, b):
    return jnp.dot(a, b, preferred_element_type=F32)


def _dot_nt(a, b):
    return lax.dot_general(a, b, (((1,), (1,)), ((), ())), preferred_element_type=F32)


def _dot_tn(a, b):
    return lax.dot_general(a, b, (((0,), (0,)), ((), ())), preferred_element_type=F32)


def _bf(a):
    return a.astype(BF16)


def _split_bf16(a, pieces):
    out = []
    for _ in range(pieces):
        part = a.astype(BF16)
        out.append(part)
        a = a - part.astype(F32)
    return out


def _sigmoid(x):
    return 1.0 / (1.0 + jnp.exp(-x))


def _silu(x):
    return x * _sigmoid(x)


def _params(*sem):
    return pltpu.CompilerParams(dimension_semantics=sem, vmem_limit_bytes=VMEM_LIMIT)


def _inproj_kernel(x_ref, g_ref, wrow_ref, wt_ref, prow_ref, kc_ref, pt_ref):
    tm = x_ref.shape[1]
    hbs = []
    for r0 in range(0, tm, INPROJ_SUBTILE):
        x = x_ref[0, r0:r0 + INPROJ_SUBTILE, :]
        h = x * lax.rsqrt(jnp.mean(x * x, axis=-1, keepdims=True) + NORM_EPS) * g_ref[...]
        hbs.append(h.astype(BF16))
    for k, hb in enumerate(hbs):
        rows = slice(k * INPROJ_SUBTILE, (k + 1) * INPROJ_SUBTILE)
        row = _dot(hb, wrow_ref[...])
        prow_ref[0, rows, :] = row[:, :RWKV_COLS]
        kc_ref[0, rows, :] = row[:, RWKV_COLS:]
        pt_ref[0, :, rows] = _dot_nt(wt_ref[...], hb).astype(pt_ref.dtype)


def _input_projection(x, g, w_row, w_t, tm):
    B, T, D = x.shape
    n_row = w_row.shape[1]
    n_t = w_t.shape[0]
    return pl.pallas_call(
        _inproj_kernel,
        grid=(B, T // tm),
        in_specs=[
            pl.BlockSpec((1, tm, D), lambda b, i: (b, i, 0)),
            pl.BlockSpec((1, D), lambda b, i: (0, 0)),
            pl.BlockSpec((D, n_row), lambda b, i: (0, 0)),
            pl.BlockSpec((n_t, D), lambda b, i: (0, 0)),
        ],
        out_specs=[
            pl.BlockSpec((1, tm, RWKV_COLS), lambda b, i: (b, i, 0)),
            pl.BlockSpec((1, tm, n_row - RWKV_COLS), lambda b, i: (b, i, 0)),
            pl.BlockSpec((1, n_t, tm), lambda b, i: (b, 0, i)),
        ],
        out_shape=[
            jax.ShapeDtypeStruct((B, T, RWKV_COLS), F32),
            jax.ShapeDtypeStruct((B, T, n_row - RWKV_COLS), F32),
            jax.ShapeDtypeStruct((B, n_t, T), BF16),
        ],
        compiler_params=_params("parallel", "parallel"),
        name="input_projection",
    )(x, g, w_row, w_t)


_X_R, _X_A, _X_K, _X_B, _X_V, _X_BV, _X_SG, _X_G = range(8)


def _rwkv_kernel(podd_ref, peven_ref, prevodd_ref, preveven_ref, mu_ref, w0_ref, wup_ref, a0_ref,
                 aup_ref, kk_ref, ka_ref, rk_ref, lnw_ref, lnb_ref, bd_ref, tri_ref, y_ref,
                 s_ref, xa_ref, xb_ref):
    C, W, N, H = RWKV_CHUNK, RWKV_WIDTH, HEAD_DIM, RWKV_HEADS
    HG = RWKV_HEAD_GROUP
    GW = HG * N
    n = pl.program_id(1)
    n_chunks = 2 * (pl.num_programs(1) - 1)
    bt = podd_ref.shape[0]

    @pl.when(n == 0)
    def _():
        s_ref[...] = jnp.zeros_like(s_ref)
        xa_ref[...] = jnp.zeros_like(xa_ref)
        xb_ref[...] = jnp.zeros_like(xb_ref)

    bd = bd_ref[...]
    head_sum = lambda a: jnp.concatenate(
        [_dot(_bf(a[:, g * GW:(g + 1) * GW]), bd) for g in range(H // HG)], axis=1)
    row = lax.broadcasted_iota(jnp.int32, (C, 1), 0)

    def prepare(b, chunk, p_ref, prev_ref, x_ref):
        p = p_ref[b]
        prev_last = jnp.where(chunk <= 0, 0.0, prev_ref[b][7:8, :])
        prev = jnp.where(row == 0, prev_last, pltpu.roll(p, 1, axis=0))
        pf = p + mu_ref[...] * (prev - p)
        r = pf[:, 0:W]
        k = pf[:, W:2 * W]
        v = pf[:, 2 * W:3 * W]
        gate = pf[:, 3 * W:4 * W]
        wd = _bf(jnp.tanh(pf[:, 4 * W:4 * W + LORA]))
        ad = _bf(pf[:, 4 * W + LORA:4 * W + 2 * LORA])
        yield
        z = w0_ref[...] + _dot(wd, wup_ref[...])
        logw = -DECAY_SCALE * _sigmoid(z)
        eta = _sigmoid(a0_ref[...] + _dot(ad, aup_ref[...]))
        kk = k * kk_ref[...]
        k2 = k * (1.0 + (eta - 1.0) * ka_ref[...])
        kk_sq = _bf(kk * kk)
        rk2 = _bf(r * k2 * rk_ref[...])
        logw_parts = _split_bf16(logw, 3)
        yield
        kk = kk * lax.rsqrt(jnp.maximum(head_sum(kk_sq), 1e-24))
        x_ref[b, _X_BV] = head_sum(rk2) * v
        x_ref[b, _X_SG] = _silu(gate)
        tri = tri_ref[...]
        cs = sum(_dot(tri, part) for part in logw_parts)
        mid = cs[C // 2 - 1:C // 2, :]
        csm = cs - mid
        end = cs[C - 1:C, :]
        x_ref[b, _X_G] = jnp.concatenate([jnp.exp(mid), jnp.exp(end), jnp.exp(end - mid),
                                          jnp.zeros((C - 3, W), F32)], axis=0)
        yield
        e_out = jnp.exp(-csm)
        x_ref[b, _X_R] = r * jnp.exp(csm)
        x_ref[b, _X_A] = -kk * jnp.exp(csm - logw)
        yield
        x_ref[b, _X_K] = k2 * e_out
        x_ref[b, _X_B] = kk * eta * e_out
        x_ref[b, _X_V] = v

    li = lax.broadcasted_iota(jnp.int32, (C, GW), 1) % N
    ti = lax.broadcasted_iota(jnp.int32, (C, GW), 0)
    strict, incl = ti > li, ti >= li
    eye = (ti == li).astype(F32)
    same_head = (lax.broadcasted_iota(jnp.int32, (GW, GW), 0) // N
                 == lax.broadcasted_iota(jnp.int32, (GW, GW), 1) // N)

    def blockdiag(a):
        a = _bf(a)
        return jnp.where(same_head, jnp.concatenate([a] * HG, axis=0), jnp.zeros((), BF16))

    def advance(chunk, x_ref, y_rows, fill):
        live = (chunk >= 0) & (chunk < n_chunks)
        chains = [(b, g) for b in range(bt) for g in range(H // HG)]
        st = {}
        for c in chains:
            b, g = c
            ls = slice(g * GW, (g + 1) * GW)
            kb = jnp.concatenate([blockdiag(x_ref[b, _X_K][:, ls]), blockdiag(x_ref[b, _X_B][:, ls])],
                                 axis=0)
            ar = _bf(jnp.concatenate([x_ref[b, _X_A][:, ls], x_ref[b, _X_R][:, ls]], axis=0))
            gc = _dot_nt(ar, kb)
            s0 = s_ref[b, g]
            gates = x_ref[b, _X_G]
            st[c] = dict(
                ls=ls, kb=kb, s0=s0, v_bd=blockdiag(x_ref[b, _X_V][:, ls]),
                g_end=gates[1:2, ls], g_end_mid=gates[2:3, ls],
                a_ak=jnp.where(strict, gc[0:C, 0:GW], 0.0), a_ab=jnp.where(strict, gc[0:C, GW:2 * GW], 0.0),
                a_rk=jnp.where(incl, gc[C:2 * C, 0:GW], 0.0), a_rb=jnp.where(incl, gc[C:2 * C, GW:2 * GW], 0.0),
                uy0=_dot_nt(ar, _bf(s0 * gates[0:1, ls])))
            fill()
        for c in chains:
            s = st[c]
            s["rhs"] = s["uy0"][0:C] + _dot(_bf(s["a_ak"]), s["v_bd"])
            s["tm"] = eye + s["a_ab"]
            s["pw"] = _dot(_bf(s["a_ab"]), blockdiag(s["a_ab"]))
            fill()
        span = 2
        while span < C:
            for c in chains:
                s = st[c]
                if 2 * span < C:
                    tp = _dot(_bf(jnp.concatenate([s["tm"], s["pw"]], axis=0)), blockdiag(s["pw"]))
                    s["tm"] = s["tm"] + tp[0:C]
                    s["pw"] = tp[C:2 * C]
                else:
                    s["tm"] = s["tm"] + _dot(_bf(s["tm"]), blockdiag(s["pw"]))
                fill()
            span *= 2
        for c in chains:
            s = st[c]
            s["u"] = _dot(_bf(s["tm"]), blockdiag(s["rhs"]))
            fill()
        for c in chains:
            s = st[c]
            s["vu"] = jnp.concatenate([s["v_bd"], blockdiag(s["u"])], axis=0)
            s["y"] = s["uy0"][C:2 * C] + _dot(_bf(jnp.concatenate([s["a_rk"], s["a_rb"]], axis=1)), s["vu"])
            fill()
        for c in chains:
            b, g = c
            s = st[c]
            ls = s["ls"]
            vu_rows = jnp.concatenate([_bf(x_ref[b, _X_V][:, ls]), _bf(s["u"])], axis=0)
            kb_rows = _bf(jnp.concatenate([x_ref[b, _X_K][:, ls], x_ref[b, _X_B][:, ls]], axis=0))
            keep = jnp.where(live, s["g_end"], 1.0)
            gain = jnp.where(live, s["g_end_mid"], 0.0)
            s_ref[b, g] = s["s0"] * keep + jnp.where(same_head, _dot_tn(vu_rows, kb_rows) * gain, 0.0)
            fill()
        for b in range(bt):
            y = jnp.concatenate([st[(b, g)]["y"] for g in range(H // HG)], axis=1)
            mean = head_sum(y) * (1.0 / N)
            var = head_sum(jnp.square(y - mean)) * (1.0 / N)
            y = (y - mean) * lax.rsqrt(var + RWKV_GN_EPS) * lnw_ref[...] + lnb_ref[...] + x_ref[b, _X_BV]
            y_ref[b, y_rows] = (y * x_ref[b, _X_SG]).astype(y_ref.dtype)
            fill()

    def filler(gens):
        gens = list(gens)

        def fill():
            while gens:
                try:
                    next(gens[0])
                    return
                except StopIteration:
                    gens.pop(0)

        def drain():
            while gens:
                fill()

        return fill, drain

    fill, drain = filler(prepare(b, 2 * n - 1, podd_ref, prevodd_ref, xb_ref) for b in range(bt))
    advance(2 * n - 2, xa_ref, slice(0, C), fill)
    drain()
    fill, drain = filler(prepare(b, 2 * n, peven_ref, preveven_ref, xa_ref) for b in range(bt))
    advance(2 * n - 1, xb_ref, slice(C, 2 * C), fill)
    drain()


def _rwkv_group(p_row, mu, w0, w_up, a0, a_up, k_k, k_a, r_k, ln_w, ln_b):
    B, T, _ = p_row.shape
    C, W = RWKV_CHUNK, RWKV_WIDTH
    heads = np.arange(RWKV_HEAD_GROUP * HEAD_DIM) // HEAD_DIM
    bd = jnp.asarray(heads[:, None] == heads[None, :], BF16)
    tri = jnp.asarray(np.tril(np.ones((C, C), np.float32)), BF16)
    vec = lambda a: a.reshape(1, -1)
    const = lambda shape: pl.BlockSpec(shape, lambda b, n: (0,) * len(shape))
    bt = RWKV_BATCH_TILE if B % RWKV_BATCH_TILE == 0 else 1
    n_chunks = T // C
    assert n_chunks % 2 == 0
    odd = lambda n: jnp.maximum(2 * n - 1, 0)
    even = lambda n: jnp.minimum(2 * n, n_chunks - 1)
    before = lambda c: jnp.maximum(c * (C // 8) - 1, 0)
    return pl.pallas_call(
        _rwkv_kernel,
        grid=(B // bt, n_chunks // 2 + 1),
        in_specs=[
            pl.BlockSpec((bt, C, RWKV_COLS), lambda b, n: (b, odd(n), 0)),
            pl.BlockSpec((bt, C, RWKV_COLS), lambda b, n: (b, even(n), 0)),
            pl.BlockSpec((bt, 8, RWKV_COLS), lambda b, n: (b, before(odd(n)), 0)),
            pl.BlockSpec((bt, 8, RWKV_COLS), lambda b, n: (b, before(even(n)), 0)),
            const((1, RWKV_COLS)), const((1, W)), const((LORA, W)), const((1, W)), const((LORA, W)),
            const((1, W)), const((1, W)), const((1, W)), const((1, W)), const((1, W)),
            const(bd.shape), const((C, C)),
        ],
        out_specs=pl.BlockSpec((bt, 2 * C, W), lambda b, n: (b, jnp.maximum(n - 1, 0), 0)),
        out_shape=jax.ShapeDtypeStruct((B, T, W), BF16),
        scratch_shapes=[pltpu.VMEM((bt, RWKV_HEADS // RWKV_HEAD_GROUP) + (RWKV_HEAD_GROUP * HEAD_DIM,) * 2,
                                   F32),
                        pltpu.VMEM((bt, 8, C, W), F32), pltpu.VMEM((bt, 8, C, W), F32)],
        compiler_params=_params("parallel", "arbitrary"),
        name="rwkv7_group",
    )(p_row, p_row, p_row, p_row, vec(mu), vec(w0), _bf(w_up), vec(a0), _bf(a_up), vec(k_k), vec(k_a),
      vec(r_k), vec(ln_w), vec(ln_b), bd, tri)


def _rope_rows(x, cos, sin):
    x1, x2 = x[0:ROPE_HALF], x[ROPE_HALF:2 * ROPE_HALF]
    return jnp.concatenate([x1 * cos - x2 * sin, x2 * cos + x1 * sin, x[2 * ROPE_HALF:]], axis=0)


def _nsa_keys_kernel(kv_ref, rope_ref, kskw_ref, vt_ref, vs2_ref):
    kv = kv_ref[0].astype(F32)
    cos, sin = rope_ref[0:ROPE_HALF], rope_ref[ROPE_HALF:2 * ROPE_HALF]
    keys = jnp.concatenate([_rope_rows(kv[0:HEAD_DIM], cos, sin),
                            _rope_rows(kv[HEAD_DIM:2 * HEAD_DIM], cos, sin)], axis=0)
    kskw_ref[0] = keys.T.astype(BF16)
    vals = kv[2 * HEAD_DIM:4 * HEAD_DIM].astype(BF16)
    ones = jnp.ones((ONES_ROWS, LANES), BF16)
    for u in range(vals.shape[1] // LANES):
        blk = vals[:, u * LANES:(u + 1) * LANES]
        vt_ref[0, u] = jnp.concatenate([blk[0:HEAD_DIM], ones, blk[HEAD_DIM:2 * HEAD_DIM], ones], axis=0)
    ones2 = jnp.ones((ONES_ROWS, MXU_DEPTH), BF16)
    for u in range(vals.shape[1] // MXU_DEPTH):
        vs2_ref[0, u] = jnp.concatenate([vals[0:HEAD_DIM, u * MXU_DEPTH:(u + 1) * MXU_DEPTH], ones2], axis=0)


def _nsa_keys(pt, rope, tk):
    B, _, T = pt.shape
    return pl.pallas_call(
        _nsa_keys_kernel,
        grid=(B, T // tk),
        in_specs=[
            pl.BlockSpec((1, 4 * HEAD_DIM, tk), lambda b, i: (b, 0, i)),
            pl.BlockSpec((2 * ROPE_HALF, tk), lambda b, i: (0, i)),
        ],
        out_specs=[
            pl.BlockSpec((1, tk, LANES), lambda b, i: (b, i, 0)),
            pl.BlockSpec((1, tk // LANES, 2 * VAL_ROWS, LANES), lambda b, i: (b, i, 0, 0)),
            pl.BlockSpec((1, tk // MXU_DEPTH, VAL_ROWS, MXU_DEPTH), lambda b, i: (b, i, 0, 0)),
        ],
        out_shape=[
            jax.ShapeDtypeStruct((B, T, LANES), BF16),
            jax.ShapeDtypeStruct((B, T // LANES, 2 * VAL_ROWS, LANES), BF16),
            jax.ShapeDtypeStruct((B, T // MXU_DEPTH, VAL_ROWS, MXU_DEPTH), BF16),
        ],
        compiler_params=_params("parallel", "parallel"),
        name="nsa_keys",
    )(pt, rope)


def _nsa_compress_kernel(g_ref, wc_ref, pos_ref, w2_ref, w2t_ref, rm_ref, t_ref):
    ng = g_ref.shape[1]
    wc = wc_ref[...]
    m = _dot(g_ref[0].astype(BF16), wc)
    pm = _dot(pos_ref[...], wc)
    pos_term = pm[0:1, 0:LANES] + pm[1:2, LANES:2 * LANES]
    pre = m[:, 0:LANES] + pltpu.roll(m[:, LANES:2 * LANES], ng - 1, axis=0) + pos_term
    act = _silu(pre).astype(BF16)
    row = lax.broadcasted_iota(jnp.int32, (ng, 1), 0)
    col = lax.broadcasted_iota(jnp.int32, (1, ng), 1)
    rm_ref[0] = jnp.where(row < ng - 1, _dot(act, w2_ref[...]), 0.0).astype(BF16)
    vt = jnp.where(col < ng - 1, _dot_nt(w2t_ref[...], act)[HEAD_DIM:2 * HEAD_DIM], 0.0)
    t_ref[0] = jnp.concatenate([vt, jnp.ones((ONES_ROWS, ng), F32)], axis=0).astype(BF16)


def _nsa_compress(kcvc, cmp_pos, k_w1, k_w2, v_w1, v_w2):
    B, T, _ = kcvc.shape
    ng = T // CMP_STRIDE
    half = CMP_BLOCK // 2
    g = kcvc.reshape(B, ng, half * LANES)
    D = HEAD_DIM

    def spread(w, second, is_v):
        blk = w[second * half * D:(second + 1) * half * D].reshape(half, D, D)
        z = jnp.zeros_like(blk)
        return jnp.concatenate([z, blk] if is_v else [blk, z], axis=1).reshape(half * LANES, D)

    wc = jnp.concatenate([spread(k_w1, 0, False), spread(v_w1, 0, True),
                          spread(k_w1, 1, False), spread(v_w1, 1, True)], axis=1).astype(BF16)
    pos2 = jnp.concatenate([cmp_pos, cmp_pos], axis=1)
    pos = jnp.zeros((8, half * LANES), F32)
    pos = pos.at[0].set(pos2[:half].reshape(-1)).at[1].set(pos2[half:].reshape(-1)).astype(BF16)
    z = jnp.zeros((D, D), F32)
    w2 = jnp.block([[k_w2, z], [z, v_w2]])
    const = lambda shape: pl.BlockSpec(shape, lambda b: (0,) * len(shape))
    return pl.pallas_call(
        _nsa_compress_kernel,
        grid=(B,),
        in_specs=[
            pl.BlockSpec((1, ng, half * LANES), lambda b: (b, 0, 0)),
            const((half * LANES, 2 * LANES)), const((8, half * LANES)),
            const((LANES, LANES)), const((LANES, LANES)),
        ],
        out_specs=[
            pl.BlockSpec((1, ng, LANES), lambda b: (b, 0, 0)),
            pl.BlockSpec((1, VAL_ROWS, ng), lambda b: (b, 0, 0)),
        ],
        out_shape=[
            jax.ShapeDtypeStruct((B, ng, LANES), BF16),
            jax.ShapeDtypeStruct((B, VAL_ROWS, ng), BF16),
        ],
        compiler_params=_params("parallel"),
        name="nsa_compress",
    )(g, wc, pos, w2.astype(BF16), w2.T.astype(BF16))


def _tile_heads(x):
    return jnp.concatenate([x] * NSA_HEADS, axis=1)


def _nsa_attn_kernel(q_ref, gate_ref, glog_ref, rope_ref, rm_ref, ct_ref, kskw_ref, vt_ref, vs2_ref,
                     ov_ref, oh_ref, gb_ref, og_ref, y_ref, m_ref, acc_ref, sa_ref, sb_ref, ma_ref,
                     mb_ref):
    D, Q, Hn = HEAD_DIM, Q_BLOCK, NSA_HEADS
    KC = SEL_KEY_CHUNK
    ng = rm_ref.shape[1]
    ns = ov_ref.shape[0]
    n_top = min(SEL_TOPK, ns)
    i = pl.program_id(1)
    t0 = i * Q
    tq = t0 + lax.broadcasted_iota(jnp.int32, (1, Q), 1)

    q = q_ref[0].astype(F32) * (D ** -0.5 * LOG2E)
    cos, sin = rope_ref[0:ROPE_HALF], rope_ref[ROPE_HALF:2 * ROPE_HALF]
    qh = [q[h * D:(h + 1) * D] for h in range(Hn)]
    q4 = jnp.concatenate(qh, axis=1)
    q4r = jnp.concatenate([_rope_rows(x, cos, sin) for x in qh], axis=1)
    zero = jnp.zeros_like(q4)
    q_lo = jnp.concatenate([q4, zero], axis=0).astype(BF16)
    qr_lo = jnp.concatenate([q4r, zero], axis=0).astype(BF16)
    qr_hi = jnp.concatenate([zero, q4r], axis=0).astype(BF16)

    def masked(s, bias):
        return jnp.concatenate([s[:, h * Q:(h + 1) * Q] + bias for h in range(Hn)], axis=1)


    cend = lax.broadcasted_iota(jnp.int32, (ng, 1), 0) * CMP_STRIDE + (CMP_BLOCK - 1)
    s = masked(_dot(rm_ref[0], q_lo), jnp.where(cend <= tq, 0.0, NEG_INF))
    eb = jnp.exp2(s - jnp.max(s, axis=0, keepdims=True)).astype(BF16)
    ol = _dot(ct_ref[0], eb)
    seen = _tile_heads((tq >= CMP_BLOCK - 1).astype(F32))
    inv = seen / jnp.maximum(ol[D:D + 1], 1e-30)
    o_c = ol[0:D] * inv
    psum = sum(eb[:, h * Q:(h + 1) * Q] * inv[:, h * Q:(h + 1) * Q] for h in range(Hn))
    imp = _dot(ov_ref[...], psum.astype(BF16))

    WK = WINDOW + Q
    w0 = pl.multiple_of(jnp.maximum(t0 - WINDOW, 0), LANES)
    diff = tq - (w0 + lax.broadcasted_iota(jnp.int32, (WK, 1), 0))
    w_bias = jnp.where((diff >= 0) & (diff < WINDOW), 0.0, NEG_INF)

    o_w = []

    def window_stages():
        ss = []
        for h in range(Hn):
            ss.append(_dot(kskw_ref[0, pl.ds(w0, WK), :], qr_hi[:, h * Q:(h + 1) * Q]) + w_bias)
            yield
        ebs = []
        for s in ss:
            ebs.append(jnp.exp2(s - jnp.max(s, axis=0, keepdims=True)).astype(BF16))
            yield
        for eb in ebs:
            ol = _dot(vt_ref[0, w0 // LANES][VAL_ROWS:2 * VAL_ROWS], eb[0:LANES])
            for u in range(1, WK // LANES):
                ol = ol + _dot(vt_ref[0, w0 // LANES + u][VAL_ROWS:2 * VAL_ROWS], eb[u * LANES:(u + 1) * LANES])
            o_w.append(ol[0:D] / jnp.maximum(ol[D:D + 1], 1e-30))
            yield

    blk = lax.broadcasted_iota(jnp.int32, (ns, Q), 0)
    cur = tq // SEL_BLOCK
    forced = (blk == 0) | (blk == cur) | (blk == cur - 1)
    taken = -2.0
    score = jnp.where(forced, taken, jnp.where(blk <= cur, imp, -1.0))
    window = window_stages()
    for _ in range(max(n_top - 3, 0)):
        best = jnp.max(score, axis=0, keepdims=True)
        first = jnp.min(jnp.where(score == best, blk, ns), axis=0, keepdims=True)
        score = jnp.where(blk == first, taken, score)
        next(window, None)
    for _ in window:
        pass
    o_w = jnp.concatenate(o_w, axis=1)

    def with_block_mask(keep):
        bias = _tile_heads(jnp.where(keep, 0.0, NEG_INF))
        pad = oh_ref.shape[1] - ns
        if pad:
            bias = jnp.concatenate([bias, jnp.zeros((pad, Hn * Q), F32)], axis=0)
        return jnp.concatenate([qr_lo, bias.astype(BF16)], axis=0)

    chosen = (score == taken) & (blk <= cur)
    q_own = with_block_mask(chosen)
    q_sel = with_block_mask(chosen & (blk < t0 // SEL_BLOCK))

    def sel_scores(j, s_ref, cmax_ref):
        k0 = pl.multiple_of(j * KC, KC)
        keys = jnp.concatenate([kskw_ref[0, pl.ds(k0, KC), :], oh_ref[pl.ds(k0, KC), :]], axis=1)
        s = _dot(keys, q_sel)
        s_ref[...] = s
        cmax_ref[...] = jnp.max(s, axis=0, keepdims=True)

    UB = KC // MXU_DEPTH

    def sel_update(j, s_ref, cmax_ref):
        m_old = m_ref[...]
        m_new = jnp.maximum(m_old, cmax_ref[...])
        eb = jnp.exp2(s_ref[...] - m_new).astype(BF16)
        pv = acc_ref[...] * jnp.exp2(m_old - m_new)
        for u in range(UB):
            pv = pv + _dot(vs2_ref[0, j * UB + u], eb[u * MXU_DEPTH:(u + 1) * MXU_DEPTH])
        acc_ref[...] = pv
        m_ref[...] = m_new

    n_chunks = (t0 + KC - 1) // KC
    last = kskw_ref.shape[1] // KC - 1
    own = pl.ds(pl.multiple_of(t0, Q), Q)
    s = _dot(jnp.concatenate([kskw_ref[0, own, :], oh_ref[own, :]], axis=1), q_own)
    sel_scores(0, sa_ref, ma_ref)
    lk = lax.broadcasted_iota(jnp.int32, (Q, 1), 0)
    lq = lax.broadcasted_iota(jnp.int32, (1, Q), 1)
    s = masked(s, jnp.where(lk <= lq, 0.0, NEG_INF))
    m0 = jnp.max(s, axis=0, keepdims=True)
    m_ref[...] = m0
    e0 = jnp.exp2(s - m0).astype(BF16)
    QB = Q // MXU_DEPTH
    acc_ref[...] = sum(_dot(vs2_ref[0, i * QB + u], e0[u * MXU_DEPTH:(u + 1) * MXU_DEPTH]) for u in range(QB))

    def chunk_pair(j, carry):
        sel_scores(jnp.minimum(2 * j + 1, last), sb_ref, mb_ref)
        sel_update(2 * j, sa_ref, ma_ref)
        sel_scores(jnp.minimum(2 * j + 2, last), sa_ref, ma_ref)
        sel_update(jnp.minimum(2 * j + 1, last), sb_ref, mb_ref)
        return carry

    lax.fori_loop(0, (n_chunks + 1) // 2, chunk_pair, 0)
    acc = acc_ref[...]
    o_s = acc[0:D] / jnp.maximum(acc[D:D + 1], 1e-30)

    gl = _sigmoid(glog_ref[0].astype(F32) + gb_ref[...])
    ys = []
    for h in range(Hn):
        cs = slice(h * Q, (h + 1) * Q)
        o = (gl[3 * h:3 * h + 1] * o_c[:, cs] + gl[3 * h + 1:3 * h + 2] * o_s[:, cs]
             + gl[3 * h + 2:3 * h + 3] * o_w[:, cs])
        o = o * lax.rsqrt(jnp.mean(o * o, axis=0, keepdims=True) + NORM_EPS)
        ys.append(o)
    y = jnp.concatenate(ys, axis=0) * og_ref[...] * _silu(gate_ref[0].astype(F32))
    y_ref[0] = y.T.astype(y_ref.dtype)


def _nsa_attention(pt, rope, cmp_rm, cmp_t, kskw, vt, vs2, gate_b, out_g):
    B, _, T = pt.shape
    ng = T // CMP_STRIDE
    ns = T // SEL_BLOCK
    Q, W = Q_BLOCK, NSA_WIDTH
    QH = Q * NSA_HEADS
    c0 = np.arange(ng)[None, :] * CMP_STRIDE
    s0 = np.arange(ns)[:, None] * SEL_BLOCK
    ov = np.clip(np.minimum(c0 + CMP_BLOCK, s0 + SEL_BLOCK) - np.maximum(c0, s0), 0, None) / CMP_BLOCK
    ov[:, ng - 1] = 0.0
    oh_lanes = -(-ns // LANES) * LANES
    onehot = (np.arange(T)[:, None] // SEL_BLOCK) == np.arange(oh_lanes)[None, :]
    gb = jnp.zeros((16, 1), F32).at[:NSA_HEADS * N_BRANCH, 0].set(gate_b)
    glog_blk = (4 * HEAD_DIM + 4 * W) // 16
    return pl.pallas_call(
        _nsa_attn_kernel,
        grid=(B, T // Q),
        in_specs=[
            pl.BlockSpec((1, W, Q), lambda b, i: (b, 1, i)),
            pl.BlockSpec((1, W, Q), lambda b, i: (b, 2, i)),
            pl.BlockSpec((1, 16, Q), lambda b, i: (b, glog_blk, i)),
            pl.BlockSpec((2 * ROPE_HALF, Q), lambda b, i: (0, i)),
            pl.BlockSpec((1, ng, LANES), lambda b, i: (b, 0, 0)),
            pl.BlockSpec((1, VAL_ROWS, ng), lambda b, i: (b, 0, 0)),
            pl.BlockSpec((1, T, LANES), lambda b, i: (b, 0, 0)),
            pl.BlockSpec((1, T // LANES, 2 * VAL_ROWS, LANES), lambda b, i: (b, 0, 0, 0)),
            pl.BlockSpec((1, T // MXU_DEPTH, VAL_ROWS, MXU_DEPTH), lambda b, i: (b, 0, 0, 0)),
            pl.BlockSpec((ns, ng), lambda b, i: (0, 0)),
            pl.BlockSpec((T, oh_lanes), lambda b, i: (0, 0)),
            pl.BlockSpec((16, 1), lambda b, i: (0, 0)),
            pl.BlockSpec((W, 1), lambda b, i: (0, 0)),
        ],
        out_specs=pl.BlockSpec((1, Q, W), lambda b, i: (b, i, 0)),
        out_shape=jax.ShapeDtypeStruct((B, T, W), BF16),
        scratch_shapes=[pltpu.VMEM((1, QH), F32), pltpu.VMEM((VAL_ROWS, QH), F32),
                        pltpu.VMEM((SEL_KEY_CHUNK, QH), F32), pltpu.VMEM((SEL_KEY_CHUNK, QH), F32),
                        pltpu.VMEM((1, QH), F32), pltpu.VMEM((1, QH), F32)],
        compiler_params=_params("parallel", "arbitrary"),
        name="nsa_attention",
    )(pt, pt, pt, rope, cmp_rm, cmp_t, kskw, vt, vs2, jnp.asarray(ov, BF16), jnp.asarray(onehot, BF16), gb,
      out_g.reshape(W, 1))


def _mem_kv_kernel(mem_ref, g_ref, w_ref, wt_ref, k_ref, vt_ref):
    x = mem_ref[0]
    h = x * lax.rsqrt(jnp.mean(x * x, axis=-1, keepdims=True) + NORM_EPS) * g_ref[...]
    hb = h.astype(BF16)
    k = _dot(hb, w_ref[...]) * (HEAD_DIM ** -0.5 * LOG2E)
    lane_head = lax.broadcasted_iota(jnp.int32, k.shape, 1) // HEAD_DIM
    for h in range(MEM_HEADS):
        k_ref[0, h] = jnp.where(lane_head == h, k, 0.0).astype(BF16)
    vt = _dot_nt(wt_ref[...], hb)
    ones = jnp.ones((ONES_ROWS, vt.shape[1]), F32)
    vt_ref[0] = jnp.concatenate(
        [part for h in range(MEM_HEADS) for part in (vt[h * HEAD_DIM:(h + 1) * HEAD_DIM], ones)],
        axis=0).astype(BF16)


def _mem_kv(mem, g, w_kv):
    B, M, D = mem.shape
    W = MEM_WIDTH
    const = lambda shape: pl.BlockSpec(shape, lambda b: (0,) * len(shape))
    return pl.pallas_call(
        _mem_kv_kernel,
        grid=(B,),
        in_specs=[pl.BlockSpec((1, M, D), lambda b: (b, 0, 0)), const((1, D)), const((D, W)),
                  const((W, D))],
        out_specs=[pl.BlockSpec((1, MEM_HEADS, M, W), lambda b: (b, 0, 0, 0)),
                   pl.BlockSpec((1, MEM_HEADS * VAL_ROWS, M), lambda b: (b, 0, 0))],
        out_shape=[jax.ShapeDtypeStruct((B, MEM_HEADS, M, W), BF16),
                   jax.ShapeDtypeStruct((B, MEM_HEADS * VAL_ROWS, M), BF16)],
        compiler_params=_params("parallel"),
        name="mem_kv",
    )(mem, g.reshape(1, D), w_kv[:, :W].astype(BF16), w_kv[:, W:].T.astype(BF16))


def _mem_attn_kernel(q_ref, gate_ref, k_ref, vt_ref, og_ref, y_ref):
    D, Hm = HEAD_DIM, MEM_HEADS
    q = q_ref[0]
    vt = vt_ref[0]
    ss = [_dot(k_ref[0, h], q) for h in range(Hm)]
    ebs = [jnp.exp2(s - jnp.max(s, axis=0, keepdims=True)).astype(BF16) for s in ss]
    ols = [_dot(vt[h * VAL_ROWS:(h + 1) * VAL_ROWS], eb) for h, eb in enumerate(ebs)]
    ys = []
    for ol in ols:
        o = ol[0:D] / ol[D:D + 1]
        ys.append(o * lax.rsqrt(jnp.mean(o * o, axis=0, keepdims=True) + NORM_EPS))
    y = jnp.concatenate(ys, axis=0) * og_ref[...] * _silu(gate_ref[0].astype(F32))
    y_ref[0] = y.T.astype(y_ref.dtype)


def _mem_attention(pt, mem_k, mem_vt, out_g, tm):
    B, _, T = pt.shape
    M = mem_k.shape[2]
    W = MEM_WIDTH
    return pl.pallas_call(
        _mem_attn_kernel,
        grid=(B, T // tm),
        in_specs=[
            pl.BlockSpec((1, W, tm), lambda b, i: (b, 3, i)),
            pl.BlockSpec((1, W, tm), lambda b, i: (b, 4, i)),
            pl.BlockSpec((1, MEM_HEADS, M, W), lambda b, i: (b, 0, 0, 0)),
            pl.BlockSpec((1, MEM_HEADS * VAL_ROWS, M), lambda b, i: (b, 0, 0)),
            pl.BlockSpec((W, 1), lambda b, i: (0, 0)),
        ],
        out_specs=pl.BlockSpec((1, tm, W), lambda b, i: (b, i, 0)),
        out_shape=jax.ShapeDtypeStruct((B, T, W), BF16),
        compiler_params=_params("parallel", "parallel"),
        name="mem_attention",
    )(pt, pt, mem_k, mem_vt, out_g.reshape(W, 1))


def _outproj_kernel(x_ref, yr_ref, yn_ref, ym_ref, wr_ref, wn_ref, wm_ref, g_ref, o_ref):
    z = (x_ref[0] + _dot(yr_ref[0], wr_ref[...]) + _dot(yn_ref[0], wn_ref[...])
         + _dot(ym_ref[0], wm_ref[...]))
    o_ref[0] = z * lax.rsqrt(jnp.mean(z * z, axis=-1, keepdims=True) + NORM_EPS) * g_ref[...]


def _output_projection(x, y_rwkv, y_nsa, y_mem, w_out, g, tm):
    B, T, D = x.shape
    wb = w_out.astype(BF16)
    w_r, w_n, w_m = wb[:RWKV_WIDTH], wb[RWKV_WIDTH:RWKV_WIDTH + NSA_WIDTH], wb[RWKV_WIDTH + NSA_WIDTH:]
    tile = lambda w: pl.BlockSpec((1, tm, w), lambda b, i: (b, i, 0))
    const = lambda shape: pl.BlockSpec(shape, lambda b, i: (0,) * len(shape))
    return pl.pallas_call(
        _outproj_kernel,
        grid=(B, T // tm),
        in_specs=[tile(D), tile(RWKV_WIDTH), tile(NSA_WIDTH), tile(MEM_WIDTH),
                  const((RWKV_WIDTH, D)), const((NSA_WIDTH, D)), const((MEM_WIDTH, D)), const((1, D))],
        out_specs=tile(D),
        out_shape=jax.ShapeDtypeStruct((B, T, D), F32),
        compiler_params=_params("parallel", "parallel"),
        name="output_projection",
    )(x, y_rwkv, y_nsa, y_mem, w_r, w_n, w_m, g.reshape(1, D))


def _rope_table(T):
    inv_freq = ROPE_THETA ** (-jnp.arange(ROPE_HALF, dtype=F32) / ROPE_HALF)
    ang = inv_freq[:, None] * jnp.arange(T).astype(F32)[None, :]
    return jnp.concatenate([jnp.cos(ang), jnp.sin(ang)], axis=0)


def _split_w_in(w):
    D = HEAD_DIM
    n0 = RWKV_COLS
    q, gate, glog = n0, n0 + NSA_WIDTH, n0 + 2 * NSA_WIDTH
    kc = glog + NSA_HEADS * N_BRANCH
    vc, ks, vs, kw, vw = kc + D, kc + 2 * D, kc + 3 * D, kc + 4 * D, kc + 5 * D
    m0 = vw + D
    cols = lambda a, n: w[:, a:a + n]
    w_row = jnp.concatenate([cols(0, n0), cols(kc, D), cols(vc, D)], axis=1)
    w_t = jnp.concatenate([cols(ks, D), cols(kw, D), cols(vs, D), cols(vw, D), cols(q, NSA_WIDTH),
                           cols(gate, NSA_WIDTH), cols(m0, MEM_WIDTH), cols(m0 + MEM_WIDTH, MEM_WIDTH),
                           cols(glog, NSA_HEADS * N_BRANCH),
                           jnp.zeros((w.shape[0], 16 - NSA_HEADS * N_BRANCH), w.dtype)], axis=1)
    return w_row.astype(BF16), w_t.T.astype(BF16)


def kernel(x, mem, norm_in_g, w_in, rwkv_mu, rwkv_w0, rwkv_w_up, rwkv_a0, rwkv_a_up, rwkv_k_k,
           rwkv_k_a, rwkv_r_k, rwkv_ln_w, rwkv_ln_b, nsa_cmp_pos, nsa_cmp_k_w1, nsa_cmp_k_w2,
           nsa_cmp_v_w1, nsa_cmp_v_w2, nsa_gate_b, nsa_out_g, mem_norm_g, w_mem_kv, mem_out_g, w_out,
           norm_final_g):
    B, T, D = x.shape
    assert w_in.shape[0] == 1, "single-layer stack: the final norm is fused into the output projection"
    rope = _rope_table(T)
    w_row, w_t = _split_w_in(w_in[0])
    p_row, kcvc, pt = _input_projection(x, norm_in_g[0].reshape(1, D), w_row, w_t, tm=512)
    y_rwkv = _rwkv_group(p_row, rwkv_mu[0], rwkv_w0[0], rwkv_w_up[0], rwkv_a0[0], rwkv_a_up[0],
                         rwkv_k_k[0], rwkv_k_a[0], rwkv_r_k[0].reshape(-1), rwkv_ln_w[0],
                         rwkv_ln_b[0])
    kskw, vt, vs2 = _nsa_keys(pt, rope, tk=512)
    cmp_rm, cmp_t = _nsa_compress(kcvc, nsa_cmp_pos[0], nsa_cmp_k_w1[0], nsa_cmp_k_w2[0],
                                  nsa_cmp_v_w1[0], nsa_cmp_v_w2[0])
    y_nsa = _nsa_attention(pt, rope, cmp_rm, cmp_t, kskw, vt, vs2, nsa_gate_b[0], nsa_out_g[0])
    mem_k, mem_vt = _mem_kv(mem, mem_norm_g[0], w_mem_kv[0])
    y_mem = _mem_attention(pt, mem_k, mem_vt, mem_out_g[0], tm=512)
    return _output_projection(x, y_rwkv, y_nsa, y_mem, w_out[0], norm_final_g, tm=512)
```

```python
import numpy as np
import jax
import jax.numpy as jnp
from jax import lax
from jax.experimental import pallas as pl
from jax.experimental.pallas import tpu as pltpu

F32 = jnp.float32
BF16 = jnp.bfloat16

HEAD_DIM = 64
RWKV_HEADS = 8
RWKV_WIDTH = RWKV_HEADS * HEAD_DIM
LORA = 64
RWKV_COLS = 4 * RWKV_WIDTH + 2 * LORA
RWKV_GN_EPS = 64e-5
NSA_HEADS = 4
NSA_WIDTH = NSA_HEADS * HEAD_DIM
N_BRANCH = 3
CMP_BLOCK = 32
CMP_STRIDE = 16
SEL_BLOCK = 64
SEL_TOPK = 16
WINDOW = 512
MEM_HEADS = 4
MEM_WIDTH = MEM_HEADS * HEAD_DIM
ROPE_THETA = 500000.0
ROPE_HALF = 8
Q_BLOCK = 256
NORM_EPS = 1e-6
NEG_INF = -1e30
LOG2E = 1.4426950408889634
DECAY_SCALE = 0.6065306597126334

RWKV_CHUNK = 64
RWKV_HEAD_GROUP = 4
RWKV_BATCH_TILE = 4
SEL_KEY_CHUNK = 512
INPROJ_SUBTILE = 256
LANES = 128
MXU_DEPTH = 256
ONES_ROWS = 16
VAL_ROWS = HEAD_DIM + ONES_ROWS
VMEM_LIMIT = 48 * 1024 * 1024


def _dot(a, b):
    return jnp.dot(a, b, preferred_element_type=F32)


def _dot_nt(a, b):
    return lax.dot_general(a, b, (((1,), (1,)), ((), ())), preferred_element_type=F32)


def _dot_tn(a, b):
    return lax.dot_general(a, b, (((0,), (0,)), ((), ())), preferred_element_type=F32)


def _bf(a):
    return a.astype(BF16)


def _split_bf16(a, pieces):
    out = []
    for _ in range(pieces):
        part = a.astype(BF16)
        out.append(part)
        a = a - part.astype(F32)
    return out


def _sigmoid(x):
    return 1.0 / (1.0 + jnp.exp(-x))


def _silu(x):
    return x * _sigmoid(x)


def _params(*sem):
    return pltpu.CompilerParams(dimension_semantics=sem, vmem_limit_bytes=VMEM_LIMIT)


def _inproj_kernel(x_ref, g_ref, wrow_ref, wt_ref, prow_ref, kc_ref, pt_ref):
    tm = x_ref.shape[1]
    hbs = []
    for r0 in range(0, tm, INPROJ_SUBTILE):
        x = x_ref[0, r0:r0 + INPROJ_SUBTILE, :]
        h = x * lax.rsqrt(jnp.mean(x * x, axis=-1, keepdims=True) + NORM_EPS) * g_ref[...]
        hbs.append(h.astype(BF16))
    for k, hb in enumerate(hbs):
        rows = slice(k * INPROJ_SUBTILE, (k + 1) * INPROJ_SUBTILE)
        row = _dot(hb, wrow_ref[...])
        prow_ref[0, rows, :] = row[:, :RWKV_COLS]
        kc_ref[0, rows, :] = row[:, RWKV_COLS:]
        pt_ref[0, :, rows] = _dot_nt(wt_ref[...], hb).astype(pt_ref.dtype)


def _input_projection(x, g, w_row, w_t, tm):
    B, T, D = x.shape
    n_row = w_row.shape[1]
    n_t = w_t.shape[0]
    return pl.pallas_call(
        _inproj_kernel,
        grid=(B, T // tm),
        in_specs=[
            pl.BlockSpec((1, tm, D), lambda b, i: (b, i, 0)),
            pl.BlockSpec((1, D), lambda b, i: (0, 0)),
            pl.BlockSpec((D, n_row), lambda b, i: (0, 0)),
            pl.BlockSpec((n_t, D), lambda b, i: (0, 0)),
        ],
        out_specs=[
            pl.BlockSpec((1, tm, RWKV_COLS), lambda b, i: (b, i, 0)),
            pl.BlockSpec((1, tm, n_row - RWKV_COLS), lambda b, i: (b, i, 0)),
            pl.BlockSpec((1, n_t, tm), lambda b, i: (b, 0, i)),
        ],
        out_shape=[
            jax.ShapeDtypeStruct((B, T, RWKV_COLS), F32),
            jax.ShapeDtypeStruct((B, T, n_row - RWKV_COLS), F32),
            jax.ShapeDtypeStruct((B, n_t, T), BF16),
        ],
        compiler_params=_params("parallel", "parallel"),
        name="input_projection",
    )(x, g, w_row, w_t)


_X_R, _X_A, _X_K, _X_B, _X_V, _X_BV, _X_SG, _X_G = range(8)


def _rwkv_kernel(podd_ref, peven_ref, prevodd_ref, preveven_ref, mu_ref, w0_ref, wup_ref, a0_ref,
                 aup_ref, kk_ref, ka_ref, rk_ref, lnw_ref, lnb_ref, bd_ref, tri_ref, y_ref,
                 s_ref, xa_ref, xb_ref):
    C, W, N, H = RWKV_CHUNK, RWKV_WIDTH, HEAD_DIM, RWKV_HEADS
    HG = RWKV_HEAD_GROUP
    GW = HG * N
    n = pl.program_id(1)
    n_chunks = 2 * (pl.num_programs(1) - 1)
    bt = podd_ref.shape[0]

    @pl.when(n == 0)
    def _():
        s_ref[...] = jnp.zeros_like(s_ref)
        xa_ref[...] = jnp.zeros_like(xa_ref)
        xb_ref[...] = jnp.zeros_like(xb_ref)

    bd = bd_ref[...]
    head_sum = lambda a: jnp.concatenate(
        [_dot(_bf(a[:, g * GW:(g + 1) * GW]), bd) for g in range(H // HG)], axis=1)
    row = lax.broadcasted_iota(jnp.int32, (C, 1), 0)

    def prepare(b, chunk, p_ref, prev_ref, x_ref):
        p = p_ref[b]
        prev_last = jnp.where(chunk <= 0, 0.0, prev_ref[b][7:8, :])
        prev = jnp.where(row == 0, prev_last, pltpu.roll(p, 1, axis=0))
        pf = p + mu_ref[...] * (prev - p)
        r = pf[:, 0:W]
        k = pf[:, W:2 * W]
        v = pf[:, 2 * W:3 * W]
        gate = pf[:, 3 * W:4 * W]
        wd = _bf(jnp.tanh(pf[:, 4 * W:4 * W + LORA]))
        ad = _bf(pf[:, 4 * W + LORA:4 * W + 2 * LORA])
        yield
        z = w0_ref[...] + _dot(wd, wup_ref[...])
        logw = -DECAY_SCALE * _sigmoid(z)
        eta = _sigmoid(a0_ref[...] + _dot(ad, aup_ref[...]))
        kk = k * kk_ref[...]
        k2 = k * (1.0 + (eta - 1.0) * ka_ref[...])
        kk_sq = _bf(kk * kk)
        rk2 = _bf(r * k2 * rk_ref[...])
        logw_parts = _split_bf16(logw, 3)
        yield
        kk = kk * lax.rsqrt(jnp.maximum(head_sum(kk_sq), 1e-24))
        x_ref[b, _X_BV] = head_sum(rk2) * v
        x_ref[b, _X_SG] = _silu(gate)
        tri = tri_ref[...]
        cs = sum(_dot(tri, part) for part in logw_parts)
        mid = cs[C // 2 - 1:C // 2, :]
        csm = cs - mid
        end = cs[C - 1:C, :]
        x_ref[b, _X_G] = jnp.concatenate([jnp.exp(mid), jnp.exp(end), jnp.exp(end - mid),
                                          jnp.zeros((C - 3, W), F32)], axis=0)
        yield
        e_out = jnp.exp(-csm)
        x_ref[b, _X_R] = r * jnp.exp(csm)
        x_ref[b, _X_A] = -kk * jnp.exp(csm - logw)
        yield
        x_ref[b, _X_K] = k2 * e_out
        x_ref[b, _X_B] = kk * eta * e_out
        x_ref[b, _X_V] = v

    li = lax.broadcasted_iota(jnp.int32, (C, GW), 1) % N
    ti = lax.broadcasted_iota(jnp.int32, (C, GW), 0)
    strict, incl = ti > li, ti >= li
    eye = (ti == li).astype(F32)
    same_head = (lax.broadcasted_iota(jnp.int32, (GW, GW), 0) // N
                 == lax.broadcasted_iota(jnp.int32, (GW, GW), 1) // N)

    def blockdiag(a):
        a = _bf(a)
        return jnp.where(same_head, jnp.concatenate([a] * HG, axis=0), jnp.zeros((), BF16))

    def advance(chunk, x_ref, y_rows, fill):
        live = (chunk >= 0) & (chunk < n_chunks)
        chains = [(b, g) for b in range(bt) for g in range(H // HG)]
        st = {}
        for c in chains:
            b, g = c
            ls = slice(g * GW, (g + 1) * GW)
            kb = jnp.concatenate([blockdiag(x_ref[b, _X_K][:, ls]), blockdiag(x_ref[b, _X_B][:, ls])],
                                 axis=0)
            ar = _bf(jnp.concatenate([x_ref[b, _X_A][:, ls], x_ref[b, _X_R][:, ls]], axis=0))
            st[c] = dict(ls=ls, kb=kb, ar=ar, gc=_dot_nt(ar, kb))
            fill()
        for c in chains:
            b, g = c
            s = st[c]
            ls, gc = s["ls"], s["gc"]
            s0 = s_ref[b, g]
            gates = x_ref[b, _X_G]
            s.update(
                s0=s0, v_bd=blockdiag(x_ref[b, _X_V][:, ls]),
                g_end=gates[1:2, ls], g_end_mid=gates[2:3, ls],
                uy0=_dot_nt(s["ar"], _bf(s0 * gates[0:1, ls])),
                a_ak=jnp.where(strict, gc[0:C, 0:GW], 0.0), a_ab=jnp.where(strict, gc[0:C, GW:2 * GW], 0.0),
                a_rk=jnp.where(incl, gc[C:2 * C, 0:GW], 0.0), a_rb=jnp.where(incl, gc[C:2 * C, GW:2 * GW], 0.0))
            fill()
        for c in chains:
            s = st[c]
            s["rhs"] = s["uy0"][0:C] + _dot(_bf(s["a_ak"]), s["v_bd"])
            s["tm"] = eye + s["a_ab"]
            s["pw"] = _dot(_bf(s["a_ab"]), blockdiag(s["a_ab"]))
            fill()
        span = 2
        while span < C:
            for c in chains:
                s = st[c]
                if 2 * span < C:
                    tp = _dot(_bf(jnp.concatenate([s["tm"], s["pw"]], axis=0)), blockdiag(s["pw"]))
                    s["tm"] = s["tm"] + tp[0:C]
                    s["pw"] = tp[C:2 * C]
                else:
                    s["tm"] = s["tm"] + _dot(_bf(s["tm"]), blockdiag(s["pw"]))
                fill()
            span *= 2
        for c in chains:
            s = st[c]
            s["u"] = _dot(_bf(s["tm"]), blockdiag(s["rhs"]))
            fill()
        for c in chains:
            s = st[c]
            s["vu"] = jnp.concatenate([s["v_bd"], blockdiag(s["u"])], axis=0)
            s["y"] = s["uy0"][C:2 * C] + _dot(_bf(jnp.concatenate([s["a_rk"], s["a_rb"]], axis=1)), s["vu"])
            fill()
        for c in chains:
            b, g = c
            s = st[c]
            ls = s["ls"]
            vu_rows = jnp.concatenate([_bf(x_ref[b, _X_V][:, ls]), _bf(s["u"])], axis=0)
            kb_rows = _bf(jnp.concatenate([x_ref[b, _X_K][:, ls], x_ref[b, _X_B][:, ls]], axis=0))
            keep = jnp.where(live, s["g_end"], 1.0)
            gain = jnp.where(live, s["g_end_mid"], 0.0)
            s_ref[b, g] = s["s0"] * keep + jnp.where(same_head, _dot_tn(vu_rows, kb_rows) * gain, 0.0)
            fill()
        ys = [jnp.concatenate([st[(b, g)]["y"] for g in range(H // HG)], axis=1) for b in range(bt)]
        means = [head_sum(y) * (1.0 / N) for y in ys]
        fill()
        ys = [y - mean for y, mean in zip(ys, means)]
        vars_ = [head_sum(jnp.square(y)) * (1.0 / N) for y in ys]
        fill()
        for b, (y, var) in enumerate(zip(ys, vars_)):
            y = y * lax.rsqrt(var + RWKV_GN_EPS) * lnw_ref[...] + lnb_ref[...] + x_ref[b, _X_BV]
            y_ref[b, y_rows] = (y * x_ref[b, _X_SG]).astype(y_ref.dtype)
            fill()

    def filler(gens):
        gens = list(gens)

        def fill():
            while gens:
                g = gens.pop(0)
                try:
                    next(g)
                    gens.append(g)
                    return
                except StopIteration:
                    pass

        def drain():
            for g in gens:
                for _ in g:
                    pass

        return fill, drain

    fill, drain = filler(prepare(b, 2 * n - 1, podd_ref, prevodd_ref, xb_ref) for b in range(bt))
    advance(2 * n - 2, xa_ref, slice(0, C), fill)
    drain()
    fill, drain = filler(prepare(b, 2 * n, peven_ref, preveven_ref, xa_ref) for b in range(bt))
    advance(2 * n - 1, xb_ref, slice(C, 2 * C), fill)
    drain()


def _rwkv_group(p_row, mu, w0, w_up, a0, a_up, k_k, k_a, r_k, ln_w, ln_b):
    B, T, _ = p_row.shape
    C, W = RWKV_CHUNK, RWKV_WIDTH
    heads = np.arange(RWKV_HEAD_GROUP * HEAD_DIM) // HEAD_DIM
    bd = jnp.asarray(heads[:, None] == heads[None, :], BF16)
    tri = jnp.asarray(np.tril(np.ones((C, C), np.float32)), BF16)
    vec = lambda a: a.reshape(1, -1)
    const = lambda shape: pl.BlockSpec(shape, lambda b, n: (0,) * len(shape))
    bt = RWKV_BATCH_TILE if B % RWKV_BATCH_TILE == 0 else 1
    n_chunks = T // C
    assert n_chunks % 2 == 0
    odd = lambda n: jnp.maximum(2 * n - 1, 0)
    even = lambda n: jnp.minimum(2 * n, n_chunks - 1)
    before = lambda c: jnp.maximum(c * (C // 8) - 1, 0)
    return pl.pallas_call(
        _rwkv_kernel,
        grid=(B // bt, n_chunks // 2 + 1),
        in_specs=[
            pl.BlockSpec((bt, C, RWKV_COLS), lambda b, n: (b, odd(n), 0)),
            pl.BlockSpec((bt, C, RWKV_COLS), lambda b, n: (b, even(n), 0)),
            pl.BlockSpec((bt, 8, RWKV_COLS), lambda b, n: (b, before(odd(n)), 0)),
            pl.BlockSpec((bt, 8, RWKV_COLS), lambda b, n: (b, before(even(n)), 0)),
            const((1, RWKV_COLS)), const((1, W)), const((LORA, W)), const((1, W)), const((LORA, W)),
            const((1, W)), const((1, W)), const((1, W)), const((1, W)), const((1, W)),
            const(bd.shape), const((C, C)),
        ],
        out_specs=pl.BlockSpec((bt, 2 * C, W), lambda b, n: (b, jnp.maximum(n - 1, 0), 0)),
        out_shape=jax.ShapeDtypeStruct((B, T, W), BF16),
        scratch_shapes=[pltpu.VMEM((bt, RWKV_HEADS // RWKV_HEAD_GROUP) + (RWKV_HEAD_GROUP * HEAD_DIM,) * 2,
                                   F32),
                        pltpu.VMEM((bt, 8, C, W), F32), pltpu.VMEM((bt, 8, C, W), F32)],
        compiler_params=_params("parallel", "arbitrary"),
        name="rwkv7_group",
    )(p_row, p_row, p_row, p_row, vec(mu), vec(w0), _bf(w_up), vec(a0), _bf(a_up), vec(k_k), vec(k_a),
      vec(r_k), vec(ln_w), vec(ln_b), bd, tri)


def _rope_rows(x, cos, sin):
    x1, x2 = x[0:ROPE_HALF], x[ROPE_HALF:2 * ROPE_HALF]
    return jnp.concatenate([x1 * cos - x2 * sin, x2 * cos + x1 * sin, x[2 * ROPE_HALF:]], axis=0)


def _nsa_keys_kernel(kv_ref, rope_ref, kskw_ref, vt_ref, vs2_ref):
    kv = kv_ref[0].astype(F32)
    cos, sin = rope_ref[0:ROPE_HALF], rope_ref[ROPE_HALF:2 * ROPE_HALF]
    keys = jnp.concatenate([_rope_rows(kv[0:HEAD_DIM], cos, sin),
                            _rope_rows(kv[HEAD_DIM:2 * HEAD_DIM], cos, sin)], axis=0)
    kskw_ref[0] = keys.T.astype(BF16)
    vals = kv[2 * HEAD_DIM:4 * HEAD_DIM].astype(BF16)
    ones = jnp.ones((ONES_ROWS, LANES), BF16)
    for u in range(vals.shape[1] // LANES):
        blk = vals[:, u * LANES:(u + 1) * LANES]
        vt_ref[0, u] = jnp.concatenate([blk[0:HEAD_DIM], ones, blk[HEAD_DIM:2 * HEAD_DIM], ones], axis=0)
    ones2 = jnp.ones((ONES_ROWS, MXU_DEPTH), BF16)
    for u in range(vals.shape[1] // MXU_DEPTH):
        vs2_ref[0, u] = jnp.concatenate([vals[0:HEAD_DIM, u * MXU_DEPTH:(u + 1) * MXU_DEPTH], ones2], axis=0)


def _nsa_keys(pt, rope, tk):
    B, _, T = pt.shape
    return pl.pallas_call(
        _nsa_keys_kernel,
        grid=(B, T // tk),
        in_specs=[
            pl.BlockSpec((1, 4 * HEAD_DIM, tk), lambda b, i: (b, 0, i)),
            pl.BlockSpec((2 * ROPE_HALF, tk), lambda b, i: (0, i)),
        ],
        out_specs=[
            pl.BlockSpec((1, tk, LANES), lambda b, i: (b, i, 0)),
            pl.BlockSpec((1, tk // LANES, 2 * VAL_ROWS, LANES), lambda b, i: (b, i, 0, 0)),
            pl.BlockSpec((1, tk // MXU_DEPTH, VAL_ROWS, MXU_DEPTH), lambda b, i: (b, i, 0, 0)),
        ],
        out_shape=[
            jax.ShapeDtypeStruct((B, T, LANES), BF16),
            jax.ShapeDtypeStruct((B, T // LANES, 2 * VAL_ROWS, LANES), BF16),
            jax.ShapeDtypeStruct((B, T // MXU_DEPTH, VAL_ROWS, MXU_DEPTH), BF16),
        ],
        compiler_params=_params("parallel", "parallel"),
        name="nsa_keys",
    )(pt, rope)


def _nsa_compress_kernel(g_ref, wc_ref, pos_ref, w2_ref, w2t_ref, rm_ref, t_ref):
    half = CMP_BLOCK // 2
    ng = g_ref.shape[1] // half
    wc = wc_ref[...]
    m = sum(_dot(g_ref[0, pl.ds(l, ng, stride=half), :].astype(BF16), wc[l * LANES:(l + 1) * LANES])
            for l in range(half))
    pm = _dot(pos_ref[...], wc)
    pos_term = pm[0:1, 0:LANES] + pm[1:2, LANES:2 * LANES]
    pre = m[:, 0:LANES] + pltpu.roll(m[:, LANES:2 * LANES], ng - 1, axis=0) + pos_term
    act = _silu(pre).astype(BF16)
    row = lax.broadcasted_iota(jnp.int32, (ng, 1), 0)
    col = lax.broadcasted_iota(jnp.int32, (1, ng), 1)
    rm_ref[0] = jnp.where(row < ng - 1, _dot(act, w2_ref[...]), 0.0).astype(BF16)
    vt = jnp.where(col < ng - 1, _dot_nt(w2t_ref[...], act)[HEAD_DIM:2 * HEAD_DIM], 0.0)
    t_ref[0] = jnp.concatenate([vt, jnp.ones((ONES_ROWS, ng), F32)], axis=0).astype(BF16)


def _nsa_compress(kcvc, cmp_pos, k_w1, k_w2, v_w1, v_w2):
    B, T, _ = kcvc.shape
    ng = T // CMP_STRIDE
    half = CMP_BLOCK // 2
    D = HEAD_DIM

    def spread(w, second, is_v):
        blk = w[second * half * D:(second + 1) * half * D].reshape(half, D, D)
        z = jnp.zeros_like(blk)
        return jnp.concatenate([z, blk] if is_v else [blk, z], axis=1).reshape(half * LANES, D)

    wc = jnp.concatenate([spread(k_w1, 0, False), spread(v_w1, 0, True),
                          spread(k_w1, 1, False), spread(v_w1, 1, True)], axis=1).astype(BF16)
    pos2 = jnp.concatenate([cmp_pos, cmp_pos], axis=1)
    pos = jnp.zeros((8, half * LANES), F32)
    pos = pos.at[0].set(pos2[:half].reshape(-1)).at[1].set(pos2[half:].reshape(-1)).astype(BF16)
    z = jnp.zeros((D, D), F32)
    w2 = jnp.block([[k_w2, z], [z, v_w2]])
    const = lambda shape: pl.BlockSpec(shape, lambda b: (0,) * len(shape))
    return pl.pallas_call(
        _nsa_compress_kernel,
        grid=(B,),
        in_specs=[
            pl.BlockSpec((1, T, LANES), lambda b: (b, 0, 0)),
            const((half * LANES, 2 * LANES)), const((8, half * LANES)),
            const((LANES, LANES)), const((LANES, LANES)),
        ],
        out_specs=[
            pl.BlockSpec((1, ng, LANES), lambda b: (b, 0, 0)),
            pl.BlockSpec((1, VAL_ROWS, ng), lambda b: (b, 0, 0)),
        ],
        out_shape=[
            jax.ShapeDtypeStruct((B, ng, LANES), BF16),
            jax.ShapeDtypeStruct((B, VAL_ROWS, ng), BF16),
        ],
        compiler_params=_params("parallel"),
        name="nsa_compress",
    )(kcvc, wc, pos, w2.astype(BF16), w2.T.astype(BF16))


def _tile_heads(x):
    return jnp.concatenate([x] * NSA_HEADS, axis=1)


def _nsa_attn_kernel(q_ref, gate_ref, glog_ref, rope_ref, rm_ref, ct_ref, kskw_ref, vt_ref, vs2_ref,
                     ov_ref, oh_ref, gb_ref, og_ref, y_ref, m_ref, acc_ref, sa_ref, sb_ref, ma_ref,
                     mb_ref):
    D, Q, Hn = HEAD_DIM, Q_BLOCK, NSA_HEADS
    KC = SEL_KEY_CHUNK
    ng = rm_ref.shape[1]
    ns = ov_ref.shape[0]
    n_top = min(SEL_TOPK, ns)
    i = pl.program_id(1)
    t0 = i * Q
    tq = t0 + lax.broadcasted_iota(jnp.int32, (1, Q), 1)

    q = q_ref[0].astype(F32) * (D ** -0.5 * LOG2E)
    cos, sin = rope_ref[0:ROPE_HALF], rope_ref[ROPE_HALF:2 * ROPE_HALF]
    qh = [q[h * D:(h + 1) * D] for h in range(Hn)]
    q4 = jnp.concatenate(qh, axis=1)
    q4r = jnp.concatenate([_rope_rows(x, cos, sin) for x in qh], axis=1)
    zero = jnp.zeros_like(q4)
    q_lo = jnp.concatenate([q4, zero], axis=0).astype(BF16)
    qr_lo = jnp.concatenate([q4r, zero], axis=0).astype(BF16)
    qr_hi = jnp.concatenate([zero, q4r], axis=0).astype(BF16)

    def masked(s, bias):
        return jnp.concatenate([s[:, h * Q:(h + 1) * Q] + bias for h in range(Hn)], axis=1)


    cend = lax.broadcasted_iota(jnp.int32, (ng, 1), 0) * CMP_STRIDE + (CMP_BLOCK - 1)
    s = masked(_dot(rm_ref[0], q_lo), jnp.where(cend <= tq, 0.0, NEG_INF))
    eb = jnp.exp2(s - jnp.max(s, axis=0, keepdims=True)).astype(BF16)
    ol = _dot(ct_ref[0], eb)
    seen = _tile_heads((tq >= CMP_BLOCK - 1).astype(F32))
    inv = seen / jnp.maximum(ol[D:D + 1], 1e-30)
    o_c = ol[0:D] * inv
    psum = sum(eb[:, h * Q:(h + 1) * Q] * inv[:, h * Q:(h + 1) * Q] for h in range(Hn))
    imp = _dot(ov_ref[...], psum.astype(BF16))

    WK = WINDOW + Q
    w0 = pl.multiple_of(jnp.maximum(t0 - WINDOW, 0), LANES)
    diff = tq - (w0 + lax.broadcasted_iota(jnp.int32, (WK, 1), 0))
    w_bias = jnp.where((diff >= 0) & (diff < WINDOW), 0.0, NEG_INF)

    o_w = []

    def window_stages():
        ss = []
        for h in range(Hn):
            ss.append(_dot(kskw_ref[0, pl.ds(w0, WK), :], qr_hi[:, h * Q:(h + 1) * Q]) + w_bias)
            yield
        ebs = []
        for s in ss:
            ebs.append(jnp.exp2(s - jnp.max(s, axis=0, keepdims=True)).astype(BF16))
            yield
        for eb in ebs:
            ol = _dot(vt_ref[0, w0 // LANES][VAL_ROWS:2 * VAL_ROWS], eb[0:LANES])
            for u in range(1, WK // LANES):
                ol = ol + _dot(vt_ref[0, w0 // LANES + u][VAL_ROWS:2 * VAL_ROWS], eb[u * LANES:(u + 1) * LANES])
            o_w.append(ol[0:D] / jnp.maximum(ol[D:D + 1], 1e-30))
            yield

    blk = lax.broadcasted_iota(jnp.int32, (ns, Q), 0)
    cur = tq // SEL_BLOCK
    forced = (blk == 0) | (blk == cur) | (blk == cur - 1)
    taken = -2.0
    score = jnp.where(forced, taken, jnp.where(blk <= cur, imp, -1.0))
    window = window_stages()
    for _ in range(max(n_top - 3, 0)):
        best = jnp.max(score, axis=0, keepdims=True)
        first = jnp.min(jnp.where(score == best, blk, ns), axis=0, keepdims=True)
        score = jnp.where(blk == first, taken, score)
        next(window, None)
    for _ in window:
        pass
    o_w = jnp.concatenate(o_w, axis=1)

    def with_block_mask(keep):
        bias = _tile_heads(jnp.where(keep, 0.0, NEG_INF))
        pad = oh_ref.shape[1] - ns
        if pad:
            bias = jnp.concatenate([bias, jnp.zeros((pad, Hn * Q), F32)], axis=0)
        return jnp.concatenate([qr_lo, bias.astype(BF16)], axis=0)

    chosen = (score == taken) & (blk <= cur)
    q_own = with_block_mask(chosen)
    q_sel = with_block_mask(chosen & (blk < t0 // SEL_BLOCK))

    def sel_scores(j, s_ref, cmax_ref):
        k0 = pl.multiple_of(j * KC, KC)
        keys = jnp.concatenate([kskw_ref[0, pl.ds(k0, KC), :], oh_ref[pl.ds(k0, KC), :]], axis=1)
        s = _dot(keys, q_sel)
        s_ref[...] = s
        cmax_ref[...] = jnp.max(s, axis=0, keepdims=True)

    UB = KC // MXU_DEPTH

    def sel_update(j, s_ref, cmax_ref):
        m_old = m_ref[...]
        m_new = jnp.maximum(m_old, cmax_ref[...])
        eb = jnp.exp2(s_ref[...] - m_new).astype(BF16)
        pv = acc_ref[...] * jnp.exp2(m_old - m_new)
        for u in range(UB):
            pv = pv + _dot(vs2_ref[0, j * UB + u], eb[u * MXU_DEPTH:(u + 1) * MXU_DEPTH])
        acc_ref[...] = pv
        m_ref[...] = m_new

    n_chunks = (t0 + KC - 1) // KC
    last = kskw_ref.shape[1] // KC - 1
    own = pl.ds(pl.multiple_of(t0, Q), Q)
    s = _dot(jnp.concatenate([kskw_ref[0, own, :], oh_ref[own, :]], axis=1), q_own)
    sel_scores(0, sa_ref, ma_ref)
    lk = lax.broadcasted_iota(jnp.int32, (Q, 1), 0)
    lq = lax.broadcasted_iota(jnp.int32, (1, Q), 1)
    s = masked(s, jnp.where(lk <= lq, 0.0, NEG_INF))
    m0 = jnp.max(s, axis=0, keepdims=True)
    m_ref[...] = m0
    e0 = jnp.exp2(s - m0).astype(BF16)
    QB = Q // MXU_DEPTH
    acc_ref[...] = sum(_dot(vs2_ref[0, i * QB + u], e0[u * MXU_DEPTH:(u + 1) * MXU_DEPTH]) for u in range(QB))

    def chunk_pair(j, carry):
        sel_scores(jnp.minimum(2 * j + 1, last), sb_ref, mb_ref)
        sel_update(2 * j, sa_ref, ma_ref)
        sel_scores(jnp.minimum(2 * j + 2, last), sa_ref, ma_ref)
        sel_update(jnp.minimum(2 * j + 1, last), sb_ref, mb_ref)
        return carry

    lax.fori_loop(0, (n_chunks + 1) // 2, chunk_pair, 0)
    acc = acc_ref[...]
    o_s = acc[0:D] / jnp.maximum(acc[D:D + 1], 1e-30)

    gl = _sigmoid(glog_ref[0].astype(F32) + gb_ref[...])
    ys = []
    for h in range(Hn):
        cs = slice(h * Q, (h + 1) * Q)
        o = (gl[3 * h:3 * h + 1] * o_c[:, cs] + gl[3 * h + 1:3 * h + 2] * o_s[:, cs]
             + gl[3 * h + 2:3 * h + 3] * o_w[:, cs])
        o = o * lax.rsqrt(jnp.mean(o * o, axis=0, keepdims=True) + NORM_EPS)
        ys.append(o)
    y = jnp.concatenate(ys, axis=0) * og_ref[...] * _silu(gate_ref[0].astype(F32))
    y_ref[0] = y.T.astype(y_ref.dtype)


def _nsa_attention(pt, rope, cmp_rm, cmp_t, kskw, vt, vs2, gate_b, out_g):
    B, _, T = pt.shape
    ng = T // CMP_STRIDE
    ns = T // SEL_BLOCK
    Q, W = Q_BLOCK, NSA_WIDTH
    QH = Q * NSA_HEADS
    c0 = np.arange(ng)[None, :] * CMP_STRIDE
    s0 = np.arange(ns)[:, None] * SEL_BLOCK
    ov = np.clip(np.minimum(c0 + CMP_BLOCK, s0 + SEL_BLOCK) - np.maximum(c0, s0), 0, None) / CMP_BLOCK
    ov[:, ng - 1] = 0.0
    oh_lanes = -(-ns // LANES) * LANES
    onehot = (np.arange(T)[:, None] // SEL_BLOCK) == np.arange(oh_lanes)[None, :]
    gb = jnp.zeros((16, 1), F32).at[:NSA_HEADS * N_BRANCH, 0].set(gate_b)
    glog_blk = (4 * HEAD_DIM + 4 * W) // 16
    return pl.pallas_call(
        _nsa_attn_kernel,
        grid=(B, T // Q),
        in_specs=[
            pl.BlockSpec((1, W, Q), lambda b, i: (b, 1, i)),
            pl.BlockSpec((1, W, Q), lambda b, i: (b, 2, i)),
            pl.BlockSpec((1, 16, Q), lambda b, i: (b, glog_blk, i)),
            pl.BlockSpec((2 * ROPE_HALF, Q), lambda b, i: (0, i)),
            pl.BlockSpec((1, ng, LANES), lambda b, i: (b, 0, 0)),
            pl.BlockSpec((1, VAL_ROWS, ng), lambda b, i: (b, 0, 0)),
            pl.BlockSpec((1, T, LANES), lambda b, i: (b, 0, 0)),
            pl.BlockSpec((1, T // LANES, 2 * VAL_ROWS, LANES), lambda b, i: (b, 0, 0, 0)),
            pl.BlockSpec((1, T // MXU_DEPTH, VAL_ROWS, MXU_DEPTH), lambda b, i: (b, 0, 0, 0)),
            pl.BlockSpec((ns, ng), lambda b, i: (0, 0)),
            pl.BlockSpec((T, oh_lanes), lambda b, i: (0, 0)),
            pl.BlockSpec((16, 1), lambda b, i: (0, 0)),
            pl.BlockSpec((W, 1), lambda b, i: (0, 0)),
        ],
        out_specs=pl.BlockSpec((1, Q, W), lambda b, i: (b, i, 0)),
        out_shape=jax.ShapeDtypeStruct((B, T, W), BF16),
        scratch_shapes=[pltpu.VMEM((1, QH), F32), pltpu.VMEM((VAL_ROWS, QH), F32),
                        pltpu.VMEM((SEL_KEY_CHUNK, QH), F32), pltpu.VMEM((SEL_KEY_CHUNK, QH), F32),
                        pltpu.VMEM((1, QH), F32), pltpu.VMEM((1, QH), F32)],
        compiler_params=_params("parallel", "arbitrary"),
        name="nsa_attention",
    )(pt, pt, pt, rope, cmp_rm, cmp_t, kskw, vt, vs2, jnp.asarray(ov, BF16), jnp.asarray(onehot, BF16), gb,
      out_g.reshape(W, 1))


def _mem_kv_kernel(mem_ref, g_ref, w_ref, wt_ref, k_ref, vt_ref):
    x = mem_ref[0]
    h = x * lax.rsqrt(jnp.mean(x * x, axis=-1, keepdims=True) + NORM_EPS) * g_ref[...]
    hb = h.astype(BF16)
    k = _dot(hb, w_ref[...]) * (HEAD_DIM ** -0.5 * LOG2E)
    lane_head = lax.broadcasted_iota(jnp.int32, k.shape, 1) // HEAD_DIM
    for h in range(MEM_HEADS):
        k_ref[0, h] = jnp.where(lane_head == h, k, 0.0).astype(BF16)
    vt = _dot_nt(wt_ref[...], hb)
    ones = jnp.ones((ONES_ROWS, vt.shape[1]), F32)
    vt_ref[0] = jnp.concatenate(
        [part for h in range(MEM_HEADS) for part in (vt[h * HEAD_DIM:(h + 1) * HEAD_DIM], ones)],
        axis=0).astype(BF16)


def _mem_kv(mem, g, w_kv):
    B, M, D = mem.shape
    W = MEM_WIDTH
    const = lambda shape: pl.BlockSpec(shape, lambda b: (0,) * len(shape))
    return pl.pallas_call(
        _mem_kv_kernel,
        grid=(B,),
        in_specs=[pl.BlockSpec((1, M, D), lambda b: (b, 0, 0)), const((1, D)), const((D, W)),
                  const((W, D))],
        out_specs=[pl.BlockSpec((1, MEM_HEADS, M, W), lambda b: (b, 0, 0, 0)),
                   pl.BlockSpec((1, MEM_HEADS * VAL_ROWS, M), lambda b: (b, 0, 0))],
        out_shape=[jax.ShapeDtypeStruct((B, MEM_HEADS, M, W), BF16),
                   jax.ShapeDtypeStruct((B, MEM_HEADS * VAL_ROWS, M), BF16)],
        compiler_params=_params("parallel"),
        name="mem_kv",
    )(mem, g.reshape(1, D), w_kv[:, :W].astype(BF16), w_kv[:, W:].T.astype(BF16))


def _mem_attn_kernel(q_ref, gate_ref, k_ref, vt_ref, og_ref, y_ref):
    D, Hm = HEAD_DIM, MEM_HEADS
    q = q_ref[0]
    vt = vt_ref[0]
    ss = [_dot(k_ref[0, h], q) for h in range(Hm)]
    ebs = [jnp.exp2(s - jnp.max(s, axis=0, keepdims=True)).astype(BF16) for s in ss]
    ols = [_dot(vt[h * VAL_ROWS:(h + 1) * VAL_ROWS], eb) for h, eb in enumerate(ebs)]
    ys = []
    for ol in ols:
        o = ol[0:D] / ol[D:D + 1]
        ys.append(o * lax.rsqrt(jnp.mean(o * o, axis=0, keepdims=True) + NORM_EPS))
    y = jnp.concatenate(ys, axis=0) * og_ref[...] * _silu(gate_ref[0].astype(F32))
    y_ref[0] = y.T.astype(y_ref.dtype)


def _mem_attention(pt, mem_k, mem_vt, out_g, tm):
    B, _, T = pt.shape
    M = mem_k.shape[2]
    W = MEM_WIDTH
    return pl.pallas_call(
        _mem_attn_kernel,
        grid=(B, T // tm),
        in_specs=[
            pl.BlockSpec((1, W, tm), lambda b, i: (b, 3, i)),
            pl.BlockSpec((1, W, tm), lambda b, i: (b, 4, i)),
            pl.BlockSpec((1, MEM_HEADS, M, W), lambda b, i: (b, 0, 0, 0)),
            pl.BlockSpec((1, MEM_HEADS * VAL_ROWS, M), lambda b, i: (b, 0, 0)),
            pl.BlockSpec((W, 1), lambda b, i: (0, 0)),
        ],
        out_specs=pl.BlockSpec((1, tm, W), lambda b, i: (b, i, 0)),
        out_shape=jax.ShapeDtypeStruct((B, T, W), BF16),
        compiler_params=_params("parallel", "parallel"),
        name="mem_attention",
    )(pt, pt, mem_k, mem_vt, out_g.reshape(W, 1))


def _outproj_kernel(x_ref, yr_ref, yn_ref, ym_ref, wr_ref, wn_ref, wm_ref, g_ref, o_ref):
    z = (x_ref[0] + _dot(yr_ref[0], wr_ref[...]) + _dot(yn_ref[0], wn_ref[...])
         + _dot(ym_ref[0], wm_ref[...]))
    o_ref[0] = z * lax.rsqrt(jnp.mean(z * z, axis=-1, keepdims=True) + NORM_EPS) * g_ref[...]


def _output_projection(x, y_rwkv, y_nsa, y_mem, w_out, g, tm):
    B, T, D = x.shape
    wb = w_out.astype(BF16)
    w_r, w_n, w_m = wb[:RWKV_WIDTH], wb[RWKV_WIDTH:RWKV_WIDTH + NSA_WIDTH], wb[RWKV_WIDTH + NSA_WIDTH:]
    tile = lambda w: pl.BlockSpec((1, tm, w), lambda b, i: (b, i, 0))
    const = lambda shape: pl.BlockSpec(shape, lambda b, i: (0,) * len(shape))
    return pl.pallas_call(
        _outproj_kernel,
        grid=(B, T // tm),
        in_specs=[tile(D), tile(RWKV_WIDTH), tile(NSA_WIDTH), tile(MEM_WIDTH),
                  const((RWKV_WIDTH, D)), const((NSA_WIDTH, D)), const((MEM_WIDTH, D)), const((1, D))],
        out_specs=tile(D),
        out_shape=jax.ShapeDtypeStruct((B, T, D), F32),
        compiler_params=_params("parallel", "parallel"),
        name="output_projection",
    )(x, y_rwkv, y_nsa, y_mem, w_r, w_n, w_m, g.reshape(1, D))


def _rope_table(T):
    inv_freq = ROPE_THETA ** (-jnp.arange(ROPE_HALF, dtype=F32) / ROPE_HALF)
    ang = inv_freq[:, None] * jnp.arange(T).astype(F32)[None, :]
    return jnp.concatenate([jnp.cos(ang), jnp.sin(ang)], axis=0)


def _split_w_in(w):
    D = HEAD_DIM
    n0 = RWKV_COLS
    q, gate, glog = n0, n0 + NSA_WIDTH, n0 + 2 * NSA_WIDTH
    kc = glog + NSA_HEADS * N_BRANCH
    vc, ks, vs, kw, vw = kc + D, kc + 2 * D, kc + 3 * D, kc + 4 * D, kc + 5 * D
    m0 = vw + D
    cols = lambda a, n: w[:, a:a + n]
    w_row = jnp.concatenate([cols(0, n0), cols(kc, D), cols(vc, D)], axis=1)
    w_t = jnp.concatenate([cols(ks, D), cols(kw, D), cols(vs, D), cols(vw, D), cols(q, NSA_WIDTH),
                           cols(gate, NSA_WIDTH), cols(m0, MEM_WIDTH), cols(m0 + MEM_WIDTH, MEM_WIDTH),
                           cols(glog, NSA_HEADS * N_BRANCH),
                           jnp.zeros((w.shape[0], 16 - NSA_HEADS * N_BRANCH), w.dtype)], axis=1)
    return w_row.astype(BF16), w_t.T.astype(BF16)


def kernel(x, mem, norm_in_g, w_in, rwkv_mu, rwkv_w0, rwkv_w_up, rwkv_a0, rwkv_a_up, rwkv_k_k,
           rwkv_k_a, rwkv_r_k, rwkv_ln_w, rwkv_ln_b, nsa_cmp_pos, nsa_cmp_k_w1, nsa_cmp_k_w2,
           nsa_cmp_v_w1, nsa_cmp_v_w2, nsa_gate_b, nsa_out_g, mem_norm_g, w_mem_kv, mem_out_g, w_out,
           norm_final_g):
    B, T, D = x.shape
    assert w_in.shape[0] == 1, "single-layer stack: the final norm is fused into the output projection"
    rope = _rope_table(T)
    w_row, w_t = _split_w_in(w_in[0])
    p_row, kcvc, pt = _input_projection(x, norm_in_g[0].reshape(1, D), w_row, w_t, tm=512)
    y_rwkv = _rwkv_group(p_row, rwkv_mu[0], rwkv_w0[0], rwkv_w_up[0], rwkv_a0[0], rwkv_a_up[0],
                         rwkv_k_k[0], rwkv_k_a[0], rwkv_r_k[0].reshape(-1), rwkv_ln_w[0],
                         rwkv_ln_b[0])
    kskw, vt, vs2 = _nsa_keys(pt, rope, tk=2048)
    cmp_rm, cmp_t = _nsa_compress(kcvc, nsa_cmp_pos[0], nsa_cmp_k_w1[0], nsa_cmp_k_w2[0],
                                  nsa_cmp_v_w1[0], nsa_cmp_v_w2[0])
    y_nsa = _nsa_attention(pt, rope, cmp_rm, cmp_t, kskw, vt, vs2, nsa_gate_b[0], nsa_out_g[0])
    mem_k, mem_vt = _mem_kv(mem, mem_norm_g[0], w_mem_kv[0])
    y_mem = _mem_attention(pt, mem_k, mem_vt, mem_out_g[0], tm=512)
    return _output_projection(x, y_rwkv, y_nsa, y_mem, w_out[0], norm_final_g, tm=512)
```

```python
import numpy as np
import jax
import jax.numpy as jnp
from jax import lax
from jax.experimental import pallas as pl
from jax.experimental.pallas import tpu as pltpu

F32 = jnp.float32
BF16 = jnp.bfloat16

HEAD_DIM = 64
RWKV_HEADS = 8
RWKV_WIDTH = RWKV_HEADS * HEAD_DIM
LORA = 64
RWKV_COLS = 4 * RWKV_WIDTH + 2 * LORA
RWKV_GN_EPS = 64e-5
NSA_HEADS = 4
NSA_WIDTH = NSA_HEADS * HEAD_DIM
N_BRANCH = 3
CMP_BLOCK = 32
CMP_STRIDE = 16
SEL_BLOCK = 64
SEL_TOPK = 16
WINDOW = 512
MEM_HEADS = 4
MEM_WIDTH = MEM_HEADS * HEAD_DIM
ROPE_THETA = 500000.0
ROPE_HALF = 8
Q_BLOCK = 256
NORM_EPS = 1e-6
NEG_INF = -1e30
LOG2E = 1.4426950408889634
DECAY_SCALE = 0.6065306597126334

RWKV_CHUNK = 64
RWKV_HEAD_GROUP = 4
RWKV_BATCH_TILE = 4
SEL_KEY_CHUNK = 512
INPROJ_SUBTILE = 256
LANES = 128
MXU_DEPTH = 256
ONES_ROWS = 16
VAL_ROWS = HEAD_DIM + ONES_ROWS
VMEM_LIMIT = 48 * 1024 * 1024


def _dot(a, b):
    return jnp.dot(a, b, preferred_element_type=F32)


def _dot_nt(a, b):
    return lax.dot_general(a, b, (((1,), (1,)), ((), ())), preferred_element_type=F32)


def _dot_tn(a, b):
    return lax.dot_general(a, b, (((0,), (0,)), ((), ())), preferred_element_type=F32)


def _bf(a):
    return a.astype(BF16)


def _split_bf16(a, pieces):
    out = []
    for _ in range(pieces):
        part = a.astype(BF16)
        out.append(part)
        a = a - part.astype(F32)
    return out


def _sigmoid(x):
    return 0.5 * jnp.tanh(0.5 * x) + 0.5


def _silu(x):
    return x * _sigmoid(x)


def _params(*sem):
    return pltpu.CompilerParams(dimension_semantics=sem, vmem_limit_bytes=VMEM_LIMIT)


def _inproj_kernel(x_ref, g_ref, wrow_ref, wt_ref, prow_ref, kc_ref, pt_ref):
    tm = x_ref.shape[1]
    hbs = []
    for r0 in range(0, tm, INPROJ_SUBTILE):
        x = x_ref[0, r0:r0 + INPROJ_SUBTILE, :]
        h = x * lax.rsqrt(jnp.mean(x * x, axis=-1, keepdims=True) + NORM_EPS) * g_ref[...]
        hbs.append(h.astype(BF16))
    for k, hb in enumerate(hbs):
        rows = slice(k * INPROJ_SUBTILE, (k + 1) * INPROJ_SUBTILE)
        row = _dot(hb, wrow_ref[...])
        prow_ref[0, rows, :] = row[:, :RWKV_COLS]
        kc_ref[0, rows, :] = row[:, RWKV_COLS:]
        pt_ref[0, :, rows] = _dot_nt(wt_ref[...], hb).astype(pt_ref.dtype)


def _input_projection(x, g, w_row, w_t, tm):
    B, T, D = x.shape
    n_row = w_row.shape[1]
    n_t = w_t.shape[0]
    return pl.pallas_call(
        _inproj_kernel,
        grid=(B, T // tm),
        in_specs=[
            pl.BlockSpec((1, tm, D), lambda b, i: (b, i, 0)),
            pl.BlockSpec((1, D), lambda b, i: (0, 0)),
            pl.BlockSpec((D, n_row), lambda b, i: (0, 0)),
            pl.BlockSpec((n_t, D), lambda b, i: (0, 0)),
        ],
        out_specs=[
            pl.BlockSpec((1, tm, RWKV_COLS), lambda b, i: (b, i, 0)),
            pl.BlockSpec((1, tm, n_row - RWKV_COLS), lambda b, i: (b, i, 0)),
            pl.BlockSpec((1, n_t, tm), lambda b, i: (b, 0, i)),
        ],
        out_shape=[
            jax.ShapeDtypeStruct((B, T, RWKV_COLS), F32),
            jax.ShapeDtypeStruct((B, T, n_row - RWKV_COLS), F32),
            jax.ShapeDtypeStruct((B, n_t, T), BF16),
        ],
        compiler_params=_params("parallel", "parallel"),
        name="input_projection",
    )(x, g, w_row, w_t)


_X_R, _X_A, _X_K, _X_B, _X_V, _X_BV, _X_SG, _X_G = range(8)


def _rwkv_kernel(podd_ref, peven_ref, prevodd_ref, preveven_ref, mu_ref, w0_ref, wup_ref, a0_ref,
                 aup_ref, kk_ref, ka_ref, rk_ref, lnw_ref, lnb_ref, bd_ref, tri_ref, y_ref,
                 s_ref, xa_ref, xb_ref):
    C, W, N, H = RWKV_CHUNK, RWKV_WIDTH, HEAD_DIM, RWKV_HEADS
    HG = RWKV_HEAD_GROUP
    GW = HG * N
    n = pl.program_id(1)
    n_chunks = 2 * (pl.num_programs(1) - 1)
    bt = podd_ref.shape[0]

    @pl.when(n == 0)
    def _():
        s_ref[...] = jnp.zeros_like(s_ref)
        xa_ref[...] = jnp.zeros_like(xa_ref)
        xb_ref[...] = jnp.zeros_like(xb_ref)

    bd = bd_ref[...]
    head_sum = lambda a: jnp.concatenate(
        [_dot(_bf(a[:, g * GW:(g + 1) * GW]), bd) for g in range(H // HG)], axis=1)
    row = lax.broadcasted_iota(jnp.int32, (C, 1), 0)

    def prepare(b, chunk, p_ref, prev_ref, x_ref):
        p = p_ref[b]
        prev_last = jnp.where(chunk <= 0, 0.0, prev_ref[b][7:8, :])
        prev = jnp.where(row == 0, prev_last, pltpu.roll(p, 1, axis=0))
        pf = p + mu_ref[...] * (prev - p)
        r = pf[:, 0:W]
        k = pf[:, W:2 * W]
        v = pf[:, 2 * W:3 * W]
        gate = pf[:, 3 * W:4 * W]
        wd = _bf(jnp.tanh(pf[:, 4 * W:4 * W + LORA]))
        ad = _bf(pf[:, 4 * W + LORA:4 * W + 2 * LORA])
        yield
        z = w0_ref[...] + _dot(wd, wup_ref[...])
        logw = -DECAY_SCALE * _sigmoid(z)
        eta = _sigmoid(a0_ref[...] + _dot(ad, aup_ref[...]))
        kk = k * kk_ref[...]
        k2 = k * (eta * ka_ref[...] + (1.0 - ka_ref[...]))
        kk_sq = _bf(kk * kk)
        rk2 = _bf(r * k2 * rk_ref[...])
        logw_parts = _split_bf16(logw, 3)
        yield
        kk = kk * lax.rsqrt(jnp.maximum(head_sum(kk_sq), 1e-24))
        x_ref[b, _X_BV] = head_sum(rk2) * v
        x_ref[b, _X_SG] = _silu(gate)
        tri = tri_ref[...]
        cs = sum(_dot(tri, part) for part in logw_parts)
        mid = cs[C // 2 - 1:C // 2, :]
        csm = cs - mid
        end = cs[C - 1:C, :]
        x_ref[b, _X_G] = jnp.concatenate([jnp.exp(mid), jnp.exp(end), jnp.exp(end - mid),
                                          jnp.zeros((C - 3, W), F32)], axis=0)
        yield
        e_out = jnp.exp(-csm)
        x_ref[b, _X_R] = r * jnp.exp(csm)
        x_ref[b, _X_A] = -kk * jnp.exp(csm - logw)
        yield
        x_ref[b, _X_K] = k2 * e_out
        x_ref[b, _X_B] = kk * eta * e_out
        x_ref[b, _X_V] = v

    li = lax.broadcasted_iota(jnp.int32, (C, GW), 1) % N
    ti = lax.broadcasted_iota(jnp.int32, (C, GW), 0)
    strict, incl = ti > li, ti >= li
    eye = (ti == li).astype(F32)
    same_head = (lax.broadcasted_iota(jnp.int32, (GW, GW), 0) // N
                 == lax.broadcasted_iota(jnp.int32, (GW, GW), 1) // N)

    tile_heads = LANES // N
    pair_head = (lax.broadcasted_iota(jnp.int32, (LANES, LANES), 0) // N
                 == lax.broadcasted_iota(jnp.int32, (LANES, LANES), 1) // N)
    zero_tile = jnp.zeros((LANES, LANES), BF16)

    def blockdiag(a):
        a = _bf(a)
        n_tiles = GW // LANES
        rows = []
        for j in range(n_tiles):
            own = jnp.concatenate([a[:, j * LANES:(j + 1) * LANES]] * tile_heads, axis=0)
            own = jnp.where(pair_head, own, jnp.zeros((), BF16))
            rows.append(jnp.concatenate([own if k == j else zero_tile for k in range(n_tiles)], axis=1))
        return jnp.concatenate(rows, axis=0)

    def advance(chunk, x_ref, y_rows, fill):
        live = (chunk >= 0) & (chunk < n_chunks)
        chains = [(b, g) for b in range(bt) for g in range(H // HG)]
        st = {}
        for c in chains:
            b, g = c
            ls = slice(g * GW, (g + 1) * GW)
            kb = jnp.concatenate([blockdiag(x_ref[b, _X_K][:, ls]), blockdiag(x_ref[b, _X_B][:, ls])],
                                 axis=0)
            ar = _bf(jnp.concatenate([x_ref[b, _X_A][:, ls], x_ref[b, _X_R][:, ls]], axis=0))
            st[c] = dict(ls=ls, kb=kb, ar=ar, gc=_dot_nt(ar, kb))
            fill()
        for c in chains:
            b, g = c
            s = st[c]
            ls, gc = s["ls"], s["gc"]
            s0 = s_ref[b, g]
            gates = x_ref[b, _X_G]
            s.update(
                s0=s0, v_bd=blockdiag(x_ref[b, _X_V][:, ls]),
                g_end=gates[1:2, ls], g_end_mid=gates[2:3, ls],
                uy0=_dot_nt(s["ar"], _bf(s0 * gates[0:1, ls])),
                a_ak=jnp.where(strict, gc[0:C, 0:GW], 0.0), a_ab=jnp.where(strict, gc[0:C, GW:2 * GW], 0.0),
                a_rk=jnp.where(incl, gc[C:2 * C, 0:GW], 0.0), a_rb=jnp.where(incl, gc[C:2 * C, GW:2 * GW], 0.0))
            fill()
        for c in chains:
            s = st[c]
            s["rhs"] = s["uy0"][0:C] + _dot(_bf(s["a_ak"]), s["v_bd"])
            s["tm"] = eye + s["a_ab"]
            s["pw"] = _dot(_bf(s["a_ab"]), blockdiag(s["a_ab"]))
            fill()
        span = 2
        while span < C:
            for c in chains:
                s = st[c]
                if 2 * span < C:
                    tp = _dot(_bf(jnp.concatenate([s["tm"], s["pw"]], axis=0)), blockdiag(s["pw"]))
                    s["tm"] = s["tm"] + tp[0:C]
                    s["pw"] = tp[C:2 * C]
                else:
                    s["tm"] = s["tm"] + _dot(_bf(s["tm"]), blockdiag(s["pw"]))
                fill()
            span *= 2
        for c in chains:
            s = st[c]
            s["u"] = _dot(_bf(s["tm"]), blockdiag(s["rhs"]))
            fill()
        for c in chains:
            s = st[c]
            s["vu"] = jnp.concatenate([s["v_bd"], blockdiag(s["u"])], axis=0)
            s["y"] = s["uy0"][C:2 * C] + _dot(_bf(jnp.concatenate([s["a_rk"], s["a_rb"]], axis=1)), s["vu"])
            fill()
        for c in chains:
            b, g = c
            s = st[c]
            ls = s["ls"]
            vu_rows = jnp.concatenate([_bf(x_ref[b, _X_V][:, ls]), _bf(s["u"])], axis=0)
            kb_rows = _bf(jnp.concatenate([x_ref[b, _X_K][:, ls], x_ref[b, _X_B][:, ls]], axis=0))
            keep = jnp.where(live, s["g_end"], 1.0)
            gain = jnp.where(live, s["g_end_mid"], 0.0)
            s_ref[b, g] = s["s0"] * keep + jnp.where(same_head, _dot_tn(vu_rows, kb_rows) * gain, 0.0)
            fill()
        ys = [jnp.concatenate([st[(b, g)]["y"] for g in range(H // HG)], axis=1) for b in range(bt)]
        means = [head_sum(y) * (1.0 / N) for y in ys]
        fill()
        ys = [y - mean for y, mean in zip(ys, means)]
        vars_ = [head_sum(jnp.square(y)) * (1.0 / N) for y in ys]
        fill()
        for b, (y, var) in enumerate(zip(ys, vars_)):
            y = y * lax.rsqrt(var + RWKV_GN_EPS) * lnw_ref[...] + lnb_ref[...] + x_ref[b, _X_BV]
            y_ref[b, y_rows] = (y * x_ref[b, _X_SG]).astype(y_ref.dtype)
            fill()

    def filler(gens):
        gens = list(gens)

        def fill():
            while gens:
                g = gens.pop(0)
                try:
                    next(g)
                    gens.append(g)
                    return
                except StopIteration:
                    pass

        def drain():
            for g in gens:
                for _ in g:
                    pass

        return fill, drain

    fill, drain = filler(prepare(b, 2 * n - 1, podd_ref, prevodd_ref, xb_ref) for b in range(bt))
    advance(2 * n - 2, xa_ref, slice(0, C), fill)
    drain()
    fill, drain = filler(prepare(b, 2 * n, peven_ref, preveven_ref, xa_ref) for b in range(bt))
    advance(2 * n - 1, xb_ref, slice(C, 2 * C), fill)
    drain()


def _rwkv_group(p_row, mu, w0, w_up, a0, a_up, k_k, k_a, r_k, ln_w, ln_b):
    B, T, _ = p_row.shape
    C, W = RWKV_CHUNK, RWKV_WIDTH
    heads = np.arange(RWKV_HEAD_GROUP * HEAD_DIM) // HEAD_DIM
    bd = jnp.asarray(heads[:, None] == heads[None, :], BF16)
    tri = jnp.asarray(np.tril(np.ones((C, C), np.float32)), BF16)
    vec = lambda a: a.reshape(1, -1)
    const = lambda shape: pl.BlockSpec(shape, lambda b, n: (0,) * len(shape))
    bt = RWKV_BATCH_TILE if B % RWKV_BATCH_TILE == 0 else 1
    n_chunks = T // C
    assert n_chunks % 2 == 0
    odd = lambda n: jnp.maximum(2 * n - 1, 0)
    even = lambda n: jnp.minimum(2 * n, n_chunks - 1)
    before = lambda c: jnp.maximum(c * (C // 8) - 1, 0)
    return pl.pallas_call(
        _rwkv_kernel,
        grid=(B // bt, n_chunks // 2 + 1),
        in_specs=[
            pl.BlockSpec((bt, C, RWKV_COLS), lambda b, n: (b, odd(n), 0)),
            pl.BlockSpec((bt, C, RWKV_COLS), lambda b, n: (b, even(n), 0)),
            pl.BlockSpec((bt, 8, RWKV_COLS), lambda b, n: (b, before(odd(n)), 0)),
            pl.BlockSpec((bt, 8, RWKV_COLS), lambda b, n: (b, before(even(n)), 0)),
            const((1, RWKV_COLS)), const((1, W)), const((LORA, W)), const((1, W)), const((LORA, W)),
            const((1, W)), const((1, W)), const((1, W)), const((1, W)), const((1, W)),
            const(bd.shape), const((C, C)),
        ],
        out_specs=pl.BlockSpec((bt, 2 * C, W), lambda b, n: (b, jnp.maximum(n - 1, 0), 0)),
        out_shape=jax.ShapeDtypeStruct((B, T, W), BF16),
        scratch_shapes=[pltpu.VMEM((bt, RWKV_HEADS // RWKV_HEAD_GROUP) + (RWKV_HEAD_GROUP * HEAD_DIM,) * 2,
                                   F32),
                        pltpu.VMEM((bt, 8, C, W), F32), pltpu.VMEM((bt, 8, C, W), F32)],
        compiler_params=_params("parallel", "arbitrary"),
        name="rwkv7_group",
    )(p_row, p_row, p_row, p_row, vec(mu), vec(w0), _bf(w_up), vec(a0), _bf(a_up), vec(k_k), vec(k_a),
      vec(r_k), vec(ln_w), vec(ln_b), bd, tri)


def _rope_rows(x, cos, sin):
    x1, x2 = x[0:ROPE_HALF], x[ROPE_HALF:2 * ROPE_HALF]
    return jnp.concatenate([x1 * cos - x2 * sin, x2 * cos + x1 * sin, x[2 * ROPE_HALF:]], axis=0)


def _nsa_keys_kernel(kv_ref, rope_ref, kskw_ref, vt_ref, vs2_ref):
    kv = kv_ref[0].astype(F32)
    cos, sin = rope_ref[0:ROPE_HALF], rope_ref[ROPE_HALF:2 * ROPE_HALF]
    keys = jnp.concatenate([_rope_rows(kv[0:HEAD_DIM], cos, sin),
                            _rope_rows(kv[HEAD_DIM:2 * HEAD_DIM], cos, sin)], axis=0)
    kskw_ref[0] = keys.T.astype(BF16)
    vals = kv[2 * HEAD_DIM:4 * HEAD_DIM].astype(BF16)
    ones = jnp.ones((ONES_ROWS, LANES), BF16)
    for u in range(vals.shape[1] // LANES):
        blk = vals[:, u * LANES:(u + 1) * LANES]
        vt_ref[0, u] = jnp.concatenate([blk[0:HEAD_DIM], ones, blk[HEAD_DIM:2 * HEAD_DIM], ones], axis=0)
    ones2 = jnp.ones((ONES_ROWS, MXU_DEPTH), BF16)
    for u in range(vals.shape[1] // MXU_DEPTH):
        vs2_ref[0, u] = jnp.concatenate([vals[0:HEAD_DIM, u * MXU_DEPTH:(u + 1) * MXU_DEPTH], ones2], axis=0)


def _nsa_keys(pt, rope, tk):
    B, _, T = pt.shape
    return pl.pallas_call(
        _nsa_keys_kernel,
        grid=(B, T // tk),
        in_specs=[
            pl.BlockSpec((1, 4 * HEAD_DIM, tk), lambda b, i: (b, 0, i)),
            pl.BlockSpec((2 * ROPE_HALF, tk), lambda b, i: (0, i)),
        ],
        out_specs=[
            pl.BlockSpec((1, tk, LANES), lambda b, i: (b, i, 0)),
            pl.BlockSpec((1, tk // LANES, 2 * VAL_ROWS, LANES), lambda b, i: (b, i, 0, 0)),
            pl.BlockSpec((1, tk // MXU_DEPTH, VAL_ROWS, MXU_DEPTH), lambda b, i: (b, i, 0, 0)),
        ],
        out_shape=[
            jax.ShapeDtypeStruct((B, T, LANES), BF16),
            jax.ShapeDtypeStruct((B, T // LANES, 2 * VAL_ROWS, LANES), BF16),
            jax.ShapeDtypeStruct((B, T // MXU_DEPTH, VAL_ROWS, MXU_DEPTH), BF16),
        ],
        compiler_params=_params("parallel", "parallel"),
        name="nsa_keys",
    )(pt, rope)


def _nsa_compress_kernel(g_ref, wc_ref, pos_ref, w2_ref, w2t_ref, rm_ref, t_ref):
    half = CMP_BLOCK // 2
    ng = g_ref.shape[1] // half
    wc = wc_ref[...]
    m = sum(_dot(g_ref[0, pl.ds(l, ng, stride=half), :].astype(BF16), wc[l * LANES:(l + 1) * LANES])
            for l in range(half))
    pm = _dot(pos_ref[...], wc)
    pos_term = pm[0:1, 0:LANES] + pm[1:2, LANES:2 * LANES]
    pre = m[:, 0:LANES] + pltpu.roll(m[:, LANES:2 * LANES], ng - 1, axis=0) + pos_term
    act = _silu(pre).astype(BF16)
    row = lax.broadcasted_iota(jnp.int32, (ng, 1), 0)
    col = lax.broadcasted_iota(jnp.int32, (1, ng), 1)
    rm_ref[0] = jnp.where(row < ng - 1, _dot(act, w2_ref[...]), 0.0).astype(BF16)
    vt = jnp.where(col < ng - 1, _dot_nt(w2t_ref[...], act)[HEAD_DIM:2 * HEAD_DIM], 0.0)
    t_ref[0] = jnp.concatenate([vt, jnp.ones((ONES_ROWS, ng), F32)], axis=0).astype(BF16)


def _nsa_compress(kcvc, cmp_pos, k_w1, k_w2, v_w1, v_w2):
    B, T, _ = kcvc.shape
    ng = T // CMP_STRIDE
    half = CMP_BLOCK // 2
    D = HEAD_DIM

    def spread(w, second, is_v):
        blk = w[second * half * D:(second + 1) * half * D].reshape(half, D, D)
        z = jnp.zeros_like(blk)
        return jnp.concatenate([z, blk] if is_v else [blk, z], axis=1).reshape(half * LANES, D)

    wc = jnp.concatenate([spread(k_w1, 0, False), spread(v_w1, 0, True),
                          spread(k_w1, 1, False), spread(v_w1, 1, True)], axis=1).astype(BF16)
    pos2 = jnp.concatenate([cmp_pos, cmp_pos], axis=1)
    pos = jnp.zeros((8, half * LANES), F32)
    pos = pos.at[0].set(pos2[:half].reshape(-1)).at[1].set(pos2[half:].reshape(-1)).astype(BF16)
    z = jnp.zeros((D, D), F32)
    w2 = jnp.block([[k_w2, z], [z, v_w2]])
    const = lambda shape: pl.BlockSpec(shape, lambda b: (0,) * len(shape))
    return pl.pallas_call(
        _nsa_compress_kernel,
        grid=(B,),
        in_specs=[
            pl.BlockSpec((1, T, LANES), lambda b: (b, 0, 0)),
            const((half * LANES, 2 * LANES)), const((8, half * LANES)),
            const((LANES, LANES)), const((LANES, LANES)),
        ],
        out_specs=[
            pl.BlockSpec((1, ng, LANES), lambda b: (b, 0, 0)),
            pl.BlockSpec((1, VAL_ROWS, ng), lambda b: (b, 0, 0)),
        ],
        out_shape=[
            jax.ShapeDtypeStruct((B, ng, LANES), BF16),
            jax.ShapeDtypeStruct((B, VAL_ROWS, ng), BF16),
        ],
        compiler_params=_params("parallel"),
        name="nsa_compress",
    )(kcvc, wc, pos, w2.astype(BF16), w2.T.astype(BF16))


def _tile_heads(x):
    return jnp.concatenate([x] * NSA_HEADS, axis=1)


def _nsa_attn_kernel(q_ref, gate_ref, glog_ref, rope_ref, rm_ref, ct_ref, kskw_ref, vt_ref, vs2_ref,
                     ov_ref, oh_ref, gb_ref, og_ref, y_ref, m_ref, acc_ref, sa_ref, sb_ref, ma_ref,
                     mb_ref):
    D, Q, Hn = HEAD_DIM, Q_BLOCK, NSA_HEADS
    KC = SEL_KEY_CHUNK
    ng = rm_ref.shape[1]
    ns = ov_ref.shape[0]
    n_top = min(SEL_TOPK, ns)
    i = pl.program_id(1)
    t0 = i * Q
    tq = t0 + lax.broadcasted_iota(jnp.int32, (1, Q), 1)

    q = q_ref[0].astype(F32) * (D ** -0.5 * LOG2E)
    cos, sin = rope_ref[0:ROPE_HALF], rope_ref[ROPE_HALF:2 * ROPE_HALF]
    qh = [q[h * D:(h + 1) * D] for h in range(Hn)]
    q4 = jnp.concatenate(qh, axis=1)
    q4r = jnp.concatenate([_rope_rows(x, cos, sin) for x in qh], axis=1)
    zero = jnp.zeros_like(q4)
    q_lo = jnp.concatenate([q4, zero], axis=0).astype(BF16)
    qr_lo = jnp.concatenate([q4r, zero], axis=0).astype(BF16)
    qr_hi = jnp.concatenate([zero, q4r], axis=0).astype(BF16)

    def masked(s, bias):
        return jnp.concatenate([s[:, h * Q:(h + 1) * Q] + bias for h in range(Hn)], axis=1)


    WK = WINDOW + Q
    w0 = pl.multiple_of(jnp.maximum(t0 - WINDOW, 0), LANES)
    diff = tq - (w0 + lax.broadcasted_iota(jnp.int32, (WK, 1), 0))
    w_bias = jnp.where((diff >= 0) & (diff < WINDOW), 0.0, NEG_INF)

    o_w = []

    def window_stages():
        ss = []
        for h in range(Hn):
            ss.append(_dot(kskw_ref[0, pl.ds(w0, WK), :], qr_hi[:, h * Q:(h + 1) * Q]) + w_bias)
            yield
        ebs = []
        for s in ss:
            ebs.append(jnp.exp2(s - jnp.max(s, axis=0, keepdims=True)).astype(BF16))
            yield
        for eb in ebs:
            ol = _dot(vt_ref[0, w0 // LANES][VAL_ROWS:2 * VAL_ROWS], eb[0:LANES])
            for u in range(1, WK // LANES):
                ol = ol + _dot(vt_ref[0, w0 // LANES + u][VAL_ROWS:2 * VAL_ROWS], eb[u * LANES:(u + 1) * LANES])
            o_w.append(ol[0:D] / jnp.maximum(ol[D:D + 1], 1e-30))
            yield

    cend = lax.broadcasted_iota(jnp.int32, (ng, 1), 0) * CMP_STRIDE + (CMP_BLOCK - 1)
    s = masked(_dot(rm_ref[0], q_lo), jnp.where(cend <= tq, 0.0, NEG_INF))
    eb = jnp.exp2(s - jnp.max(s, axis=0, keepdims=True)).astype(BF16)
    ol = _dot(ct_ref[0], eb)
    seen = _tile_heads((tq >= CMP_BLOCK - 1).astype(F32))
    inv = seen / jnp.maximum(ol[D:D + 1], 1e-30)
    o_c = ol[0:D] * inv
    psum = sum(eb[:, h * Q:(h + 1) * Q] * inv[:, h * Q:(h + 1) * Q] for h in range(Hn))
    imp = _dot(ov_ref[...], psum.astype(BF16))

    blk = lax.broadcasted_iota(jnp.int32, (ns, Q), 0)
    cur = tq // SEL_BLOCK
    forced = (blk == 0) | (blk == cur) | (blk == cur - 1)
    taken = -2.0
    score = jnp.where(forced, taken, jnp.where(blk <= cur, imp, -1.0))
    window = window_stages()
    for _ in range(max(n_top - 3, 0)):
        best = jnp.max(score, axis=0, keepdims=True)
        first = jnp.min(jnp.where(score == best, blk, ns), axis=0, keepdims=True)
        score = jnp.where(blk == first, taken, score)
        next(window, None)
    for _ in window:
        pass
    o_w = jnp.concatenate(o_w, axis=1)

    def with_block_mask(keep):
        bias = _tile_heads(jnp.where(keep, 0.0, NEG_INF))
        pad = oh_ref.shape[1] - ns
        if pad:
            bias = jnp.concatenate([bias, jnp.zeros((pad, Hn * Q), F32)], axis=0)
        return jnp.concatenate([qr_lo, bias.astype(BF16)], axis=0)

    chosen = (score == taken) & (blk <= cur)
    q_own = with_block_mask(chosen)
    q_sel = with_block_mask(chosen & (blk < t0 // SEL_BLOCK))

    def sel_scores(j, s_ref, cmax_ref):
        k0 = pl.multiple_of(j * KC, KC)
        keys = jnp.concatenate([kskw_ref[0, pl.ds(k0, KC), :], oh_ref[pl.ds(k0, KC), :]], axis=1)
        s = _dot(keys, q_sel)
        s_ref[...] = s
        cmax_ref[...] = jnp.max(s, axis=0, keepdims=True)

    UB = KC // MXU_DEPTH

    def sel_update(j, s_ref, cmax_ref):
        m_old = m_ref[...]
        m_new = jnp.maximum(m_old, cmax_ref[...])
        eb = jnp.exp2(s_ref[...] - m_new).astype(BF16)
        pv = acc_ref[...] * jnp.exp2(m_old - m_new)
        for u in range(UB):
            pv = pv + _dot(vs2_ref[0, j * UB + u], eb[u * MXU_DEPTH:(u + 1) * MXU_DEPTH])
        acc_ref[...] = pv
        m_ref[...] = m_new

    n_chunks = (t0 + KC - 1) // KC
    last = kskw_ref.shape[1] // KC - 1
    own = pl.ds(pl.multiple_of(t0, Q), Q)
    s = _dot(jnp.concatenate([kskw_ref[0, own, :], oh_ref[own, :]], axis=1), q_own)
    sel_scores(0, sa_ref, ma_ref)
    lk = lax.broadcasted_iota(jnp.int32, (Q, 1), 0)
    lq = lax.broadcasted_iota(jnp.int32, (1, Q), 1)
    s = masked(s, jnp.where(lk <= lq, 0.0, NEG_INF))
    m0 = jnp.max(s, axis=0, keepdims=True)
    m_ref[...] = m0
    e0 = jnp.exp2(s - m0).astype(BF16)
    QB = Q // MXU_DEPTH
    acc_ref[...] = sum(_dot(vs2_ref[0, i * QB + u], e0[u * MXU_DEPTH:(u + 1) * MXU_DEPTH]) for u in range(QB))

    def chunk_pair(j, carry):
        sel_scores(jnp.minimum(2 * j + 1, last), sb_ref, mb_ref)
        sel_update(2 * j, sa_ref, ma_ref)
        sel_scores(jnp.minimum(2 * j + 2, last), sa_ref, ma_ref)
        sel_update(jnp.minimum(2 * j + 1, last), sb_ref, mb_ref)
        return carry

    lax.fori_loop(0, (n_chunks + 1) // 2, chunk_pair, 0)
    acc = acc_ref[...]
    o_s = acc[0:D] / jnp.maximum(acc[D:D + 1], 1e-30)

    gl = _sigmoid(glog_ref[0].astype(F32) + gb_ref[...])
    ys = []
    for h in range(Hn):
        cs = slice(h * Q, (h + 1) * Q)
        o = (gl[3 * h:3 * h + 1] * o_c[:, cs] + gl[3 * h + 1:3 * h + 2] * o_s[:, cs]
             + gl[3 * h + 2:3 * h + 3] * o_w[:, cs])
        o = o * lax.rsqrt(jnp.mean(o * o, axis=0, keepdims=True) + NORM_EPS)
        ys.append(o)
    y = jnp.concatenate(ys, axis=0) * og_ref[...] * _silu(gate_ref[0].astype(F32))
    y_ref[0] = y.T.astype(y_ref.dtype)


def _nsa_attention(pt, rope, cmp_rm, cmp_t, kskw, vt, vs2, gate_b, out_g):
    B, _, T = pt.shape
    ng = T // CMP_STRIDE
    ns = T // SEL_BLOCK
    Q, W = Q_BLOCK, NSA_WIDTH
    QH = Q * NSA_HEADS
    c0 = np.arange(ng)[None, :] * CMP_STRIDE
    s0 = np.arange(ns)[:, None] * SEL_BLOCK
    ov = np.clip(np.minimum(c0 + CMP_BLOCK, s0 + SEL_BLOCK) - np.maximum(c0, s0), 0, None) / CMP_BLOCK
    ov[:, ng - 1] = 0.0
    oh_lanes = -(-ns // LANES) * LANES
    onehot = (np.arange(T)[:, None] // SEL_BLOCK) == np.arange(oh_lanes)[None, :]
    gb = jnp.zeros((16, 1), F32).at[:NSA_HEADS * N_BRANCH, 0].set(gate_b)
    glog_blk = (4 * HEAD_DIM + 4 * W) // 16
    return pl.pallas_call(
        _nsa_attn_kernel,
        grid=(B, T // Q),
        in_specs=[
            pl.BlockSpec((1, W, Q), lambda b, i: (b, 1, i)),
            pl.BlockSpec((1, W, Q), lambda b, i: (b, 2, i)),
            pl.BlockSpec((1, 16, Q), lambda b, i: (b, glog_blk, i)),
            pl.BlockSpec((2 * ROPE_HALF, Q), lambda b, i: (0, i)),
            pl.BlockSpec((1, ng, LANES), lambda b, i: (b, 0, 0)),
            pl.BlockSpec((1, VAL_ROWS, ng), lambda b, i: (b, 0, 0)),
            pl.BlockSpec((1, T, LANES), lambda b, i: (b, 0, 0)),
            pl.BlockSpec((1, T // LANES, 2 * VAL_ROWS, LANES), lambda b, i: (b, 0, 0, 0)),
            pl.BlockSpec((1, T // MXU_DEPTH, VAL_ROWS, MXU_DEPTH), lambda b, i: (b, 0, 0, 0)),
            pl.BlockSpec((ns, ng), lambda b, i: (0, 0)),
            pl.BlockSpec((T, oh_lanes), lambda b, i: (0, 0)),
            pl.BlockSpec((16, 1), lambda b, i: (0, 0)),
            pl.BlockSpec((W, 1), lambda b, i: (0, 0)),
        ],
        out_specs=pl.BlockSpec((1, Q, W), lambda b, i: (b, i, 0)),
        out_shape=jax.ShapeDtypeStruct((B, T, W), BF16),
        scratch_shapes=[pltpu.VMEM((1, QH), F32), pltpu.VMEM((VAL_ROWS, QH), F32),
                        pltpu.VMEM((SEL_KEY_CHUNK, QH), F32), pltpu.VMEM((SEL_KEY_CHUNK, QH), F32),
                        pltpu.VMEM((1, QH), F32), pltpu.VMEM((1, QH), F32)],
        compiler_params=_params("parallel", "arbitrary"),
        name="nsa_attention",
    )(pt, pt, pt, rope, cmp_rm, cmp_t, kskw, vt, vs2, jnp.asarray(ov, BF16), jnp.asarray(onehot, BF16), gb,
      out_g.reshape(W, 1))


def _mem_kv_kernel(mem_ref, g_ref, w_ref, wt_ref, k_ref, vt_ref):
    x = mem_ref[0]
    h = x * lax.rsqrt(jnp.mean(x * x, axis=-1, keepdims=True) + NORM_EPS) * g_ref[...]
    hb = h.astype(BF16)
    k = _dot(hb, w_ref[...]) * (HEAD_DIM ** -0.5 * LOG2E)
    lane_head = lax.broadcasted_iota(jnp.int32, k.shape, 1) // HEAD_DIM
    for h in range(MEM_HEADS):
        k_ref[0, h] = jnp.where(lane_head == h, k, 0.0).astype(BF16)
    vt = _dot_nt(wt_ref[...], hb)
    ones = jnp.ones((ONES_ROWS, vt.shape[1]), F32)
    vt_ref[0] = jnp.concatenate(
        [part for h in range(MEM_HEADS) for part in (vt[h * HEAD_DIM:(h + 1) * HEAD_DIM], ones)],
        axis=0).astype(BF16)


def _mem_kv(mem, g, w_kv):
    B, M, D = mem.shape
    W = MEM_WIDTH
    const = lambda shape: pl.BlockSpec(shape, lambda b: (0,) * len(shape))
    return pl.pallas_call(
        _mem_kv_kernel,
        grid=(B,),
        in_specs=[pl.BlockSpec((1, M, D), lambda b: (b, 0, 0)), const((1, D)), const((D, W)),
                  const((W, D))],
        out_specs=[pl.BlockSpec((1, MEM_HEADS, M, W), lambda b: (b, 0, 0, 0)),
                   pl.BlockSpec((1, MEM_HEADS * VAL_ROWS, M), lambda b: (b, 0, 0))],
        out_shape=[jax.ShapeDtypeStruct((B, MEM_HEADS, M, W), BF16),
                   jax.ShapeDtypeStruct((B, MEM_HEADS * VAL_ROWS, M), BF16)],
        compiler_params=_params("parallel"),
        name="mem_kv",
    )(mem, g.reshape(1, D), w_kv[:, :W].astype(BF16), w_kv[:, W:].T.astype(BF16))


def _mem_attn_kernel(q_ref, gate_ref, k_ref, vt_ref, og_ref, y_ref):
    D, Hm = HEAD_DIM, MEM_HEADS
    q = q_ref[0]
    vt = vt_ref[0]
    ss = [_dot(k_ref[0, h], q) for h in range(Hm)]
    ebs = [jnp.exp2(s - jnp.max(s, axis=0, keepdims=True)).astype(BF16) for s in ss]
    ols = [_dot(vt[h * VAL_ROWS:(h + 1) * VAL_ROWS], eb) for h, eb in enumerate(ebs)]
    ys = []
    for ol in ols:
        o = ol[0:D] / ol[D:D + 1]
        ys.append(o * lax.rsqrt(jnp.mean(o * o, axis=0, keepdims=True) + NORM_EPS))
    y = jnp.concatenate(ys, axis=0) * og_ref[...] * _silu(gate_ref[0].astype(F32))
    y_ref[0] = y.T.astype(y_ref.dtype)


def _mem_attention(pt, mem_k, mem_vt, out_g, tm):
    B, _, T = pt.shape
    M = mem_k.shape[2]
    W = MEM_WIDTH
    return pl.pallas_call(
        _mem_attn_kernel,
        grid=(B, T // tm),
        in_specs=[
            pl.BlockSpec((1, W, tm), lambda b, i: (b, 3, i)),
            pl.BlockSpec((1, W, tm), lambda b, i: (b, 4, i)),
            pl.BlockSpec((1, MEM_HEADS, M, W), lambda b, i: (b, 0, 0, 0)),
            pl.BlockSpec((1, MEM_HEADS * VAL_ROWS, M), lambda b, i: (b, 0, 0)),
            pl.BlockSpec((W, 1), lambda b, i: (0, 0)),
        ],
        out_specs=pl.BlockSpec((1, tm, W), lambda b, i: (b, i, 0)),
        out_shape=jax.ShapeDtypeStruct((B, T, W), BF16),
        compiler_params=_params("parallel", "parallel"),
        name="mem_attention",
    )(pt, pt, mem_k, mem_vt, out_g.reshape(W, 1))


def _outproj_kernel(x_ref, yr_ref, yn_ref, ym_ref, wr_ref, wn_ref, wm_ref, g_ref, o_ref):
    z = (x_ref[0] + _dot(yr_ref[0], wr_ref[...]) + _dot(yn_ref[0], wn_ref[...])
         + _dot(ym_ref[0], wm_ref[...]))
    o_ref[0] = z * lax.rsqrt(jnp.mean(z * z, axis=-1, keepdims=True) + NORM_EPS) * g_ref[...]


def _output_projection(x, y_rwkv, y_nsa, y_mem, w_out, g, tm):
    B, T, D = x.shape
    wb = w_out.astype(BF16)
    w_r, w_n, w_m = wb[:RWKV_WIDTH], wb[RWKV_WIDTH:RWKV_WIDTH + NSA_WIDTH], wb[RWKV_WIDTH + NSA_WIDTH:]
    tile = lambda w: pl.BlockSpec((1, tm, w), lambda b, i: (b, i, 0))
    const = lambda shape: pl.BlockSpec(shape, lambda b, i: (0,) * len(shape))
    return pl.pallas_call(
        _outproj_kernel,
        grid=(B, T // tm),
        in_specs=[tile(D), tile(RWKV_WIDTH), tile(NSA_WIDTH), tile(MEM_WIDTH),
                  const((RWKV_WIDTH, D)), const((NSA_WIDTH, D)), const((MEM_WIDTH, D)), const((1, D))],
        out_specs=tile(D),
        out_shape=jax.ShapeDtypeStruct((B, T, D), F32),
        compiler_params=_params("parallel", "parallel"),
        name="output_projection",
    )(x, y_rwkv, y_nsa, y_mem, w_r, w_n, w_m, g.reshape(1, D))


def _rope_table(T):
    inv_freq = ROPE_THETA ** (-jnp.arange(ROPE_HALF, dtype=F32) / ROPE_HALF)
    ang = inv_freq[:, None] * jnp.arange(T).astype(F32)[None, :]
    return jnp.concatenate([jnp.cos(ang), jnp.sin(ang)], axis=0)


def _split_w_in(w):
    D = HEAD_DIM
    n0 = RWKV_COLS
    q, gate, glog = n0, n0 + NSA_WIDTH, n0 + 2 * NSA_WIDTH
    kc = glog + NSA_HEADS * N_BRANCH
    vc, ks, vs, kw, vw = kc + D, kc + 2 * D, kc + 3 * D, kc + 4 * D, kc + 5 * D
    m0 = vw + D
    cols = lambda a, n: w[:, a:a + n]
    w_row = jnp.concatenate([cols(0, n0), cols(kc, D), cols(vc, D)], axis=1)
    w_t = jnp.concatenate([cols(ks, D), cols(kw, D), cols(vs, D), cols(vw, D), cols(q, NSA_WIDTH),
                           cols(gate, NSA_WIDTH), cols(m0, MEM_WIDTH), cols(m0 + MEM_WIDTH, MEM_WIDTH),
                           cols(glog, NSA_HEADS * N_BRANCH),
                           jnp.zeros((w.shape[0], 16 - NSA_HEADS * N_BRANCH), w.dtype)], axis=1)
    return w_row.astype(BF16), w_t.T.astype(BF16)


def kernel(x, mem, norm_in_g, w_in, rwkv_mu, rwkv_w0, rwkv_w_up, rwkv_a0, rwkv_a_up, rwkv_k_k,
           rwkv_k_a, rwkv_r_k, rwkv_ln_w, rwkv_ln_b, nsa_cmp_pos, nsa_cmp_k_w1, nsa_cmp_k_w2,
           nsa_cmp_v_w1, nsa_cmp_v_w2, nsa_gate_b, nsa_out_g, mem_norm_g, w_mem_kv, mem_out_g, w_out,
           norm_final_g):
    B, T, D = x.shape
    assert w_in.shape[0] == 1, "single-layer stack: the final norm is fused into the output projection"
    rope = _rope_table(T)
    w_row, w_t = _split_w_in(w_in[0])
    p_row, kcvc, pt = _input_projection(x, norm_in_g[0].reshape(1, D), w_row, w_t, tm=512)
    y_rwkv = _rwkv_group(p_row, rwkv_mu[0], rwkv_w0[0], rwkv_w_up[0], rwkv_a0[0], rwkv_a_up[0],
                         rwkv_k_k[0], rwkv_k_a[0], rwkv_r_k[0].reshape(-1), rwkv_ln_w[0],
                         rwkv_ln_b[0])
    kskw, vt, vs2 = _nsa_keys(pt, rope, tk=2048)
    cmp_rm, cmp_t = _nsa_compress(kcvc, nsa_cmp_pos[0], nsa_cmp_k_w1[0], nsa_cmp_k_w2[0],
                                  nsa_cmp_v_w1[0], nsa_cmp_v_w2[0])
    y_nsa = _nsa_attention(pt, rope, cmp_rm, cmp_t, kskw, vt, vs2, nsa_gate_b[0], nsa_out_g[0])
    mem_k, mem_vt = _mem_kv(mem, mem_norm_g[0], w_mem_kv[0])
    y_mem = _mem_attention(pt, mem_k, mem_vt, mem_out_g[0], tm=1024)
    return _output_projection(x, y_rwkv, y_nsa, y_mem, w_out[0], norm_final_g, tm=1024)
```

```python
import numpy as np
import jax
import jax.numpy as jnp
from jax import lax
from jax.experimental import pallas as pl
from jax.experimental.pallas import tpu as pltpu

F32 = jnp.float32
BF16 = jnp.bfloat16

HEAD_DIM = 64
RWKV_HEADS = 8
RWKV_WIDTH = RWKV_HEADS * HEAD_DIM
LORA = 64
RWKV_COLS = 4 * RWKV_WIDTH + 2 * LORA
RWKV_GN_EPS = 64e-5
NSA_HEADS = 4
NSA_WIDTH = NSA_HEADS * HEAD_DIM
N_BRANCH = 3
CMP_BLOCK = 32
CMP_STRIDE = 16
SEL_BLOCK = 64
SEL_TOPK = 16
WINDOW = 512
MEM_HEADS = 4
MEM_WIDTH = MEM_HEADS * HEAD_DIM
ROPE_THETA = 500000.0
ROPE_HALF = 8
Q_BLOCK = 256
NORM_EPS = 1e-6
NEG_INF = -1e30
LOG2E = 1.4426950408889634
DECAY_SCALE = 0.6065306597126334

RWKV_CHUNK = 64
RWKV_HEAD_GROUP = 4
RWKV_BATCH_TILE = 4
SEL_KEY_CHUNK = 512
INPROJ_SUBTILE = 256
LANES = 128
MXU_DEPTH = 256
ONES_ROWS = 16
VAL_ROWS = HEAD_DIM + ONES_ROWS
VMEM_LIMIT = 48 * 1024 * 1024


def _dot(a, b):
    return jnp.dot(a, b, preferred_element_type=F32)


def _dot_nt(a, b):
    return lax.dot_general(a, b, (((1,), (1,)), ((), ())), preferred_element_type=F32)


def _dot_tn(a, b):
    return lax.dot_general(a, b, (((0,), (0,)), ((), ())), preferred_element_type=F32)


def _bf(a):
    return a.astype(BF16)


def _split_bf16(a, pieces):
    out = []
    for _ in range(pieces):
        part = a.astype(BF16)
        out.append(part)
        a = a - part.astype(F32)
    return out


def _sigmoid(x):
    return 0.5 * jnp.tanh(0.5 * x) + 0.5


def _silu(x):
    return x * _sigmoid(x)


def _params(*sem):
    return pltpu.CompilerParams(dimension_semantics=sem, vmem_limit_bytes=VMEM_LIMIT)


def _inproj_kernel(x_ref, g_ref, wrow_ref, wt_ref, prow_ref, kc_ref, pt_ref):
    tm = x_ref.shape[1]
    hbs = []
    for r0 in range(0, tm, INPROJ_SUBTILE):
        x = x_ref[0, r0:r0 + INPROJ_SUBTILE, :]
        h = x * lax.rsqrt(jnp.mean(x * x, axis=-1, keepdims=True) + NORM_EPS) * g_ref[...]
        hbs.append(h.astype(BF16))
    for k, hb in enumerate(hbs):
        rows = slice(k * INPROJ_SUBTILE, (k + 1) * INPROJ_SUBTILE)
        row = _dot(hb, wrow_ref[...])
        prow_ref[0, rows, :] = row[:, :RWKV_COLS]
        kc_ref[0, rows, :] = row[:, RWKV_COLS:]
        pt_ref[0, :, rows] = _dot_nt(wt_ref[...], hb).astype(pt_ref.dtype)


def _input_projection(x, g, w_row, w_t, tm):
    B, T, D = x.shape
    n_row = w_row.shape[1]
    n_t = w_t.shape[0]
    return pl.pallas_call(
        _inproj_kernel,
        grid=(B, T // tm),
        in_specs=[
            pl.BlockSpec((1, tm, D), lambda b, i: (b, i, 0)),
            pl.BlockSpec((1, D), lambda b, i: (0, 0)),
            pl.BlockSpec((D, n_row), lambda b, i: (0, 0)),
            pl.BlockSpec((n_t, D), lambda b, i: (0, 0)),
        ],
        out_specs=[
            pl.BlockSpec((1, tm, RWKV_COLS), lambda b, i: (b, i, 0)),
            pl.BlockSpec((1, tm, n_row - RWKV_COLS), lambda b, i: (b, i, 0)),
            pl.BlockSpec((1, n_t, tm), lambda b, i: (b, 0, i)),
        ],
        out_shape=[
            jax.ShapeDtypeStruct((B, T, RWKV_COLS), F32),
            jax.ShapeDtypeStruct((B, T, n_row - RWKV_COLS), F32),
            jax.ShapeDtypeStruct((B, n_t, T), BF16),
        ],
        compiler_params=_params("parallel", "parallel"),
        name="input_projection",
    )(x, g, w_row, w_t)


_X_R, _X_A, _X_K, _X_B, _X_V, _X_BV, _X_SG, _X_G = range(8)


def _rwkv_kernel(podd_ref, peven_ref, prevodd_ref, preveven_ref, mu_ref, w0_ref, wup_ref, a0_ref,
                 aup_ref, kk_ref, ka_ref, rk_ref, lnw_ref, lnb_ref, bd_ref, tri_ref, y_ref,
                 s_ref, xa_ref, xb_ref):
    C, W, N, H = RWKV_CHUNK, RWKV_WIDTH, HEAD_DIM, RWKV_HEADS
    HG = RWKV_HEAD_GROUP
    GW = HG * N
    n = pl.program_id(1)
    n_chunks = 2 * (pl.num_programs(1) - 1)
    bt = podd_ref.shape[0]

    @pl.when(n == 0)
    def _():
        s_ref[...] = jnp.zeros_like(s_ref)
        xa_ref[...] = jnp.zeros_like(xa_ref)
        xb_ref[...] = jnp.zeros_like(xb_ref)

    bd = bd_ref[...]
    head_sum = lambda a: jnp.concatenate(
        [_dot(_bf(a[:, g * GW:(g + 1) * GW]), bd) for g in range(H // HG)], axis=1)
    row = lax.broadcasted_iota(jnp.int32, (C, 1), 0)

    def prepare(b, chunk, p_ref, prev_ref, x_ref):
        p = p_ref[b]
        prev_last = jnp.where(chunk <= 0, 0.0, prev_ref[b][7:8, :])
        prev = jnp.where(row == 0, prev_last, pltpu.roll(p, 1, axis=0))
        pf = p + mu_ref[...] * (prev - p)
        r = pf[:, 0:W]
        k = pf[:, W:2 * W]
        v = pf[:, 2 * W:3 * W]
        gate = pf[:, 3 * W:4 * W]
        wd = _bf(jnp.tanh(pf[:, 4 * W:4 * W + LORA]))
        ad = _bf(pf[:, 4 * W + LORA:4 * W + 2 * LORA])
        yield
        z = w0_ref[...] + _dot(wd, wup_ref[...])
        half_scale = -0.5 * DECAY_SCALE * LOG2E
        logw = half_scale * jnp.tanh(0.5 * z) + half_scale
        eta = _sigmoid(a0_ref[...] + _dot(ad, aup_ref[...]))
        kk = k * kk_ref[...]
        k2 = k * (eta * ka_ref[...] + (1.0 - ka_ref[...]))
        kk_sq = _bf(kk * kk)
        rk2 = _bf(r * k2 * rk_ref[...])
        logw_parts = _split_bf16(logw, 3)
        yield
        kk = kk * lax.rsqrt(jnp.maximum(head_sum(kk_sq), 1e-24))
        x_ref[b, _X_BV] = head_sum(rk2) * v
        x_ref[b, _X_SG] = _silu(gate)
        tri = tri_ref[...]
        cs = sum(_dot(tri, part) for part in logw_parts)
        mid = cs[C // 2 - 1:C // 2, :]
        csm = cs - mid
        end = cs[C - 1:C, :]
        x_ref[b, _X_G] = jnp.concatenate([jnp.exp2(mid), jnp.exp2(end), jnp.exp2(end - mid),
                                          jnp.zeros((C - 3, W), F32)], axis=0)
        yield
        e_out = jnp.exp2(-csm)
        x_ref[b, _X_R] = r * jnp.exp2(csm)
        x_ref[b, _X_A] = -kk * jnp.exp2(csm - logw)
        yield
        x_ref[b, _X_K] = k2 * e_out
        x_ref[b, _X_B] = kk * eta * e_out
        x_ref[b, _X_V] = v

    li = lax.broadcasted_iota(jnp.int32, (C, GW), 1) % N
    ti = lax.broadcasted_iota(jnp.int32, (C, GW), 0)
    strict, incl = ti > li, ti >= li
    eye = (ti == li).astype(F32)
    same_head = (lax.broadcasted_iota(jnp.int32, (GW, GW), 0) // N
                 == lax.broadcasted_iota(jnp.int32, (GW, GW), 1) // N)

    tile_heads = LANES // N
    pair_head = (lax.broadcasted_iota(jnp.int32, (LANES, LANES), 0) // N
                 == lax.broadcasted_iota(jnp.int32, (LANES, LANES), 1) // N)
    zero_tile = jnp.zeros((LANES, LANES), BF16)

    def blockdiag(a):
        a = _bf(a)
        n_tiles = GW // LANES
        rows = []
        for j in range(n_tiles):
            own = jnp.concatenate([a[:, j * LANES:(j + 1) * LANES]] * tile_heads, axis=0)
            own = jnp.where(pair_head, own, jnp.zeros((), BF16))
            rows.append(jnp.concatenate([own if k == j else zero_tile for k in range(n_tiles)], axis=1))
        return jnp.concatenate(rows, axis=0)

    def advance(chunk, x_ref, y_rows, fill):
        live = (chunk >= 0) & (chunk < n_chunks)
        chains = [(b, g) for b in range(bt) for g in range(H // HG)]
        st = {}
        for c in chains:
            b, g = c
            ls = slice(g * GW, (g + 1) * GW)
            kb = jnp.concatenate([blockdiag(x_ref[b, _X_K][:, ls]), blockdiag(x_ref[b, _X_B][:, ls])],
                                 axis=0)
            ar = _bf(jnp.concatenate([x_ref[b, _X_A][:, ls], x_ref[b, _X_R][:, ls]], axis=0))
            st[c] = dict(ls=ls, kb=kb, ar=ar, gc=_dot_nt(ar, kb))
            fill()
        for c in chains:
            b, g = c
            s = st[c]
            ls, gc = s["ls"], s["gc"]
            s0 = s_ref[b, g]
            gates = x_ref[b, _X_G]
            s.update(
                s0=s0, v_bd=blockdiag(x_ref[b, _X_V][:, ls]),
                g_end=gates[1:2, ls], g_end_mid=gates[2:3, ls],
                uy0=_dot_nt(s["ar"], _bf(s0 * gates[0:1, ls])),
                a_ak=jnp.where(strict, gc[0:C, 0:GW], 0.0), a_ab=jnp.where(strict, gc[0:C, GW:2 * GW], 0.0),
                a_rk=jnp.where(incl, gc[C:2 * C, 0:GW], 0.0), a_rb=jnp.where(incl, gc[C:2 * C, GW:2 * GW], 0.0))
            fill()
        for c in chains:
            s = st[c]
            s["rhs"] = s["uy0"][0:C] + _dot(_bf(s["a_ak"]), s["v_bd"])
            s["tm"] = eye + s["a_ab"]
            s["pw"] = _dot(_bf(s["a_ab"]), blockdiag(s["a_ab"]))
            fill()
        span = 2
        while span < C:
            for c in chains:
                s = st[c]
                if 2 * span < C:
                    tp = _dot(_bf(jnp.concatenate([s["tm"], s["pw"]], axis=0)), blockdiag(s["pw"]))
                    s["tm"] = s["tm"] + tp[0:C]
                    s["pw"] = tp[C:2 * C]
                else:
                    s["tm"] = s["tm"] + _dot(_bf(s["tm"]), blockdiag(s["pw"]))
                fill()
            span *= 2
        for c in chains:
            s = st[c]
            s["u"] = _dot(_bf(s["tm"]), blockdiag(s["rhs"]))
            fill()
        for c in chains:
            s = st[c]
            s["vu"] = jnp.concatenate([s["v_bd"], blockdiag(s["u"])], axis=0)
            s["y"] = s["uy0"][C:2 * C] + _dot(_bf(jnp.concatenate([s["a_rk"], s["a_rb"]], axis=1)), s["vu"])
            fill()
        for c in chains:
            b, g = c
            s = st[c]
            ls = s["ls"]
            vu_rows = jnp.concatenate([_bf(x_ref[b, _X_V][:, ls]), _bf(s["u"])], axis=0)
            kb_rows = _bf(jnp.concatenate([x_ref[b, _X_K][:, ls], x_ref[b, _X_B][:, ls]], axis=0))
            keep = jnp.where(live, s["g_end"], 1.0)
            gain = jnp.where(live, s["g_end_mid"], 0.0)
            s_ref[b, g] = s["s0"] * keep + jnp.where(same_head, _dot_tn(vu_rows, kb_rows) * gain, 0.0)
            fill()
        ys = [jnp.concatenate([st[(b, g)]["y"] for g in range(H // HG)], axis=1) for b in range(bt)]
        means = [head_sum(y) * (1.0 / N) for y in ys]
        fill()
        ys = [y - mean for y, mean in zip(ys, means)]
        vars_ = [head_sum(jnp.square(y)) * (1.0 / N) for y in ys]
        fill()
        for b, (y, var) in enumerate(zip(ys, vars_)):
            y = y * lax.rsqrt(var + RWKV_GN_EPS) * lnw_ref[...] + lnb_ref[...] + x_ref[b, _X_BV]
            y_ref[b, y_rows] = (y * x_ref[b, _X_SG]).astype(y_ref.dtype)
            fill()

    def filler(gens):
        gens = list(gens)

        def fill():
            while gens:
                g = gens.pop(0)
                try:
                    next(g)
                    gens.append(g)
                    return
                except StopIteration:
                    pass

        def drain():
            for g in gens:
                for _ in g:
                    pass

        return fill, drain

    fill, drain = filler(prepare(b, 2 * n - 1, podd_ref, prevodd_ref, xb_ref) for b in range(bt))
    advance(2 * n - 2, xa_ref, slice(0, C), fill)
    drain()
    fill, drain = filler(prepare(b, 2 * n, peven_ref, preveven_ref, xa_ref) for b in range(bt))
    advance(2 * n - 1, xb_ref, slice(C, 2 * C), fill)
    drain()


def _rwkv_group(p_row, mu, w0, w_up, a0, a_up, k_k, k_a, r_k, ln_w, ln_b):
    B, T, _ = p_row.shape
    C, W = RWKV_CHUNK, RWKV_WIDTH
    heads = np.arange(RWKV_HEAD_GROUP * HEAD_DIM) // HEAD_DIM
    bd = jnp.asarray(heads[:, None] == heads[None, :], BF16)
    tri = jnp.asarray(np.tril(np.ones((C, C), np.float32)), BF16)
    vec = lambda a: a.reshape(1, -1)
    const = lambda shape: pl.BlockSpec(shape, lambda b, n: (0,) * len(shape))
    bt = RWKV_BATCH_TILE if B % RWKV_BATCH_TILE == 0 else 1
    n_chunks = T // C
    assert n_chunks % 2 == 0
    odd = lambda n: jnp.maximum(2 * n - 1, 0)
    even = lambda n: jnp.minimum(2 * n, n_chunks - 1)
    before = lambda c: jnp.maximum(c * (C // 8) - 1, 0)
    return pl.pallas_call(
        _rwkv_kernel,
        grid=(B // bt, n_chunks // 2 + 1),
        in_specs=[
            pl.BlockSpec((bt, C, RWKV_COLS), lambda b, n: (b, odd(n), 0)),
            pl.BlockSpec((bt, C, RWKV_COLS), lambda b, n: (b, even(n), 0)),
            pl.BlockSpec((bt, 8, RWKV_COLS), lambda b, n: (b, before(odd(n)), 0)),
            pl.BlockSpec((bt, 8, RWKV_COLS), lambda b, n: (b, before(even(n)), 0)),
            const((1, RWKV_COLS)), const((1, W)), const((LORA, W)), const((1, W)), const((LORA, W)),
            const((1, W)), const((1, W)), const((1, W)), const((1, W)), const((1, W)),
            const(bd.shape), const((C, C)),
        ],
        out_specs=pl.BlockSpec((bt, 2 * C, W), lambda b, n: (b, jnp.maximum(n - 1, 0), 0)),
        out_shape=jax.ShapeDtypeStruct((B, T, W), BF16),
        scratch_shapes=[pltpu.VMEM((bt, RWKV_HEADS // RWKV_HEAD_GROUP) + (RWKV_HEAD_GROUP * HEAD_DIM,) * 2,
                                   F32),
                        pltpu.VMEM((bt, 8, C, W), F32), pltpu.VMEM((bt, 8, C, W), F32)],
        compiler_params=_params("parallel", "arbitrary"),
        name="rwkv7_group",
    )(p_row, p_row, p_row, p_row, vec(mu), vec(w0), _bf(w_up), vec(a0), _bf(a_up), vec(k_k), vec(k_a),
      vec(r_k), vec(ln_w), vec(ln_b), bd, tri)


def _rope_rows(x, cos, sin):
    x1, x2 = x[0:ROPE_HALF], x[ROPE_HALF:2 * ROPE_HALF]
    return jnp.concatenate([x1 * cos - x2 * sin, x2 * cos + x1 * sin, x[2 * ROPE_HALF:]], axis=0)


def _nsa_keys_kernel(kv_ref, rope_ref, kskw_ref, vt_ref, vs2_ref):
    kv = kv_ref[0].astype(F32)
    cos, sin = rope_ref[0:ROPE_HALF], rope_ref[ROPE_HALF:2 * ROPE_HALF]
    keys = jnp.concatenate([_rope_rows(kv[0:HEAD_DIM], cos, sin),
                            _rope_rows(kv[HEAD_DIM:2 * HEAD_DIM], cos, sin)], axis=0)
    kskw_ref[0] = keys.T.astype(BF16)
    vals = kv[2 * HEAD_DIM:4 * HEAD_DIM].astype(BF16)
    ones = jnp.ones((ONES_ROWS, LANES), BF16)
    for u in range(vals.shape[1] // LANES):
        blk = vals[:, u * LANES:(u + 1) * LANES]
        vt_ref[0, u] = jnp.concatenate([blk[0:HEAD_DIM], ones, blk[HEAD_DIM:2 * HEAD_DIM], ones], axis=0)
    ones2 = jnp.ones((ONES_ROWS, MXU_DEPTH), BF16)
    for u in range(vals.shape[1] // MXU_DEPTH):
        vs2_ref[0, u] = jnp.concatenate([vals[0:HEAD_DIM, u * MXU_DEPTH:(u + 1) * MXU_DEPTH], ones2], axis=0)


def _nsa_keys(pt, rope, tk):
    B, _, T = pt.shape
    return pl.pallas_call(
        _nsa_keys_kernel,
        grid=(B, T // tk),
        in_specs=[
            pl.BlockSpec((1, 4 * HEAD_DIM, tk), lambda b, i: (b, 0, i)),
            pl.BlockSpec((2 * ROPE_HALF, tk), lambda b, i: (0, i)),
        ],
        out_specs=[
            pl.BlockSpec((1, tk, LANES), lambda b, i: (b, i, 0)),
            pl.BlockSpec((1, tk // LANES, 2 * VAL_ROWS, LANES), lambda b, i: (b, i, 0, 0)),
            pl.BlockSpec((1, tk // MXU_DEPTH, VAL_ROWS, MXU_DEPTH), lambda b, i: (b, i, 0, 0)),
        ],
        out_shape=[
            jax.ShapeDtypeStruct((B, T, LANES), BF16),
            jax.ShapeDtypeStruct((B, T // LANES, 2 * VAL_ROWS, LANES), BF16),
            jax.ShapeDtypeStruct((B, T // MXU_DEPTH, VAL_ROWS, MXU_DEPTH), BF16),
        ],
        compiler_params=_params("parallel", "parallel"),
        name="nsa_keys",
    )(pt, rope)


def _nsa_compress_kernel(g_ref, wc_ref, pos_ref, w2_ref, w2t_ref, rm_ref, t_ref):
    half = CMP_BLOCK // 2
    ng = g_ref.shape[1] // half
    wc = wc_ref[...]
    m = sum(_dot(g_ref[0, pl.ds(l, ng, stride=half), :].astype(BF16), wc[l * LANES:(l + 1) * LANES])
            for l in range(half))
    pm = _dot(pos_ref[...], wc)
    pos_term = pm[0:1, 0:LANES] + pm[1:2, LANES:2 * LANES]
    pre = m[:, 0:LANES] + pltpu.roll(m[:, LANES:2 * LANES], ng - 1, axis=0) + pos_term
    act = _silu(pre).astype(BF16)
    row = lax.broadcasted_iota(jnp.int32, (ng, 1), 0)
    col = lax.broadcasted_iota(jnp.int32, (1, ng), 1)
    rm_ref[0] = jnp.where(row < ng - 1, _dot(act, w2_ref[...]), 0.0).astype(BF16)
    vt = jnp.where(col < ng - 1, _dot_nt(w2t_ref[...], act)[HEAD_DIM:2 * HEAD_DIM], 0.0)
    t_ref[0] = jnp.concatenate([vt, jnp.ones((ONES_ROWS, ng), F32)], axis=0).astype(BF16)


def _nsa_compress(kcvc, cmp_pos, k_w1, k_w2, v_w1, v_w2):
    B, T, _ = kcvc.shape
    ng = T // CMP_STRIDE
    half = CMP_BLOCK // 2
    D = HEAD_DIM

    def spread(w, second, is_v):
        blk = w[second * half * D:(second + 1) * half * D].reshape(half, D, D)
        z = jnp.zeros_like(blk)
        return jnp.concatenate([z, blk] if is_v else [blk, z], axis=1).reshape(half * LANES, D)

    wc = jnp.concatenate([spread(k_w1, 0, False), spread(v_w1, 0, True),
                          spread(k_w1, 1, False), spread(v_w1, 1, True)], axis=1).astype(BF16)
    pos2 = jnp.concatenate([cmp_pos, cmp_pos], axis=1)
    pos = jnp.zeros((8, half * LANES), F32)
    pos = pos.at[0].set(pos2[:half].reshape(-1)).at[1].set(pos2[half:].reshape(-1)).astype(BF16)
    z = jnp.zeros((D, D), F32)
    w2 = jnp.block([[k_w2, z], [z, v_w2]])
    const = lambda shape: pl.BlockSpec(shape, lambda b: (0,) * len(shape))
    return pl.pallas_call(
        _nsa_compress_kernel,
        grid=(B,),
        in_specs=[
            pl.BlockSpec((1, T, LANES), lambda b: (b, 0, 0)),
            const((half * LANES, 2 * LANES)), const((8, half * LANES)),
            const((LANES, LANES)), const((LANES, LANES)),
        ],
        out_specs=[
            pl.BlockSpec((1, ng, LANES), lambda b: (b, 0, 0)),
            pl.BlockSpec((1, VAL_ROWS, ng), lambda b: (b, 0, 0)),
        ],
        out_shape=[
            jax.ShapeDtypeStruct((B, ng, LANES), BF16),
            jax.ShapeDtypeStruct((B, VAL_ROWS, ng), BF16),
        ],
        compiler_params=_params("parallel"),
        name="nsa_compress",
    )(kcvc, wc, pos, w2.astype(BF16), w2.T.astype(BF16))


def _tile_heads(x):
    return jnp.concatenate([x] * NSA_HEADS, axis=1)


def _nsa_attn_kernel(q_ref, gate_ref, glog_ref, rope_ref, rm_ref, ct_ref, kskw_ref, vt_ref, vs2_ref,
                     ov_ref, oh_ref, gb_ref, og_ref, y_ref, m_ref, acc_ref, sa_ref, sb_ref, ma_ref,
                     mb_ref):
    D, Q, Hn = HEAD_DIM, Q_BLOCK, NSA_HEADS
    KC = SEL_KEY_CHUNK
    ng = rm_ref.shape[1]
    ns = ov_ref.shape[0]
    n_top = min(SEL_TOPK, ns)
    i = pl.program_id(1)
    t0 = i * Q
    tq = t0 + lax.broadcasted_iota(jnp.int32, (1, Q), 1)

    q = q_ref[0].astype(F32) * (D ** -0.5 * LOG2E)
    cos, sin = rope_ref[0:ROPE_HALF], rope_ref[ROPE_HALF:2 * ROPE_HALF]
    qh = [q[h * D:(h + 1) * D] for h in range(Hn)]
    q4 = jnp.concatenate(qh, axis=1)
    q4r = jnp.concatenate([_rope_rows(x, cos, sin) for x in qh], axis=1)
    zero = jnp.zeros_like(q4)
    q_lo = jnp.concatenate([q4, zero], axis=0).astype(BF16)
    qr_lo = jnp.concatenate([q4r, zero], axis=0).astype(BF16)
    qr_hi = jnp.concatenate([zero, q4r], axis=0).astype(BF16)

    def masked(s, bias):
        return jnp.concatenate([s[:, h * Q:(h + 1) * Q] + bias for h in range(Hn)], axis=1)


    WK = WINDOW + Q
    w0 = pl.multiple_of(jnp.maximum(t0 - WINDOW, 0), LANES)
    diff = tq - (w0 + lax.broadcasted_iota(jnp.int32, (WK, 1), 0))
    w_bias = jnp.where((diff >= 0) & (diff < WINDOW), 0.0, NEG_INF)

    o_w = []

    def window_stages():
        ss = []
        for h in range(Hn):
            ss.append(_dot(kskw_ref[0, pl.ds(w0, WK), :], qr_hi[:, h * Q:(h + 1) * Q]) + w_bias)
            yield
        ebs = []
        for s in ss:
            ebs.append(jnp.exp2(s - jnp.max(s, axis=0, keepdims=True)).astype(BF16))
            yield
        for eb in ebs:
            ol = _dot(vt_ref[0, w0 // LANES][VAL_ROWS:2 * VAL_ROWS], eb[0:LANES])
            for u in range(1, WK // LANES):
                ol = ol + _dot(vt_ref[0, w0 // LANES + u][VAL_ROWS:2 * VAL_ROWS], eb[u * LANES:(u + 1) * LANES])
            o_w.append(ol[0:D] / jnp.maximum(ol[D:D + 1], 1e-30))
            yield

    cend = lax.broadcasted_iota(jnp.int32, (ng, 1), 0) * CMP_STRIDE + (CMP_BLOCK - 1)
    s = masked(_dot(rm_ref[0], q_lo), jnp.where(cend <= tq, 0.0, NEG_INF))
    eb = jnp.exp2(s - jnp.max(s, axis=0, keepdims=True)).astype(BF16)
    ol = _dot(ct_ref[0], eb)
    seen = _tile_heads((tq >= CMP_BLOCK - 1).astype(F32))
    inv = seen / jnp.maximum(ol[D:D + 1], 1e-30)
    o_c = ol[0:D] * inv
    psum = sum(eb[:, h * Q:(h + 1) * Q] * inv[:, h * Q:(h + 1) * Q] for h in range(Hn))
    imp = _dot(ov_ref[...], psum.astype(BF16))

    blk = lax.broadcasted_iota(jnp.int32, (ns, Q), 0)
    cur = tq // SEL_BLOCK
    forced = (blk == 0) | (blk == cur) | (blk == cur - 1)
    taken = -2.0
    score = jnp.where(forced, taken, jnp.where(blk <= cur, imp, -1.0))
    window = window_stages()
    for _ in range(max(n_top - 3, 0)):
        best = jnp.max(score, axis=0, keepdims=True)
        first = jnp.min(jnp.where(score == best, blk, ns), axis=0, keepdims=True)
        score = jnp.where(blk == first, taken, score)
        next(window, None)
    for _ in window:
        pass
    o_w = jnp.concatenate(o_w, axis=1)

    def with_block_mask(keep):
        bias = _tile_heads(jnp.where(keep, 0.0, NEG_INF))
        pad = oh_ref.shape[1] - ns
        if pad:
            bias = jnp.concatenate([bias, jnp.zeros((pad, Hn * Q), F32)], axis=0)
        return jnp.concatenate([qr_lo, bias.astype(BF16)], axis=0)

    chosen = (score == taken) & (blk <= cur)
    q_own = with_block_mask(chosen)
    q_sel = with_block_mask(chosen & (blk < t0 // SEL_BLOCK))

    def sel_scores(j, s_ref, cmax_ref):
        k0 = pl.multiple_of(j * KC, KC)
        keys = jnp.concatenate([kskw_ref[0, pl.ds(k0, KC), :], oh_ref[pl.ds(k0, KC), :]], axis=1)
        s = _dot(keys, q_sel)
        s_ref[...] = s
        cmax_ref[...] = jnp.max(s, axis=0, keepdims=True)

    UB = KC // MXU_DEPTH

    def sel_update(j, s_ref, cmax_ref):
        m_old = m_ref[...]
        m_new = jnp.maximum(m_old, cmax_ref[...])
        eb = jnp.exp2(s_ref[...] - m_new).astype(BF16)
        pv = acc_ref[...] * jnp.exp2(m_old - m_new)
        for u in range(UB):
            pv = pv + _dot(vs2_ref[0, j * UB + u], eb[u * MXU_DEPTH:(u + 1) * MXU_DEPTH])
        acc_ref[...] = pv
        m_ref[...] = m_new

    n_chunks = (t0 + KC - 1) // KC
    last = kskw_ref.shape[1] // KC - 1
    own = pl.ds(pl.multiple_of(t0, Q), Q)
    s = _dot(jnp.concatenate([kskw_ref[0, own, :], oh_ref[own, :]], axis=1), q_own)
    sel_scores(0, sa_ref, ma_ref)
    lk = lax.broadcasted_iota(jnp.int32, (Q, 1), 0)
    lq = lax.broadcasted_iota(jnp.int32, (1, Q), 1)
    s = masked(s, jnp.where(lk <= lq, 0.0, NEG_INF))
    m0 = jnp.max(s, axis=0, keepdims=True)
    m_ref[...] = m0
    e0 = jnp.exp2(s - m0).astype(BF16)
    QB = Q // MXU_DEPTH
    acc_ref[...] = sum(_dot(vs2_ref[0, i * QB + u], e0[u * MXU_DEPTH:(u + 1) * MXU_DEPTH]) for u in range(QB))

    def chunk_pair(j, carry):
        sel_scores(jnp.minimum(2 * j + 1, last), sb_ref, mb_ref)
        sel_update(2 * j, sa_ref, ma_ref)
        sel_scores(jnp.minimum(2 * j + 2, last), sa_ref, ma_ref)
        sel_update(jnp.minimum(2 * j + 1, last), sb_ref, mb_ref)
        return carry

    lax.fori_loop(0, (n_chunks + 1) // 2, chunk_pair, 0)
    acc = acc_ref[...]
    o_s = acc[0:D] / jnp.maximum(acc[D:D + 1], 1e-30)

    gl = _sigmoid(glog_ref[0].astype(F32) + gb_ref[...])
    ys = []
    for h in range(Hn):
        cs = slice(h * Q, (h + 1) * Q)
        o = (gl[3 * h:3 * h + 1] * o_c[:, cs] + gl[3 * h + 1:3 * h + 2] * o_s[:, cs]
             + gl[3 * h + 2:3 * h + 3] * o_w[:, cs])
        o = o * lax.rsqrt(jnp.mean(o * o, axis=0, keepdims=True) + NORM_EPS)
        ys.append(o)
    y = jnp.concatenate(ys, axis=0) * og_ref[...] * _silu(gate_ref[0].astype(F32))
    y_ref[0] = y.T.astype(y_ref.dtype)


def _nsa_attention(pt, rope, cmp_rm, cmp_t, kskw, vt, vs2, gate_b, out_g):
    B, _, T = pt.shape
    ng = T // CMP_STRIDE
    ns = T // SEL_BLOCK
    Q, W = Q_BLOCK, NSA_WIDTH
    QH = Q * NSA_HEADS
    c0 = np.arange(ng)[None, :] * CMP_STRIDE
    s0 = np.arange(ns)[:, None] * SEL_BLOCK
    ov = np.clip(np.minimum(c0 + CMP_BLOCK, s0 + SEL_BLOCK) - np.maximum(c0, s0), 0, None) / CMP_BLOCK
    ov[:, ng - 1] = 0.0
    oh_lanes = -(-ns // LANES) * LANES
    onehot = (np.arange(T)[:, None] // SEL_BLOCK) == np.arange(oh_lanes)[None, :]
    gb = jnp.zeros((16, 1), F32).at[:NSA_HEADS * N_BRANCH, 0].set(gate_b)
    glog_blk = (4 * HEAD_DIM + 4 * W) // 16
    return pl.pallas_call(
        _nsa_attn_kernel,
        grid=(B, T // Q),
        in_specs=[
            pl.BlockSpec((1, W, Q), lambda b, i: (b, 1, i)),
            pl.BlockSpec((1, W, Q), lambda b, i: (b, 2, i)),
            pl.BlockSpec((1, 16, Q), lambda b, i: (b, glog_blk, i)),
            pl.BlockSpec((2 * ROPE_HALF, Q), lambda b, i: (0, i)),
            pl.BlockSpec((1, ng, LANES), lambda b, i: (b, 0, 0)),
            pl.BlockSpec((1, VAL_ROWS, ng), lambda b, i: (b, 0, 0)),
            pl.BlockSpec((1, T, LANES), lambda b, i: (b, 0, 0)),
            pl.BlockSpec((1, T // LANES, 2 * VAL_ROWS, LANES), lambda b, i: (b, 0, 0, 0)),
            pl.BlockSpec((1, T // MXU_DEPTH, VAL_ROWS, MXU_DEPTH), lambda b, i: (b, 0, 0, 0)),
            pl.BlockSpec((ns, ng), lambda b, i: (0, 0)),
            pl.BlockSpec((T, oh_lanes), lambda b, i: (0, 0)),
            pl.BlockSpec((16, 1), lambda b, i: (0, 0)),
            pl.BlockSpec((W, 1), lambda b, i: (0, 0)),
        ],
        out_specs=pl.BlockSpec((1, Q, W), lambda b, i: (b, i, 0)),
        out_shape=jax.ShapeDtypeStruct((B, T, W), BF16),
        scratch_shapes=[pltpu.VMEM((1, QH), F32), pltpu.VMEM((VAL_ROWS, QH), F32),
                        pltpu.VMEM((SEL_KEY_CHUNK, QH), F32), pltpu.VMEM((SEL_KEY_CHUNK, QH), F32),
                        pltpu.VMEM((1, QH), F32), pltpu.VMEM((1, QH), F32)],
        compiler_params=_params("parallel", "arbitrary"),
        name="nsa_attention",
    )(pt, pt, pt, rope, cmp_rm, cmp_t, kskw, vt, vs2, jnp.asarray(ov, BF16), jnp.asarray(onehot, BF16), gb,
      out_g.reshape(W, 1))


def _mem_kv_kernel(mem_ref, g_ref, w_ref, wt_ref, k_ref, vt_ref):
    x = mem_ref[0]
    h = x * lax.rsqrt(jnp.mean(x * x, axis=-1, keepdims=True) + NORM_EPS) * g_ref[...]
    hb = h.astype(BF16)
    k = _dot(hb, w_ref[...]) * (HEAD_DIM ** -0.5 * LOG2E)
    lane_head = lax.broadcasted_iota(jnp.int32, k.shape, 1) // HEAD_DIM
    for h in range(MEM_HEADS):
        k_ref[0, h] = jnp.where(lane_head == h, k, 0.0).astype(BF16)
    vt = _dot_nt(wt_ref[...], hb)
    ones = jnp.ones((ONES_ROWS, vt.shape[1]), F32)
    vt_ref[0] = jnp.concatenate(
        [part for h in range(MEM_HEADS) for part in (vt[h * HEAD_DIM:(h + 1) * HEAD_DIM], ones)],
        axis=0).astype(BF16)


def _mem_kv(mem, g, w_kv):
    B, M, D = mem.shape
    W = MEM_WIDTH
    const = lambda shape: pl.BlockSpec(shape, lambda b: (0,) * len(shape))
    return pl.pallas_call(
        _mem_kv_kernel,
        grid=(B,),
        in_specs=[pl.BlockSpec((1, M, D), lambda b: (b, 0, 0)), const((1, D)), const((D, W)),
                  const((W, D))],
        out_specs=[pl.BlockSpec((1, MEM_HEADS, M, W), lambda b: (b, 0, 0, 0)),
                   pl.BlockSpec((1, MEM_HEADS * VAL_ROWS, M), lambda b: (b, 0, 0))],
        out_shape=[jax.ShapeDtypeStruct((B, MEM_HEADS, M, W), BF16),
                   jax.ShapeDtypeStruct((B, MEM_HEADS * VAL_ROWS, M), BF16)],
        compiler_params=_params("parallel"),
        name="mem_kv",
    )(mem, g.reshape(1, D), w_kv[:, :W].astype(BF16), w_kv[:, W:].T.astype(BF16))


def _mem_attn_kernel(q_ref, gate_ref, k_ref, vt_ref, og_ref, y_ref):
    D, Hm = HEAD_DIM, MEM_HEADS
    q = q_ref[0]
    vt = vt_ref[0]
    ss = [_dot(k_ref[0, h], q) for h in range(Hm)]
    ebs = [jnp.exp2(s - jnp.max(s, axis=0, keepdims=True)).astype(BF16) for s in ss]
    ols = [_dot(vt[h * VAL_ROWS:(h + 1) * VAL_ROWS], eb) for h, eb in enumerate(ebs)]
    ys = []
    for ol in ols:
        o = ol[0:D] / ol[D:D + 1]
        ys.append(o * lax.rsqrt(jnp.mean(o * o, axis=0, keepdims=True) + NORM_EPS))
    y = jnp.concatenate(ys, axis=0) * og_ref[...] * _silu(gate_ref[0].astype(F32))
    y_ref[0] = y.T.astype(y_ref.dtype)


def _mem_attention(pt, mem_k, mem_vt, out_g, tm):
    B, _, T = pt.shape
    M = mem_k.shape[2]
    W = MEM_WIDTH
    return pl.pallas_call(
        _mem_attn_kernel,
        grid=(B, T // tm),
        in_specs=[
            pl.BlockSpec((1, W, tm), lambda b, i: (b, 3, i)),
            pl.BlockSpec((1, W, tm), lambda b, i: (b, 4, i)),
            pl.BlockSpec((1, MEM_HEADS, M, W), lambda b, i: (b, 0, 0, 0)),
            pl.BlockSpec((1, MEM_HEADS * VAL_ROWS, M), lambda b, i: (b, 0, 0)),
            pl.BlockSpec((W, 1), lambda b, i: (0, 0)),
        ],
        out_specs=pl.BlockSpec((1, tm, W), lambda b, i: (b, i, 0)),
        out_shape=jax.ShapeDtypeStruct((B, T, W), BF16),
        compiler_params=_params("parallel", "parallel"),
        name="mem_attention",
    )(pt, pt, mem_k, mem_vt, out_g.reshape(W, 1))


def _outproj_kernel(x_ref, yr_ref, yn_ref, ym_ref, wr_ref, wn_ref, wm_ref, g_ref, o_ref):
    z = (x_ref[0] + _dot(yr_ref[0], wr_ref[...]) + _dot(yn_ref[0], wn_ref[...])
         + _dot(ym_ref[0], wm_ref[...]))
    o_ref[0] = z * lax.rsqrt(jnp.mean(z * z, axis=-1, keepdims=True) + NORM_EPS) * g_ref[...]


def _output_projection(x, y_rwkv, y_nsa, y_mem, w_out, g, tm):
    B, T, D = x.shape
    wb = w_out.astype(BF16)
    w_r, w_n, w_m = wb[:RWKV_WIDTH], wb[RWKV_WIDTH:RWKV_WIDTH + NSA_WIDTH], wb[RWKV_WIDTH + NSA_WIDTH:]
    tile = lambda w: pl.BlockSpec((1, tm, w), lambda b, i: (b, i, 0))
    const = lambda shape: pl.BlockSpec(shape, lambda b, i: (0,) * len(shape))
    return pl.pallas_call(
        _outproj_kernel,
        grid=(B, T // tm),
        in_specs=[tile(D), tile(RWKV_WIDTH), tile(NSA_WIDTH), tile(MEM_WIDTH),
                  const((RWKV_WIDTH, D)), const((NSA_WIDTH, D)), const((MEM_WIDTH, D)), const((1, D))],
        out_specs=tile(D),
        out_shape=jax.ShapeDtypeStruct((B, T, D), F32),
        compiler_params=_params("parallel", "parallel"),
        name="output_projection",
    )(x, y_rwkv, y_nsa, y_mem, w_r, w_n, w_m, g.reshape(1, D))


def _rope_table(T):
    inv_freq = ROPE_THETA ** (-jnp.arange(ROPE_HALF, dtype=F32) / ROPE_HALF)
    ang = inv_freq[:, None] * jnp.arange(T).astype(F32)[None, :]
    return jnp.concatenate([jnp.cos(ang), jnp.sin(ang)], axis=0)


def _split_w_in(w):
    D = HEAD_DIM
    n0 = RWKV_COLS
    q, gate, glog = n0, n0 + NSA_WIDTH, n0 + 2 * NSA_WIDTH
    kc = glog + NSA_HEADS * N_BRANCH
    vc, ks, vs, kw, vw = kc + D, kc + 2 * D, kc + 3 * D, kc + 4 * D, kc + 5 * D
    m0 = vw + D
    cols = lambda a, n: w[:, a:a + n]
    w_row = jnp.concatenate([cols(0, n0), cols(kc, D), cols(vc, D)], axis=1)
    w_t = jnp.concatenate([cols(ks, D), cols(kw, D), cols(vs, D), cols(vw, D), cols(q, NSA_WIDTH),
                           cols(gate, NSA_WIDTH), cols(m0, MEM_WIDTH), cols(m0 + MEM_WIDTH, MEM_WIDTH),
                           cols(glog, NSA_HEADS * N_BRANCH),
                           jnp.zeros((w.shape[0], 16 - NSA_HEADS * N_BRANCH), w.dtype)], axis=1)
    return w_row.astype(BF16), w_t.T.astype(BF16)


def kernel(x, mem, norm_in_g, w_in, rwkv_mu, rwkv_w0, rwkv_w_up, rwkv_a0, rwkv_a_up, rwkv_k_k,
           rwkv_k_a, rwkv_r_k, rwkv_ln_w, rwkv_ln_b, nsa_cmp_pos, nsa_cmp_k_w1, nsa_cmp_k_w2,
           nsa_cmp_v_w1, nsa_cmp_v_w2, nsa_gate_b, nsa_out_g, mem_norm_g, w_mem_kv, mem_out_g, w_out,
           norm_final_g):
    B, T, D = x.shape
    assert w_in.shape[0] == 1, "single-layer stack: the final norm is fused into the output projection"
    rope = _rope_table(T)
    w_row, w_t = _split_w_in(w_in[0])
    p_row, kcvc, pt = _input_projection(x, norm_in_g[0].reshape(1, D), w_row, w_t, tm=512)
    y_rwkv = _rwkv_group(p_row, rwkv_mu[0], rwkv_w0[0], rwkv_w_up[0], rwkv_a0[0], rwkv_a_up[0],
                         rwkv_k_k[0], rwkv_k_a[0], rwkv_r_k[0].reshape(-1), rwkv_ln_w[0],
                         rwkv_ln_b[0])
    kskw, vt, vs2 = _nsa_keys(pt, rope, tk=2048)
    cmp_rm, cmp_t = _nsa_compress(kcvc, nsa_cmp_pos[0], nsa_cmp_k_w1[0], nsa_cmp_k_w2[0],
                                  nsa_cmp_v_w1[0], nsa_cmp_v_w2[0])
    y_nsa = _nsa_attention(pt, rope, cmp_rm, cmp_t, kskw, vt, vs2, nsa_gate_b[0], nsa_out_g[0])
    mem_k, mem_vt = _mem_kv(mem, mem_norm_g[0], w_mem_kv[0])
    y_mem = _mem_attention(pt, mem_k, mem_vt, mem_out_g[0], tm=1024)
    return _output_projection(x, y_rwkv, y_nsa, y_mem, w_out[0], norm_final_g, tm=1024)
```

```python
import numpy as np
import jax
import jax.numpy as jnp
from jax import lax
from jax.experimental import pallas as pl
from jax.experimental.pallas import tpu as pltpu

F32 = jnp.float32
BF16 = jnp.bfloat16

HEAD_DIM = 64
RWKV_HEADS = 8
RWKV_WIDTH = RWKV_HEADS * HEAD_DIM
LORA = 64
RWKV_COLS = 4 * RWKV_WIDTH + 2 * LORA
RWKV_GN_EPS = 64e-5
NSA_HEADS = 4
NSA_WIDTH = NSA_HEADS * HEAD_DIM
N_BRANCH = 3
CMP_BLOCK = 32
CMP_STRIDE = 16
SEL_BLOCK = 64
SEL_TOPK = 16
WINDOW = 512
MEM_HEADS = 4
MEM_WIDTH = MEM_HEADS * HEAD_DIM
ROPE_THETA = 500000.0
ROPE_HALF = 8
Q_BLOCK = 256
NORM_EPS = 1e-6
NEG_INF = -1e30
LOG2E = 1.4426950408889634
DECAY_SCALE = 0.6065306597126334

RWKV_CHUNK = 64
RWKV_HEAD_GROUP = 2
RWKV_BATCH_TILE = 4
SEL_KEY_CHUNK = 512
INPROJ_SUBTILE = 256
LANES = 128
MXU_DEPTH = 256
ONES_ROWS = 16
VAL_ROWS = HEAD_DIM + ONES_ROWS
VMEM_LIMIT = 48 * 1024 * 1024


def _dot(a, b):
    return jnp.dot(a, b, preferred_element_type=F32)


def _dot_nt(a, b):
    return lax.dot_general(a, b, (((1,), (1,)), ((), ())), preferred_element_type=F32)


def _dot_tn(a, b):
    return lax.dot_general(a, b, (((0,), (0,)), ((), ())), preferred_element_type=F32)


def _bf(a):
    return a.astype(BF16)


def _split_bf16(a, pieces):
    out = []
    for _ in range(pieces):
        part = a.astype(BF16)
        out.append(part)
        a = a - part.astype(F32)
    return out


def _sigmoid(x):
    return 0.5 * jnp.tanh(0.5 * x) + 0.5


def _silu(x):
    return x * _sigmoid(x)


def _params(*sem):
    return pltpu.CompilerParams(dimension_semantics=sem, vmem_limit_bytes=VMEM_LIMIT)


def _inproj_kernel(x_ref, g_ref, wrow_ref, wt_ref, prow_ref, kc_ref, pt_ref):
    tm = x_ref.shape[1]
    hbs = []
    for r0 in range(0, tm, INPROJ_SUBTILE):
        x = x_ref[0, r0:r0 + INPROJ_SUBTILE, :]
        h = x * lax.rsqrt(jnp.mean(x * x, axis=-1, keepdims=True) + NORM_EPS) * g_ref[...]
        hbs.append(h.astype(BF16))
    for k, hb in enumerate(hbs):
        rows = slice(k * INPROJ_SUBTILE, (k + 1) * INPROJ_SUBTILE)
        row = _dot(hb, wrow_ref[...])
        prow_ref[0, rows, :] = row[:, :RWKV_COLS]
        kc_ref[0, rows, :] = row[:, RWKV_COLS:]
        pt_ref[0, :, rows] = _dot_nt(wt_ref[...], hb).astype(pt_ref.dtype)


def _input_projection(x, g, w_row, w_t, tm):
    B, T, D = x.shape
    n_row = w_row.shape[1]
    n_t = w_t.shape[0]
    return pl.pallas_call(
        _inproj_kernel,
        grid=(B, T // tm),
        in_specs=[
            pl.BlockSpec((1, tm, D), lambda b, i: (b, i, 0)),
            pl.BlockSpec((1, D), lambda b, i: (0, 0)),
            pl.BlockSpec((D, n_row), lambda b, i: (0, 0)),
            pl.BlockSpec((n_t, D), lambda b, i: (0, 0)),
        ],
        out_specs=[
            pl.BlockSpec((1, tm, RWKV_COLS), lambda b, i: (b, i, 0)),
            pl.BlockSpec((1, tm, n_row - RWKV_COLS), lambda b, i: (b, i, 0)),
            pl.BlockSpec((1, n_t, tm), lambda b, i: (b, 0, i)),
        ],
        out_shape=[
            jax.ShapeDtypeStruct((B, T, RWKV_COLS), F32),
            jax.ShapeDtypeStruct((B, T, n_row - RWKV_COLS), F32),
            jax.ShapeDtypeStruct((B, n_t, T), BF16),
        ],
        compiler_params=_params("parallel", "parallel"),
        name="input_projection",
    )(x, g, w_row, w_t)


_X_R, _X_A, _X_K, _X_B, _X_V, _X_BV, _X_SG, _X_G = range(8)


def _rwkv_kernel(podd_ref, peven_ref, prevodd_ref, preveven_ref, mu_ref, w0_ref, wup_ref, a0_ref,
                 aup_ref, kk_ref, ka_ref, rk_ref, lnw_ref, lnb_ref, bd_ref, tri_ref, y_ref,
                 s_ref, xa_ref, xb_ref):
    C, W, N, H = RWKV_CHUNK, RWKV_WIDTH, HEAD_DIM, RWKV_HEADS
    HG = RWKV_HEAD_GROUP
    GW = HG * N
    n = pl.program_id(1)
    n_chunks = 2 * (pl.num_programs(1) - 1)
    bt = podd_ref.shape[0]

    @pl.when(n == 0)
    def _():
        s_ref[...] = jnp.zeros_like(s_ref)
        xa_ref[...] = jnp.zeros_like(xa_ref)
        xb_ref[...] = jnp.zeros_like(xb_ref)

    bd = bd_ref[...]
    head_sum = lambda a: jnp.concatenate(
        [_dot(_bf(a[:, g * GW:(g + 1) * GW]), bd) for g in range(H // HG)], axis=1)
    row = lax.broadcasted_iota(jnp.int32, (C, 1), 0)

    def prepare(b, chunk, p_ref, prev_ref, x_ref):
        p = p_ref[b]
        prev_last = jnp.where(chunk <= 0, 0.0, prev_ref[b][7:8, :])
        prev = jnp.where(row == 0, prev_last, pltpu.roll(p, 1, axis=0))
        pf = p + mu_ref[...] * (prev - p)
        r = pf[:, 0:W]
        k = pf[:, W:2 * W]
        v = pf[:, 2 * W:3 * W]
        gate = pf[:, 3 * W:4 * W]
        wd = _bf(jnp.tanh(pf[:, 4 * W:4 * W + LORA]))
        ad = _bf(pf[:, 4 * W + LORA:4 * W + 2 * LORA])
        yield
        z = w0_ref[...] + _dot(wd, wup_ref[...])
        logw = -DECAY_SCALE * _sigmoid(z)
        eta = _sigmoid(a0_ref[...] + _dot(ad, aup_ref[...]))
        kk = k * kk_ref[...]
        k2 = k * (eta * ka_ref[...] + (1.0 - ka_ref[...]))
        kk_sq = _bf(kk * kk)
        rk2 = _bf(r * k2 * rk_ref[...])
        logw_parts = _split_bf16(logw, 3)
        yield
        kk = kk * lax.rsqrt(jnp.maximum(head_sum(kk_sq), 1e-24))
        x_ref[b, _X_BV] = head_sum(rk2) * v
        x_ref[b, _X_SG] = _silu(gate)
        tri = tri_ref[...]
        cs = sum(_dot(tri, part) for part in logw_parts)
        mid = cs[C // 2 - 1:C // 2, :]
        csm = cs - mid
        end = cs[C - 1:C, :]
        x_ref[b, _X_G] = jnp.concatenate([jnp.exp(mid), jnp.exp(end), jnp.exp(end - mid),
                                          jnp.zeros((C - 3, W), F32)], axis=0)
        yield
        e_out = jnp.exp(-csm)
        x_ref[b, _X_R] = r * jnp.exp(csm)
        x_ref[b, _X_A] = -kk * jnp.exp(csm - logw)
        yield
        x_ref[b, _X_K] = k2 * e_out
        x_ref[b, _X_B] = kk * eta * e_out
        x_ref[b, _X_V] = v

    li = lax.broadcasted_iota(jnp.int32, (C, GW), 1) % N
    ti = lax.broadcasted_iota(jnp.int32, (C, GW), 0)
    strict, incl = ti > li, ti >= li
    eye = (ti == li).astype(F32)
    same_head = (lax.broadcasted_iota(jnp.int32, (GW, GW), 0) // N
                 == lax.broadcasted_iota(jnp.int32, (GW, GW), 1) // N)

    tile_heads = LANES // N
    pair_head = (lax.broadcasted_iota(jnp.int32, (LANES, LANES), 0) // N
                 == lax.broadcasted_iota(jnp.int32, (LANES, LANES), 1) // N)
    zero_tile = jnp.zeros((LANES, LANES), BF16)

    def blockdiag(a):
        a = _bf(a)
        n_tiles = GW // LANES
        rows = []
        for j in range(n_tiles):
            own = jnp.concatenate([a[:, j * LANES:(j + 1) * LANES]] * tile_heads, axis=0)
            own = jnp.where(pair_head, own, jnp.zeros((), BF16))
            rows.append(jnp.concatenate([own if k == j else zero_tile for k in range(n_tiles)], axis=1))
        return jnp.concatenate(rows, axis=0)

    def advance(chunk, x_ref, y_rows, fill):
        live = (chunk >= 0) & (chunk < n_chunks)
        chains = [(b, g) for b in range(bt) for g in range(H // HG)]
        st = {}
        for c in chains:
            b, g = c
            ls = slice(g * GW, (g + 1) * GW)
            kb = jnp.concatenate([blockdiag(x_ref[b, _X_K][:, ls]), blockdiag(x_ref[b, _X_B][:, ls])],
                                 axis=0)
            ar = _bf(jnp.concatenate([x_ref[b, _X_A][:, ls], x_ref[b, _X_R][:, ls]], axis=0))
            st[c] = dict(ls=ls, kb=kb, ar=ar, gc=_dot_nt(ar, kb))
            fill()
        for c in chains:
            b, g = c
            s = st[c]
            ls, gc = s["ls"], s["gc"]
            s0 = s_ref[b, g]
            gates = x_ref[b, _X_G]
            s.update(
                s0=s0, v_bd=blockdiag(x_ref[b, _X_V][:, ls]),
                g_end=gates[1:2, ls], g_end_mid=gates[2:3, ls],
                uy0=_dot_nt(s["ar"], _bf(s0 * gates[0:1, ls])),
                a_ak=jnp.where(strict, gc[0:C, 0:GW], 0.0), a_ab=jnp.where(strict, gc[0:C, GW:2 * GW], 0.0),
                a_rk=jnp.where(incl, gc[C:2 * C, 0:GW], 0.0), a_rb=jnp.where(incl, gc[C:2 * C, GW:2 * GW], 0.0))
            fill()
        for c in chains:
            s = st[c]
            s["rhs"] = s["uy0"][0:C] + _dot(_bf(s["a_ak"]), s["v_bd"])
            s["tm"] = eye + s["a_ab"]
            s["pw"] = _dot(_bf(s["a_ab"]), blockdiag(s["a_ab"]))
            fill()
        span = 2
        while span < C:
            for c in chains:
                s = st[c]
                if 2 * span < C:
                    tp = _dot(_bf(jnp.concatenate([s["tm"], s["pw"]], axis=0)), blockdiag(s["pw"]))
                    s["tm"] = s["tm"] + tp[0:C]
                    s["pw"] = tp[C:2 * C]
                else:
                    s["tm"] = s["tm"] + _dot(_bf(s["tm"]), blockdiag(s["pw"]))
                fill()
            span *= 2
        for c in chains:
            s = st[c]
            s["u"] = _dot(_bf(s["tm"]), blockdiag(s["rhs"]))
            fill()
        for c in chains:
            s = st[c]
            s["vu"] = jnp.concatenate([s["v_bd"], blockdiag(s["u"])], axis=0)
            s["y"] = s["uy0"][C:2 * C] + _dot(_bf(jnp.concatenate([s["a_rk"], s["a_rb"]], axis=1)), s["vu"])
            fill()
        for c in chains:
            b, g = c
            s = st[c]
            ls = s["ls"]
            vu_rows = jnp.concatenate([_bf(x_ref[b, _X_V][:, ls]), _bf(s["u"])], axis=0)
            kb_rows = _bf(jnp.concatenate([x_ref[b, _X_K][:, ls], x_ref[b, _X_B][:, ls]], axis=0))
            keep = jnp.where(live, s["g_end"], 1.0)
            gain = jnp.where(live, s["g_end_mid"], 0.0)
            s_ref[b, g] = s["s0"] * keep + jnp.where(same_head, _dot_tn(vu_rows, kb_rows) * gain, 0.0)
            fill()
        ys = [jnp.concatenate([st[(b, g)]["y"] for g in range(H // HG)], axis=1) for b in range(bt)]
        means = [head_sum(y) * (1.0 / N) for y in ys]
        fill()
        ys = [y - mean for y, mean in zip(ys, means)]
        vars_ = [head_sum(jnp.square(y)) * (1.0 / N) for y in ys]
        fill()
        for b, (y, var) in enumerate(zip(ys, vars_)):
            y = y * lax.rsqrt(var + RWKV_GN_EPS) * lnw_ref[...] + lnb_ref[...] + x_ref[b, _X_BV]
            y_ref[b, y_rows] = (y * x_ref[b, _X_SG]).astype(y_ref.dtype)
            fill()

    def filler(gens):
        gens = list(gens)

        def fill():
            while gens:
                g = gens.pop(0)
                try:
                    next(g)
                    gens.append(g)
                    return
                except StopIteration:
                    pass

        def drain():
            for g in gens:
                for _ in g:
                    pass

        return fill, drain

    fill, drain = filler(prepare(b, 2 * n - 1, podd_ref, prevodd_ref, xb_ref) for b in range(bt))
    advance(2 * n - 2, xa_ref, slice(0, C), fill)
    drain()
    fill, drain = filler(prepare(b, 2 * n, peven_ref, preveven_ref, xa_ref) for b in range(bt))
    advance(2 * n - 1, xb_ref, slice(C, 2 * C), fill)
    drain()


def _rwkv_group(p_row, mu, w0, w_up, a0, a_up, k_k, k_a, r_k, ln_w, ln_b):
    B, T, _ = p_row.shape
    C, W = RWKV_CHUNK, RWKV_WIDTH
    heads = np.arange(RWKV_HEAD_GROUP * HEAD_DIM) // HEAD_DIM
    bd = jnp.asarray(heads[:, None] == heads[None, :], BF16)
    tri = jnp.asarray(np.tril(np.ones((C, C), np.float32)), BF16)
    vec = lambda a: a.reshape(1, -1)
    const = lambda shape: pl.BlockSpec(shape, lambda b, n: (0,) * len(shape))
    bt = RWKV_BATCH_TILE if B % RWKV_BATCH_TILE == 0 else 1
    n_chunks = T // C
    assert n_chunks % 2 == 0
    odd = lambda n: jnp.maximum(2 * n - 1, 0)
    even = lambda n: jnp.minimum(2 * n, n_chunks - 1)
    before = lambda c: jnp.maximum(c * (C // 8) - 1, 0)
    return pl.pallas_call(
        _rwkv_kernel,
        grid=(B // bt, n_chunks // 2 + 1),
        in_specs=[
            pl.BlockSpec((bt, C, RWKV_COLS), lambda b, n: (b, odd(n), 0)),
            pl.BlockSpec((bt, C, RWKV_COLS), lambda b, n: (b, even(n), 0)),
            pl.BlockSpec((bt, 8, RWKV_COLS), lambda b, n: (b, before(odd(n)), 0)),
            pl.BlockSpec((bt, 8, RWKV_COLS), lambda b, n: (b, before(even(n)), 0)),
            const((1, RWKV_COLS)), const((1, W)), const((LORA, W)), const((1, W)), const((LORA, W)),
            const((1, W)), const((1, W)), const((1, W)), const((1, W)), const((1, W)),
            const(bd.shape), const((C, C)),
        ],
        out_specs=pl.BlockSpec((bt, 2 * C, W), lambda b, n: (b, jnp.maximum(n - 1, 0), 0)),
        out_shape=jax.ShapeDtypeStruct((B, T, W), BF16),
        scratch_shapes=[pltpu.VMEM((bt, RWKV_HEADS // RWKV_HEAD_GROUP) + (RWKV_HEAD_GROUP * HEAD_DIM,) * 2,
                                   F32),
                        pltpu.VMEM((bt, 8, C, W), F32), pltpu.VMEM((bt, 8, C, W), F32)],
        compiler_params=_params("parallel", "arbitrary"),
        name="rwkv7_group",
    )(p_row, p_row, p_row, p_row, vec(mu), vec(w0), _bf(w_up), vec(a0), _bf(a_up), vec(k_k), vec(k_a),
      vec(r_k), vec(ln_w), vec(ln_b), bd, tri)


def _rope_rows(x, cos, sin):
    x1, x2 = x[0:ROPE_HALF], x[ROPE_HALF:2 * ROPE_HALF]
    return jnp.concatenate([x1 * cos - x2 * sin, x2 * cos + x1 * sin, x[2 * ROPE_HALF:]], axis=0)


def _nsa_keys_kernel(kv_ref, rope_ref, kskw_ref, vt_ref, vs2_ref):
    kv = kv_ref[0].astype(F32)
    cos, sin = rope_ref[0:ROPE_HALF], rope_ref[ROPE_HALF:2 * ROPE_HALF]
    keys = jnp.concatenate([_rope_rows(kv[0:HEAD_DIM], cos, sin),
                            _rope_rows(kv[HEAD_DIM:2 * HEAD_DIM], cos, sin)], axis=0)
    kskw_ref[0] = keys.T.astype(BF16)
    vals = kv[2 * HEAD_DIM:4 * HEAD_DIM].astype(BF16)
    ones = jnp.ones((ONES_ROWS, LANES), BF16)
    for u in range(vals.shape[1] // LANES):
        blk = vals[:, u * LANES:(u + 1) * LANES]
        vt_ref[0, u] = jnp.concatenate([blk[0:HEAD_DIM], ones, blk[HEAD_DIM:2 * HEAD_DIM], ones], axis=0)
    ones2 = jnp.ones((ONES_ROWS, MXU_DEPTH), BF16)
    for u in range(vals.shape[1] // MXU_DEPTH):
        vs2_ref[0, u] = jnp.concatenate([vals[0:HEAD_DIM, u * MXU_DEPTH:(u + 1) * MXU_DEPTH], ones2], axis=0)


def _nsa_keys(pt, rope, tk):
    B, _, T = pt.shape
    return pl.pallas_call(
        _nsa_keys_kernel,
        grid=(B, T // tk),
        in_specs=[
            pl.BlockSpec((1, 4 * HEAD_DIM, tk), lambda b, i: (b, 0, i)),
            pl.BlockSpec((2 * ROPE_HALF, tk), lambda b, i: (0, i)),
        ],
        out_specs=[
            pl.BlockSpec((1, tk, LANES), lambda b, i: (b, i, 0)),
            pl.BlockSpec((1, tk // LANES, 2 * VAL_ROWS, LANES), lambda b, i: (b, i, 0, 0)),
            pl.BlockSpec((1, tk // MXU_DEPTH, VAL_ROWS, MXU_DEPTH), lambda b, i: (b, i, 0, 0)),
        ],
        out_shape=[
            jax.ShapeDtypeStruct((B, T, LANES), BF16),
            jax.ShapeDtypeStruct((B, T // LANES, 2 * VAL_ROWS, LANES), BF16),
            jax.ShapeDtypeStruct((B, T // MXU_DEPTH, VAL_ROWS, MXU_DEPTH), BF16),
        ],
        compiler_params=_params("parallel", "parallel"),
        name="nsa_keys",
    )(pt, rope)


def _nsa_compress_kernel(g_ref, wc_ref, pos_ref, w2_ref, w2t_ref, rm_ref, t_ref):
    half = CMP_BLOCK // 2
    ng = g_ref.shape[1] // half
    wc = wc_ref[...]
    m = sum(_dot(g_ref[0, pl.ds(l, ng, stride=half), :].astype(BF16), wc[l * LANES:(l + 1) * LANES])
            for l in range(half))
    pm = _dot(pos_ref[...], wc)
    pos_term = pm[0:1, 0:LANES] + pm[1:2, LANES:2 * LANES]
    pre = m[:, 0:LANES] + pltpu.roll(m[:, LANES:2 * LANES], ng - 1, axis=0) + pos_term
    act = _silu(pre).astype(BF16)
    row = lax.broadcasted_iota(jnp.int32, (ng, 1), 0)
    col = lax.broadcasted_iota(jnp.int32, (1, ng), 1)
    rm_ref[0] = jnp.where(row < ng - 1, _dot(act, w2_ref[...]), 0.0).astype(BF16)
    vt = jnp.where(col < ng - 1, _dot_nt(w2t_ref[...], act)[HEAD_DIM:2 * HEAD_DIM], 0.0)
    t_ref[0] = jnp.concatenate([vt, jnp.ones((ONES_ROWS, ng), F32)], axis=0).astype(BF16)


def _nsa_compress(kcvc, cmp_pos, k_w1, k_w2, v_w1, v_w2):
    B, T, _ = kcvc.shape
    ng = T // CMP_STRIDE
    half = CMP_BLOCK // 2
    D = HEAD_DIM

    def spread(w, second, is_v):
        blk = w[second * half * D:(second + 1) * half * D].reshape(half, D, D)
        z = jnp.zeros_like(blk)
        return jnp.concatenate([z, blk] if is_v else [blk, z], axis=1).reshape(half * LANES, D)

    wc = jnp.concatenate([spread(k_w1, 0, False), spread(v_w1, 0, True),
                          spread(k_w1, 1, False), spread(v_w1, 1, True)], axis=1).astype(BF16)
    pos2 = jnp.concatenate([cmp_pos, cmp_pos], axis=1)
    pos = jnp.zeros((8, half * LANES), F32)
    pos = pos.at[0].set(pos2[:half].reshape(-1)).at[1].set(pos2[half:].reshape(-1)).astype(BF16)
    z = jnp.zeros((D, D), F32)
    w2 = jnp.block([[k_w2, z], [z, v_w2]])
    const = lambda shape: pl.BlockSpec(shape, lambda b: (0,) * len(shape))
    return pl.pallas_call(
        _nsa_compress_kernel,
        grid=(B,),
        in_specs=[
            pl.BlockSpec((1, T, LANES), lambda b: (b, 0, 0)),
            const((half * LANES, 2 * LANES)), const((8, half * LANES)),
            const((LANES, LANES)), const((LANES, LANES)),
        ],
        out_specs=[
            pl.BlockSpec((1, ng, LANES), lambda b: (b, 0, 0)),
            pl.BlockSpec((1, VAL_ROWS, ng), lambda b: (b, 0, 0)),
        ],
        out_shape=[
            jax.ShapeDtypeStruct((B, ng, LANES), BF16),
            jax.ShapeDtypeStruct((B, VAL_ROWS, ng), BF16),
        ],
        compiler_params=_params("parallel"),
        name="nsa_compress",
    )(kcvc, wc, pos, w2.astype(BF16), w2.T.astype(BF16))


def _tile_heads(x):
    return jnp.concatenate([x] * NSA_HEADS, axis=1)


def _nsa_attn_kernel(q_ref, gate_ref, glog_ref, rope_ref, rm_ref, ct_ref, kskw_ref, vt_ref, vs2_ref,
                     ov_ref, oh_ref, gb_ref, og_ref, y_ref, m_ref, acc_ref, sa_ref, sb_ref, ma_ref,
                     mb_ref):
    D, Q, Hn = HEAD_DIM, Q_BLOCK, NSA_HEADS
    KC = SEL_KEY_CHUNK
    ng = rm_ref.shape[1]
    ns = ov_ref.shape[0]
    n_top = min(SEL_TOPK, ns)
    i = pl.program_id(1)
    t0 = i * Q
    tq = t0 + lax.broadcasted_iota(jnp.int32, (1, Q), 1)

    q = q_ref[0].astype(F32) * (D ** -0.5 * LOG2E)
    cos, sin = rope_ref[0:ROPE_HALF], rope_ref[ROPE_HALF:2 * ROPE_HALF]
    qh = [q[h * D:(h + 1) * D] for h in range(Hn)]
    q4 = jnp.concatenate(qh, axis=1)
    q4r = jnp.concatenate([_rope_rows(x, cos, sin) for x in qh], axis=1)
    zero = jnp.zeros_like(q4)
    q_lo = jnp.concatenate([q4, zero], axis=0).astype(BF16)
    qr_lo = jnp.concatenate([q4r, zero], axis=0).astype(BF16)
    qr_hi = jnp.concatenate([zero, q4r], axis=0).astype(BF16)

    def masked(s, bias):
        return jnp.concatenate([s[:, h * Q:(h + 1) * Q] + bias for h in range(Hn)], axis=1)


    WK = WINDOW + Q
    w0 = pl.multiple_of(jnp.maximum(t0 - WINDOW, 0), LANES)
    diff = tq - (w0 + lax.broadcasted_iota(jnp.int32, (WK, 1), 0))
    w_bias = jnp.where((diff >= 0) & (diff < WINDOW), 0.0, NEG_INF)

    o_w = []

    def window_stages():
        ss = []
        for h in range(Hn):
            ss.append(_dot(kskw_ref[0, pl.ds(w0, WK), :], qr_hi[:, h * Q:(h + 1) * Q]) + w_bias)
            yield
        ebs = []
        for s in ss:
            ebs.append(jnp.exp2(s - jnp.max(s, axis=0, keepdims=True)).astype(BF16))
            yield
        for eb in ebs:
            ol = _dot(vt_ref[0, w0 // LANES][VAL_ROWS:2 * VAL_ROWS], eb[0:LANES])
            for u in range(1, WK // LANES):
                ol = ol + _dot(vt_ref[0, w0 // LANES + u][VAL_ROWS:2 * VAL_ROWS], eb[u * LANES:(u + 1) * LANES])
            o_w.append(ol[0:D] / jnp.maximum(ol[D:D + 1], 1e-30))
            yield

    cend = lax.broadcasted_iota(jnp.int32, (ng, 1), 0) * CMP_STRIDE + (CMP_BLOCK - 1)
    s = masked(_dot(rm_ref[0], q_lo), jnp.where(cend <= tq, 0.0, NEG_INF))
    eb = jnp.exp2(s - jnp.max(s, axis=0, keepdims=True)).astype(BF16)
    ol = _dot(ct_ref[0], eb)
    seen = _tile_heads((tq >= CMP_BLOCK - 1).astype(F32))
    inv = seen / jnp.maximum(ol[D:D + 1], 1e-30)
    o_c = ol[0:D] * inv
    psum = sum(eb[:, h * Q:(h + 1) * Q] * inv[:, h * Q:(h + 1) * Q] for h in range(Hn))
    imp = _dot(ov_ref[...], psum.astype(BF16))

    blk = lax.broadcasted_iota(jnp.int32, (ns, Q), 0)
    cur = tq // SEL_BLOCK
    forced = (blk == 0) | (blk == cur) | (blk == cur - 1)
    taken = -2.0
    score = jnp.where(forced, taken, jnp.where(blk <= cur, imp, -1.0))
    window = window_stages()
    for _ in range(max(n_top - 3, 0)):
        best = jnp.max(score, axis=0, keepdims=True)
        first = jnp.min(jnp.where(score == best, blk, ns), axis=0, keepdims=True)
        score = jnp.where(blk == first, taken, score)
        next(window, None)
    for _ in window:
        pass
    o_w = jnp.concatenate(o_w, axis=1)

    def with_block_mask(keep):
        bias = _tile_heads(jnp.where(keep, 0.0, NEG_INF))
        pad = oh_ref.shape[1] - ns
        if pad:
            bias = jnp.concatenate([bias, jnp.zeros((pad, Hn * Q), F32)], axis=0)
        return jnp.concatenate([qr_lo, bias.astype(BF16)], axis=0)

    chosen = (score == taken) & (blk <= cur)
    q_own = with_block_mask(chosen)
    q_sel = with_block_mask(chosen & (blk < t0 // SEL_BLOCK))

    def sel_scores(j, s_ref, cmax_ref):
        k0 = pl.multiple_of(j * KC, KC)
        keys = jnp.concatenate([kskw_ref[0, pl.ds(k0, KC), :], oh_ref[pl.ds(k0, KC), :]], axis=1)
        s = _dot(keys, q_sel)
        s_ref[...] = s
        cmax_ref[...] = jnp.max(s, axis=0, keepdims=True)

    UB = KC // MXU_DEPTH

    def sel_update(j, s_ref, cmax_ref):
        m_old = m_ref[...]
        m_new = jnp.maximum(m_old, cmax_ref[...])
        eb = jnp.exp2(s_ref[...] - m_new).astype(BF16)
        pv = acc_ref[...] * jnp.exp2(m_old - m_new)
        for u in range(UB):
            pv = pv + _dot(vs2_ref[0, j * UB + u], eb[u * MXU_DEPTH:(u + 1) * MXU_DEPTH])
        acc_ref[...] = pv
        m_ref[...] = m_new

    n_chunks = (t0 + KC - 1) // KC
    last = kskw_ref.shape[1] // KC - 1
    own = pl.ds(pl.multiple_of(t0, Q), Q)
    s = _dot(jnp.concatenate([kskw_ref[0, own, :], oh_ref[own, :]], axis=1), q_own)
    sel_scores(0, sa_ref, ma_ref)
    lk = lax.broadcasted_iota(jnp.int32, (Q, 1), 0)
    lq = lax.broadcasted_iota(jnp.int32, (1, Q), 1)
    s = masked(s, jnp.where(lk <= lq, 0.0, NEG_INF))
    m0 = jnp.max(s, axis=0, keepdims=True)
    m_ref[...] = m0
    e0 = jnp.exp2(s - m0).astype(BF16)
    QB = Q // MXU_DEPTH
    acc_ref[...] = sum(_dot(vs2_ref[0, i * QB + u], e0[u * MXU_DEPTH:(u + 1) * MXU_DEPTH]) for u in range(QB))

    def chunk_pair(j, carry):
        sel_scores(jnp.minimum(2 * j + 1, last), sb_ref, mb_ref)
        sel_update(2 * j, sa_ref, ma_ref)
        sel_scores(jnp.minimum(2 * j + 2, last), sa_ref, ma_ref)
        sel_update(jnp.minimum(2 * j + 1, last), sb_ref, mb_ref)
        return carry

    lax.fori_loop(0, (n_chunks + 1) // 2, chunk_pair, 0)
    acc = acc_ref[...]
    o_s = acc[0:D] / jnp.maximum(acc[D:D + 1], 1e-30)

    gl = _sigmoid(glog_ref[0].astype(F32) + gb_ref[...])
    ys = []
    for h in range(Hn):
        cs = slice(h * Q, (h + 1) * Q)
        o = (gl[3 * h:3 * h + 1] * o_c[:, cs] + gl[3 * h + 1:3 * h + 2] * o_s[:, cs]
             + gl[3 * h + 2:3 * h + 3] * o_w[:, cs])
        o = o * lax.rsqrt(jnp.mean(o * o, axis=0, keepdims=True) + NORM_EPS)
        ys.append(o)
    y = jnp.concatenate(ys, axis=0) * og_ref[...] * _silu(gate_ref[0].astype(F32))
    y_ref[0] = y.T.astype(y_ref.dtype)


def _nsa_attention(pt, rope, cmp_rm, cmp_t, kskw, vt, vs2, gate_b, out_g):
    B, _, T = pt.shape
    ng = T // CMP_STRIDE
    ns = T // SEL_BLOCK
    Q, W = Q_BLOCK, NSA_WIDTH
    QH = Q * NSA_HEADS
    c0 = np.arange(ng)[None, :] * CMP_STRIDE
    s0 = np.arange(ns)[:, None] * SEL_BLOCK
    ov = np.clip(np.minimum(c0 + CMP_BLOCK, s0 + SEL_BLOCK) - np.maximum(c0, s0), 0, None) / CMP_BLOCK
    ov[:, ng - 1] = 0.0
    oh_lanes = -(-ns // LANES) * LANES
    onehot = (np.arange(T)[:, None] // SEL_BLOCK) == np.arange(oh_lanes)[None, :]
    gb = jnp.zeros((16, 1), F32).at[:NSA_HEADS * N_BRANCH, 0].set(gate_b)
    glog_blk = (4 * HEAD_DIM + 4 * W) // 16
    return pl.pallas_call(
        _nsa_attn_kernel,
        grid=(B, T // Q),
        in_specs=[
            pl.BlockSpec((1, W, Q), lambda b, i: (b, 1, i)),
            pl.BlockSpec((1, W, Q), lambda b, i: (b, 2, i)),
            pl.BlockSpec((1, 16, Q), lambda b, i: (b, glog_blk, i)),
            pl.BlockSpec((2 * ROPE_HALF, Q), lambda b, i: (0, i)),
            pl.BlockSpec((1, ng, LANES), lambda b, i: (b, 0, 0)),
            pl.BlockSpec((1, VAL_ROWS, ng), lambda b, i: (b, 0, 0)),
            pl.BlockSpec((1, T, LANES), lambda b, i: (b, 0, 0)),
            pl.BlockSpec((1, T // LANES, 2 * VAL_ROWS, LANES), lambda b, i: (b, 0, 0, 0)),
            pl.BlockSpec((1, T // MXU_DEPTH, VAL_ROWS, MXU_DEPTH), lambda b, i: (b, 0, 0, 0)),
            pl.BlockSpec((ns, ng), lambda b, i: (0, 0)),
            pl.BlockSpec((T, oh_lanes), lambda b, i: (0, 0)),
            pl.BlockSpec((16, 1), lambda b, i: (0, 0)),
            pl.BlockSpec((W, 1), lambda b, i: (0, 0)),
        ],
        out_specs=pl.BlockSpec((1, Q, W), lambda b, i: (b, i, 0)),
        out_shape=jax.ShapeDtypeStruct((B, T, W), BF16),
        scratch_shapes=[pltpu.VMEM((1, QH), F32), pltpu.VMEM((VAL_ROWS, QH), F32),
                        pltpu.VMEM((SEL_KEY_CHUNK, QH), F32), pltpu.VMEM((SEL_KEY_CHUNK, QH), F32),
                        pltpu.VMEM((1, QH), F32), pltpu.VMEM((1, QH), F32)],
        compiler_params=_params("parallel", "arbitrary"),
        name="nsa_attention",
    )(pt, pt, pt, rope, cmp_rm, cmp_t, kskw, vt, vs2, jnp.asarray(ov, BF16), jnp.asarray(onehot, BF16), gb,
      out_g.reshape(W, 1))


def _mem_kv_kernel(mem_ref, g_ref, w_ref, wt_ref, k_ref, vt_ref):
    x = mem_ref[0]
    h = x * lax.rsqrt(jnp.mean(x * x, axis=-1, keepdims=True) + NORM_EPS) * g_ref[...]
    hb = h.astype(BF16)
    k = _dot(hb, w_ref[...]) * (HEAD_DIM ** -0.5 * LOG2E)
    lane_head = lax.broadcasted_iota(jnp.int32, k.shape, 1) // HEAD_DIM
    for h in range(MEM_HEADS):
        k_ref[0, h] = jnp.where(lane_head == h, k, 0.0).astype(BF16)
    vt = _dot_nt(wt_ref[...], hb)
    ones = jnp.ones((ONES_ROWS, vt.shape[1]), F32)
    vt_ref[0] = jnp.concatenate(
        [part for h in range(MEM_HEADS) for part in (vt[h * HEAD_DIM:(h + 1) * HEAD_DIM], ones)],
        axis=0).astype(BF16)


def _mem_kv(mem, g, w_kv):
    B, M, D = mem.shape
    W = MEM_WIDTH
    const = lambda shape: pl.BlockSpec(shape, lambda b: (0,) * len(shape))
    return pl.pallas_call(
        _mem_kv_kernel,
        grid=(B,),
        in_specs=[pl.BlockSpec((1, M, D), lambda b: (b, 0, 0)), const((1, D)), const((D, W)),
                  const((W, D))],
        out_specs=[pl.BlockSpec((1, MEM_HEADS, M, W), lambda b: (b, 0, 0, 0)),
                   pl.BlockSpec((1, MEM_HEADS * VAL_ROWS, M), lambda b: (b, 0, 0))],
        out_shape=[jax.ShapeDtypeStruct((B, MEM_HEADS, M, W), BF16),
                   jax.ShapeDtypeStruct((B, MEM_HEADS * VAL_ROWS, M), BF16)],
        compiler_params=_params("parallel"),
        name="mem_kv",
    )(mem, g.reshape(1, D), w_kv[:, :W].astype(BF16), w_kv[:, W:].T.astype(BF16))


def _mem_attn_kernel(q_ref, gate_ref, k_ref, vt_ref, og_ref, y_ref):
    D, Hm = HEAD_DIM, MEM_HEADS
    q = q_ref[0]
    vt = vt_ref[0]
    ss = [_dot(k_ref[0, h], q) for h in range(Hm)]
    ebs = [jnp.exp2(s - jnp.max(s, axis=0, keepdims=True)).astype(BF16) for s in ss]
    ols = [_dot(vt[h * VAL_ROWS:(h + 1) * VAL_ROWS], eb) for h, eb in enumerate(ebs)]
    ys = []
    for ol in ols:
        o = ol[0:D] / ol[D:D + 1]
        ys.append(o * lax.rsqrt(jnp.mean(o * o, axis=0, keepdims=True) + NORM_EPS))
    y = jnp.concatenate(ys, axis=0) * og_ref[...] * _silu(gate_ref[0].astype(F32))
    y_ref[0] = y.T.astype(y_ref.dtype)


def _mem_attention(pt, mem_k, mem_vt, out_g, tm):
    B, _, T = pt.shape
    M = mem_k.shape[2]
    W = MEM_WIDTH
    return pl.pallas_call(
        _mem_attn_kernel,
        grid=(B, T // tm),
        in_specs=[
            pl.BlockSpec((1, W, tm), lambda b, i: (b, 3, i)),
            pl.BlockSpec((1, W, tm), lambda b, i: (b, 4, i)),
            pl.BlockSpec((1, MEM_HEADS, M, W), lambda b, i: (b, 0, 0, 0)),
            pl.BlockSpec((1, MEM_HEADS * VAL_ROWS, M), lambda b, i: (b, 0, 0)),
            pl.BlockSpec((W, 1), lambda b, i: (0, 0)),
        ],
        out_specs=pl.BlockSpec((1, tm, W), lambda b, i: (b, i, 0)),
        out_shape=jax.ShapeDtypeStruct((B, T, W), BF16),
        compiler_params=_params("parallel", "parallel"),
        name="mem_attention",
    )(pt, pt, mem_k, mem_vt, out_g.reshape(W, 1))


def _outproj_kernel(x_ref, yr_ref, yn_ref, ym_ref, wr_ref, wn_ref, wm_ref, g_ref, o_ref):
    z = (x_ref[0] + _dot(yr_ref[0], wr_ref[...]) + _dot(yn_ref[0], wn_ref[...])
         + _dot(ym_ref[0], wm_ref[...]))
    o_ref[0] = z * lax.rsqrt(jnp.mean(z * z, axis=-1, keepdims=True) + NORM_EPS) * g_ref[...]


def _output_projection(x, y_rwkv, y_nsa, y_mem, w_out, g, tm):
    B, T, D = x.shape
    wb = w_out.astype(BF16)
    w_r, w_n, w_m = wb[:RWKV_WIDTH], wb[RWKV_WIDTH:RWKV_WIDTH + NSA_WIDTH], wb[RWKV_WIDTH + NSA_WIDTH:]
    tile = lambda w: pl.BlockSpec((1, tm, w), lambda b, i: (b, i, 0))
    const = lambda shape: pl.BlockSpec(shape, lambda b, i: (0,) * len(shape))
    return pl.pallas_call(
        _outproj_kernel,
        grid=(B, T // tm),
        in_specs=[tile(D), tile(RWKV_WIDTH), tile(NSA_WIDTH), tile(MEM_WIDTH),
                  const((RWKV_WIDTH, D)), const((NSA_WIDTH, D)), const((MEM_WIDTH, D)), const((1, D))],
        out_specs=tile(D),
        out_shape=jax.ShapeDtypeStruct((B, T, D), F32),
        compiler_params=_params("parallel", "parallel"),
        name="output_projection",
    )(x, y_rwkv, y_nsa, y_mem, w_r, w_n, w_m, g.reshape(1, D))


def _rope_table(T):
    inv_freq = ROPE_THETA ** (-jnp.arange(ROPE_HALF, dtype=F32) / ROPE_HALF)
    ang = inv_freq[:, None] * jnp.arange(T).astype(F32)[None, :]
    return jnp.concatenate([jnp.cos(ang), jnp.sin(ang)], axis=0)


def _split_w_in(w):
    D = HEAD_DIM
    n0 = RWKV_COLS
    q, gate, glog = n0, n0 + NSA_WIDTH, n0 + 2 * NSA_WIDTH
    kc = glog + NSA_HEADS * N_BRANCH
    vc, ks, vs, kw, vw = kc + D, kc + 2 * D, kc + 3 * D, kc + 4 * D, kc + 5 * D
    m0 = vw + D
    cols = lambda a, n: w[:, a:a + n]
    w_row = jnp.concatenate([cols(0, n0), cols(kc, D), cols(vc, D)], axis=1)
    w_t = jnp.concatenate([cols(ks, D), cols(kw, D), cols(vs, D), cols(vw, D), cols(q, NSA_WIDTH),
                           cols(gate, NSA_WIDTH), cols(m0, MEM_WIDTH), cols(m0 + MEM_WIDTH, MEM_WIDTH),
                           cols(glog, NSA_HEADS * N_BRANCH),
                           jnp.zeros((w.shape[0], 16 - NSA_HEADS * N_BRANCH), w.dtype)], axis=1)
    return w_row.astype(BF16), w_t.T.astype(BF16)


def kernel(x, mem, norm_in_g, w_in, rwkv_mu, rwkv_w0, rwkv_w_up, rwkv_a0, rwkv_a_up, rwkv_k_k,
           rwkv_k_a, rwkv_r_k, rwkv_ln_w, rwkv_ln_b, nsa_cmp_pos, nsa_cmp_k_w1, nsa_cmp_k_w2,
           nsa_cmp_v_w1, nsa_cmp_v_w2, nsa_gate_b, nsa_out_g, mem_norm_g, w_mem_kv, mem_out_g, w_out,
           norm_final_g):
    B, T, D = x.shape
    assert w_in.shape[0] == 1, "single-layer stack: the final norm is fused into the output projection"
    rope = _rope_table(T)
    w_row, w_t = _split_w_in(w_in[0])
    p_row, kcvc, pt = _input_projection(x, norm_in_g[0].reshape(1, D), w_row, w_t, tm=512)
    y_rwkv = _rwkv_group(p_row, rwkv_mu[0], rwkv_w0[0], rwkv_w_up[0], rwkv_a0[0], rwkv_a_up[0],
                         rwkv_k_k[0], rwkv_k_a[0], rwkv_r_k[0].reshape(-1), rwkv_ln_w[0],
                         rwkv_ln_b[0])
    kskw, vt, vs2 = _nsa_keys(pt, rope, tk=2048)
    cmp_rm, cmp_t = _nsa_compress(kcvc, nsa_cmp_pos[0], nsa_cmp_k_w1[0], nsa_cmp_k_w2[0],
                                  nsa_cmp_v_w1[0], nsa_cmp_v_w2[0])
    y_nsa = _nsa_attention(pt, rope, cmp_rm, cmp_t, kskw, vt, vs2, nsa_gate_b[0], nsa_out_g[0])
    mem_k, mem_vt = _mem_kv(mem, mem_norm_g[0], w_mem_kv[0])
    y_mem = _mem_attention(pt, mem_k, mem_vt, mem_out_g[0], tm=1024)
    return _output_projection(x, y_rwkv, y_nsa, y_mem, w_out[0], norm_final_g, tm=1024)
```

```python
import numpy as np
import jax
import jax.numpy as jnp
from jax import lax
from jax.experimental import pallas as pl
from jax.experimental.pallas import tpu as pltpu

F32 = jnp.float32
BF16 = jnp.bfloat16

HEAD_DIM = 64
RWKV_HEADS = 8
RWKV_WIDTH = RWKV_HEADS * HEAD_DIM
LORA = 64
RWKV_COLS = 4 * RWKV_WIDTH + 2 * LORA
RWKV_GN_EPS = 64e-5
NSA_HEADS = 4
NSA_WIDTH = NSA_HEADS * HEAD_DIM
N_BRANCH = 3
CMP_BLOCK = 32
CMP_STRIDE = 16
SEL_BLOCK = 64
SEL_TOPK = 16
WINDOW = 512
MEM_HEADS = 4
MEM_WIDTH = MEM_HEADS * HEAD_DIM
ROPE_THETA = 500000.0
ROPE_HALF = 8
Q_BLOCK = 256
NORM_EPS = 1e-6
NEG_INF = -1e30
LOG2E = 1.4426950408889634
DECAY_SCALE = 0.6065306597126334

RWKV_CHUNK = 64
RWKV_HEAD_GROUP = 2
RWKV_BATCH_TILE = 4
SEL_KEY_CHUNK = 512
INPROJ_SUBTILE = 256
LANES = 128
MXU_DEPTH = 256
ONES_ROWS = 16
VAL_ROWS = HEAD_DIM + ONES_ROWS
VMEM_LIMIT = 48 * 1024 * 1024


def _dot(a, b):
    return jnp.dot(a, b, preferred_element_type=F32)


def _dot_nt(a, b):
    return lax.dot_general(a, b, (((1,), (1,)), ((), ())), preferred_element_type=F32)


def _dot_tn(a, b):
    return lax.dot_general(a, b, (((0,), (0,)), ((), ())), preferred_element_type=F32)


def _bf(a):
    return a.astype(BF16)


def _split_bf16(a, pieces):
    out = []
    for _ in range(pieces):
        part = a.astype(BF16)
        out.append(part)
        a = a - part.astype(F32)
    return out


def _sigmoid(x):
    return 0.5 * jnp.tanh(0.5 * x) + 0.5


def _silu(x):
    return x * _sigmoid(x)


def _params(*sem):
    return pltpu.CompilerParams(dimension_semantics=sem, vmem_limit_bytes=VMEM_LIMIT)


def _inproj_kernel(x_ref, g_ref, wrow_ref, wt_ref, prow_ref, kc_ref, pt_ref):
    tm = x_ref.shape[1]
    hbs = []
    for r0 in range(0, tm, INPROJ_SUBTILE):
        x = x_ref[0, r0:r0 + INPROJ_SUBTILE, :]
        h = x * lax.rsqrt(jnp.mean(x * x, axis=-1, keepdims=True) + NORM_EPS) * g_ref[...]
        hbs.append(h.astype(BF16))
    for k, hb in enumerate(hbs):
        rows = slice(k * INPROJ_SUBTILE, (k + 1) * INPROJ_SUBTILE)
        row = _dot(hb, wrow_ref[...])
        prow_ref[0, rows, :] = row[:, :RWKV_COLS]
        kc_ref[0, rows, :] = row[:, RWKV_COLS:]
        pt_ref[0, :, rows] = _dot_nt(wt_ref[...], hb).astype(pt_ref.dtype)


def _input_projection(x, g, w_row, w_t, tm):
    B, T, D = x.shape
    n_row = w_row.shape[1]
    n_t = w_t.shape[0]
    return pl.pallas_call(
        _inproj_kernel,
        grid=(B, T // tm),
        in_specs=[
            pl.BlockSpec((1, tm, D), lambda b, i: (b, i, 0)),
            pl.BlockSpec((1, D), lambda b, i: (0, 0)),
            pl.BlockSpec((D, n_row), lambda b, i: (0, 0)),
            pl.BlockSpec((n_t, D), lambda b, i: (0, 0)),
        ],
        out_specs=[
            pl.BlockSpec((1, tm, RWKV_COLS), lambda b, i: (b, i, 0)),
            pl.BlockSpec((1, tm, n_row - RWKV_COLS), lambda b, i: (b, i, 0)),
            pl.BlockSpec((1, n_t, tm), lambda b, i: (b, 0, i)),
        ],
        out_shape=[
            jax.ShapeDtypeStruct((B, T, RWKV_COLS), F32),
            jax.ShapeDtypeStruct((B, T, n_row - RWKV_COLS), F32),
            jax.ShapeDtypeStruct((B, n_t, T), BF16),
        ],
        compiler_params=_params("parallel", "parallel"),
        name="input_projection",
    )(x, g, w_row, w_t)


_X_R, _X_A, _X_K, _X_B, _X_V, _X_BV, _X_SG, _X_G = range(8)


def _rwkv_kernel(podd_ref, peven_ref, prevodd_ref, preveven_ref, mu_ref, w0_ref, wup_ref, a0_ref,
                 aup_ref, kk_ref, ka_ref, rk_ref, lnw_ref, lnb_ref, bd_ref, tri_ref, y_ref,
                 s_ref, xa_ref, xb_ref):
    C, W, N, H = RWKV_CHUNK, RWKV_WIDTH, HEAD_DIM, RWKV_HEADS
    HG = RWKV_HEAD_GROUP
    GW = HG * N
    n = pl.program_id(1)
    n_chunks = 2 * (pl.num_programs(1) - 1)
    bt = podd_ref.shape[0]

    @pl.when(n == 0)
    def _():
        s_ref[...] = jnp.zeros_like(s_ref)
        xa_ref[...] = jnp.zeros_like(xa_ref)
        xb_ref[...] = jnp.zeros_like(xb_ref)

    bd = bd_ref[...]
    head_sum = lambda a: jnp.concatenate(
        [_dot(_bf(a[:, g * GW:(g + 1) * GW]), bd) for g in range(H // HG)], axis=1)
    row = lax.broadcasted_iota(jnp.int32, (C, 1), 0)

    def prepare(b, chunk, p_ref, prev_ref, x_ref):
        p = p_ref[b]
        prev_last = jnp.where(chunk <= 0, 0.0, prev_ref[b][7:8, :])
        prev = jnp.where(row == 0, prev_last, pltpu.roll(p, 1, axis=0))
        pf = p + mu_ref[...] * (prev - p)
        r = pf[:, 0:W]
        k = pf[:, W:2 * W]
        x_ref[b, _X_R] = r
        x_ref[b, _X_V] = pf[:, 2 * W:3 * W]
        x_ref[b, _X_SG] = _silu(pf[:, 3 * W:4 * W])
        wd = _bf(jnp.tanh(pf[:, 4 * W:4 * W + LORA]))
        ad = _bf(pf[:, 4 * W + LORA:4 * W + 2 * LORA])
        yield
        z = w0_ref[...] + _dot(wd, wup_ref[...])
        logw = -DECAY_SCALE * _sigmoid(z)
        eta = _sigmoid(a0_ref[...] + _dot(ad, aup_ref[...]))
        kk = k * kk_ref[...]
        k2 = k * (eta * ka_ref[...] + (1.0 - ka_ref[...]))
        x_ref[b, _X_K] = k2
        kk_sq = _bf(kk * kk)
        rk2 = _bf(r * k2 * rk_ref[...])
        logw_parts = _split_bf16(logw, 3)
        yield
        kk = kk * lax.rsqrt(jnp.maximum(head_sum(kk_sq), 1e-24))
        x_ref[b, _X_A] = kk
        x_ref[b, _X_B] = kk * eta
        x_ref[b, _X_BV] = head_sum(rk2) * x_ref[b, _X_V]
        tri = tri_ref[...]
        cs = sum(_dot(tri, part) for part in logw_parts)
        mid = cs[C // 2 - 1:C // 2, :]
        csm = cs - mid
        end = cs[C - 1:C, :]
        x_ref[b, _X_G] = jnp.concatenate([jnp.exp(mid), jnp.exp(end), jnp.exp(end - mid),
                                          jnp.zeros((C - 3, W), F32)], axis=0)
        yield
        x_ref[b, _X_R] = x_ref[b, _X_R] * jnp.exp(csm)
        x_ref[b, _X_A] = -x_ref[b, _X_A] * jnp.exp(csm - logw)
        yield
        e_out = jnp.exp(-csm)
        x_ref[b, _X_K] = x_ref[b, _X_K] * e_out
        x_ref[b, _X_B] = x_ref[b, _X_B] * e_out

    li = lax.broadcasted_iota(jnp.int32, (C, GW), 1) % N
    ti = lax.broadcasted_iota(jnp.int32, (C, GW), 0)
    strict, incl = ti > li, ti >= li
    eye = (ti == li).astype(F32)
    same_head = (lax.broadcasted_iota(jnp.int32, (GW, GW), 0) // N
                 == lax.broadcasted_iota(jnp.int32, (GW, GW), 1) // N)

    tile_heads = LANES // N
    pair_head = (lax.broadcasted_iota(jnp.int32, (LANES, LANES), 0) // N
                 == lax.broadcasted_iota(jnp.int32, (LANES, LANES), 1) // N)
    zero_tile = jnp.zeros((LANES, LANES), BF16)

    def blockdiag(a):
        a = _bf(a)
        n_tiles = GW // LANES
        rows = []
        for j in range(n_tiles):
            own = jnp.concatenate([a[:, j * LANES:(j + 1) * LANES]] * tile_heads, axis=0)
            own = jnp.where(pair_head, own, jnp.zeros((), BF16))
            rows.append(jnp.concatenate([own if k == j else zero_tile for k in range(n_tiles)], axis=1))
        return jnp.concatenate(rows, axis=0)

    def advance(chunk, x_ref, y_rows, fill):
        live = (chunk >= 0) & (chunk < n_chunks)
        chains = [(b, g) for b in range(bt) for g in range(H // HG)]
        st = {}
        for c in chains:
            b, g = c
            ls = slice(g * GW, (g + 1) * GW)
            kb = jnp.concatenate([blockdiag(x_ref[b, _X_K][:, ls]), blockdiag(x_ref[b, _X_B][:, ls])],
                                 axis=0)
            ar = _bf(jnp.concatenate([x_ref[b, _X_A][:, ls], x_ref[b, _X_R][:, ls]], axis=0))
            st[c] = dict(ls=ls, kb=kb, ar=ar, gc=_dot_nt(ar, kb))
            fill()
        for c in chains:
            b, g = c
            s = st[c]
            ls, gc = s["ls"], s["gc"]
            s0 = s_ref[b, g]
            gates = x_ref[b, _X_G]
            s.update(
                s0=s0, v_bd=blockdiag(x_ref[b, _X_V][:, ls]),
                g_end=gates[1:2, ls], g_end_mid=gates[2:3, ls],
                uy0=_dot_nt(s["ar"], _bf(s0 * gates[0:1, ls])),
                a_ak=jnp.where(strict, gc[0:C, 0:GW], 0.0), a_ab=jnp.where(strict, gc[0:C, GW:2 * GW], 0.0),
                a_rk=jnp.where(incl, gc[C:2 * C, 0:GW], 0.0), a_rb=jnp.where(incl, gc[C:2 * C, GW:2 * GW], 0.0))
            fill()
        for c in chains:
            s = st[c]
            s["rhs"] = s["uy0"][0:C] + _dot(_bf(s["a_ak"]), s["v_bd"])
            s["tm"] = eye + s["a_ab"]
            s["pw"] = _dot(_bf(s["a_ab"]), blockdiag(s["a_ab"]))
            fill()
        span = 2
        while span < C:
            for c in chains:
                s = st[c]
                if 2 * span < C:
                    tp = _dot(_bf(jnp.concatenate([s["tm"], s["pw"]], axis=0)), blockdiag(s["pw"]))
                    s["tm"] = s["tm"] + tp[0:C]
                    s["pw"] = tp[C:2 * C]
                else:
                    s["tm"] = s["tm"] + _dot(_bf(s["tm"]), blockdiag(s["pw"]))
                fill()
            span *= 2
        for c in chains:
            s = st[c]
            s["u"] = _dot(_bf(s["tm"]), blockdiag(s["rhs"]))
            fill()
        for c in chains:
            s = st[c]
            s["vu"] = jnp.concatenate([s["v_bd"], blockdiag(s["u"])], axis=0)
            s["y"] = s["uy0"][C:2 * C] + _dot(_bf(jnp.concatenate([s["a_rk"], s["a_rb"]], axis=1)), s["vu"])
            fill()
        for c in chains:
            b, g = c
            s = st[c]
            ls = s["ls"]
            vu_rows = jnp.concatenate([_bf(x_ref[b, _X_V][:, ls]), _bf(s["u"])], axis=0)
            kb_rows = _bf(jnp.concatenate([x_ref[b, _X_K][:, ls], x_ref[b, _X_B][:, ls]], axis=0))
            keep = jnp.where(live, s["g_end"], 1.0)
            gain = jnp.where(live, s["g_end_mid"], 0.0)
            s_ref[b, g] = s["s0"] * keep + jnp.where(same_head, _dot_tn(vu_rows, kb_rows) * gain, 0.0)
            fill()
        ys = [jnp.concatenate([st[(b, g)]["y"] for g in range(H // HG)], axis=1) for b in range(bt)]
        means = [head_sum(y) * (1.0 / N) for y in ys]
        fill()
        ys = [y - mean for y, mean in zip(ys, means)]
        vars_ = [head_sum(jnp.square(y)) * (1.0 / N) for y in ys]
        fill()
        for b, (y, var) in enumerate(zip(ys, vars_)):
            y = y * lax.rsqrt(var + RWKV_GN_EPS) * lnw_ref[...] + lnb_ref[...] + x_ref[b, _X_BV]
            y_ref[b, y_rows] = (y * x_ref[b, _X_SG]).astype(y_ref.dtype)
            fill()

    def filler(gens):
        gens = list(gens)

        def fill():
            while gens:
                g = gens.pop(0)
                try:
                    next(g)
                    gens.append(g)
                    return
                except StopIteration:
                    pass

        def drain():
            for g in gens:
                for _ in g:
                    pass

        return fill, drain

    fill, drain = filler(prepare(b, 2 * n - 1, podd_ref, prevodd_ref, xb_ref) for b in range(bt))
    advance(2 * n - 2, xa_ref, slice(0, C), fill)
    drain()
    fill, drain = filler(prepare(b, 2 * n, peven_ref, preveven_ref, xa_ref) for b in range(bt))
    advance(2 * n - 1, xb_ref, slice(C, 2 * C), fill)
    drain()


def _rwkv_group(p_row, mu, w0, w_up, a0, a_up, k_k, k_a, r_k, ln_w, ln_b):
    B, T, _ = p_row.shape
    C, W = RWKV_CHUNK, RWKV_WIDTH
    heads = np.arange(RWKV_HEAD_GROUP * HEAD_DIM) // HEAD_DIM
    bd = jnp.asarray(heads[:, None] == heads[None, :], BF16)
    tri = jnp.asarray(np.tril(np.ones((C, C), np.float32)), BF16)
    vec = lambda a: a.reshape(1, -1)
    const = lambda shape: pl.BlockSpec(shape, lambda b, n: (0,) * len(shape))
    bt = RWKV_BATCH_TILE if B % RWKV_BATCH_TILE == 0 else 1
    n_chunks = T // C
    assert n_chunks % 2 == 0
    odd = lambda n: jnp.maximum(2 * n - 1, 0)
    even = lambda n: jnp.minimum(2 * n, n_chunks - 1)
    before = lambda c: jnp.maximum(c * (C // 8) - 1, 0)
    return pl.pallas_call(
        _rwkv_kernel,
        grid=(B // bt, n_chunks // 2 + 1),
        in_specs=[
            pl.BlockSpec((bt, C, RWKV_COLS), lambda b, n: (b, odd(n), 0)),
            pl.BlockSpec((bt, C, RWKV_COLS), lambda b, n: (b, even(n), 0)),
            pl.BlockSpec((bt, 8, RWKV_COLS), lambda b, n: (b, before(odd(n)), 0)),
            pl.BlockSpec((bt, 8, RWKV_COLS), lambda b, n: (b, before(even(n)), 0)),
            const((1, RWKV_COLS)), const((1, W)), const((LORA, W)), const((1, W)), const((LORA, W)),
            const((1, W)), const((1, W)), const((1, W)), const((1, W)), const((1, W)),
            const(bd.shape), const((C, C)),
        ],
        out_specs=pl.BlockSpec((bt, 2 * C, W), lambda b, n: (b, jnp.maximum(n - 1, 0), 0)),
        out_shape=jax.ShapeDtypeStruct((B, T, W), BF16),
        scratch_shapes=[pltpu.VMEM((bt, RWKV_HEADS // RWKV_HEAD_GROUP) + (RWKV_HEAD_GROUP * HEAD_DIM,) * 2,
                                   F32),
                        pltpu.VMEM((bt, 8, C, W), F32), pltpu.VMEM((bt, 8, C, W), F32)],
        compiler_params=_params("parallel", "arbitrary"),
        name="rwkv7_group",
    )(p_row, p_row, p_row, p_row, vec(mu), vec(w0), _bf(w_up), vec(a0), _bf(a_up), vec(k_k), vec(k_a),
      vec(r_k), vec(ln_w), vec(ln_b), bd, tri)


def _rope_rows(x, cos, sin):
    x1, x2 = x[0:ROPE_HALF], x[ROPE_HALF:2 * ROPE_HALF]
    return jnp.concatenate([x1 * cos - x2 * sin, x2 * cos + x1 * sin, x[2 * ROPE_HALF:]], axis=0)


def _nsa_keys_kernel(kv_ref, rope_ref, kskw_ref, vt_ref, vs2_ref):
    kv = kv_ref[0].astype(F32)
    cos, sin = rope_ref[0:ROPE_HALF], rope_ref[ROPE_HALF:2 * ROPE_HALF]
    keys = jnp.concatenate([_rope_rows(kv[0:HEAD_DIM], cos, sin),
                            _rope_rows(kv[HEAD_DIM:2 * HEAD_DIM], cos, sin)], axis=0)
    kskw_ref[0] = keys.T.astype(BF16)
    vals = kv[2 * HEAD_DIM:4 * HEAD_DIM].astype(BF16)
    ones = jnp.ones((ONES_ROWS, LANES), BF16)
    for u in range(vals.shape[1] // LANES):
        blk = vals[:, u * LANES:(u + 1) * LANES]
        vt_ref[0, u] = jnp.concatenate([blk[0:HEAD_DIM], ones, blk[HEAD_DIM:2 * HEAD_DIM], ones], axis=0)
    ones2 = jnp.ones((ONES_ROWS, MXU_DEPTH), BF16)
    for u in range(vals.shape[1] // MXU_DEPTH):
        vs2_ref[0, u] = jnp.concatenate([vals[0:HEAD_DIM, u * MXU_DEPTH:(u + 1) * MXU_DEPTH], ones2], axis=0)


def _nsa_keys(pt, rope, tk):
    B, _, T = pt.shape
    return pl.pallas_call(
        _nsa_keys_kernel,
        grid=(B, T // tk),
        in_specs=[
            pl.BlockSpec((1, 4 * HEAD_DIM, tk), lambda b, i: (b, 0, i)),
            pl.BlockSpec((2 * ROPE_HALF, tk), lambda b, i: (0, i)),
        ],
        out_specs=[
            pl.BlockSpec((1, tk, LANES), lambda b, i: (b, i, 0)),
            pl.BlockSpec((1, tk // LANES, 2 * VAL_ROWS, LANES), lambda b, i: (b, i, 0, 0)),
            pl.BlockSpec((1, tk // MXU_DEPTH, VAL_ROWS, MXU_DEPTH), lambda b, i: (b, i, 0, 0)),
        ],
        out_shape=[
            jax.ShapeDtypeStruct((B, T, LANES), BF16),
            jax.ShapeDtypeStruct((B, T // LANES, 2 * VAL_ROWS, LANES), BF16),
            jax.ShapeDtypeStruct((B, T // MXU_DEPTH, VAL_ROWS, MXU_DEPTH), BF16),
        ],
        compiler_params=_params("parallel", "parallel"),
        name="nsa_keys",
    )(pt, rope)


def _nsa_compress_kernel(g_ref, wc_ref, pos_ref, w2_ref, w2t_ref, rm_ref, t_ref):
    half = CMP_BLOCK // 2
    ng = g_ref.shape[1] // half
    wc = wc_ref[...]
    m = sum(_dot(g_ref[0, pl.ds(l, ng, stride=half), :].astype(BF16), wc[l * LANES:(l + 1) * LANES])
            for l in range(half))
    pm = _dot(pos_ref[...], wc)
    pos_term = pm[0:1, 0:LANES] + pm[1:2, LANES:2 * LANES]
    pre = m[:, 0:LANES] + pltpu.roll(m[:, LANES:2 * LANES], ng - 1, axis=0) + pos_term
    act = _silu(pre).astype(BF16)
    row = lax.broadcasted_iota(jnp.int32, (ng, 1), 0)
    col = lax.broadcasted_iota(jnp.int32, (1, ng), 1)
    rm_ref[0] = jnp.where(row < ng - 1, _dot(act, w2_ref[...]), 0.0).astype(BF16)
    vt = jnp.where(col < ng - 1, _dot_nt(w2t_ref[...], act)[HEAD_DIM:2 * HEAD_DIM], 0.0)
    t_ref[0] = jnp.concatenate([vt, jnp.ones((ONES_ROWS, ng), F32)], axis=0).astype(BF16)


def _nsa_compress(kcvc, cmp_pos, k_w1, k_w2, v_w1, v_w2):
    B, T, _ = kcvc.shape
    ng = T // CMP_STRIDE
    half = CMP_BLOCK // 2
    D = HEAD_DIM

    def spread(w, second, is_v):
        blk = w[second * half * D:(second + 1) * half * D].reshape(half, D, D)
        z = jnp.zeros_like(blk)
        return jnp.concatenate([z, blk] if is_v else [blk, z], axis=1).reshape(half * LANES, D)

    wc = jnp.concatenate([spread(k_w1, 0, False), spread(v_w1, 0, True),
                          spread(k_w1, 1, False), spread(v_w1, 1, True)], axis=1).astype(BF16)
    pos2 = jnp.concatenate([cmp_pos, cmp_pos], axis=1)
    pos = jnp.zeros((8, half * LANES), F32)
    pos = pos.at[0].set(pos2[:half].reshape(-1)).at[1].set(pos2[half:].reshape(-1)).astype(BF16)
    z = jnp.zeros((D, D), F32)
    w2 = jnp.block([[k_w2, z], [z, v_w2]])
    const = lambda shape: pl.BlockSpec(shape, lambda b: (0,) * len(shape))
    return pl.pallas_call(
        _nsa_compress_kernel,
        grid=(B,),
        in_specs=[
            pl.BlockSpec((1, T, LANES), lambda b: (b, 0, 0)),
            const((half * LANES, 2 * LANES)), const((8, half * LANES)),
            const((LANES, LANES)), const((LANES, LANES)),
        ],
        out_specs=[
            pl.BlockSpec((1, ng, LANES), lambda b: (b, 0, 0)),
            pl.BlockSpec((1, VAL_ROWS, ng), lambda b: (b, 0, 0)),
        ],
        out_shape=[
            jax.ShapeDtypeStruct((B, ng, LANES), BF16),
            jax.ShapeDtypeStruct((B, VAL_ROWS, ng), BF16),
        ],
        compiler_params=_params("parallel"),
        name="nsa_compress",
    )(kcvc, wc, pos, w2.astype(BF16), w2.T.astype(BF16))


def _tile_heads(x):
    return jnp.concatenate([x] * NSA_HEADS, axis=1)


def _nsa_attn_kernel(q_ref, gate_ref, glog_ref, rope_ref, rm_ref, ct_ref, kskw_ref, vt_ref, vs2_ref,
                     ov_ref, oh_ref, gb_ref, og_ref, y_ref, m_ref, acc_ref, sa_ref, sb_ref, ma_ref,
                     mb_ref):
    D, Q, Hn = HEAD_DIM, Q_BLOCK, NSA_HEADS
    KC = SEL_KEY_CHUNK
    ng = rm_ref.shape[1]
    ns = ov_ref.shape[0]
    n_top = min(SEL_TOPK, ns)
    i = pl.program_id(1)
    t0 = i * Q
    tq = t0 + lax.broadcasted_iota(jnp.int32, (1, Q), 1)

    q = q_ref[0].astype(F32) * (D ** -0.5 * LOG2E)
    cos, sin = rope_ref[0:ROPE_HALF], rope_ref[ROPE_HALF:2 * ROPE_HALF]
    qh = [q[h * D:(h + 1) * D] for h in range(Hn)]
    q4 = jnp.concatenate(qh, axis=1)
    q4r = jnp.concatenate([_rope_rows(x, cos, sin) for x in qh], axis=1)
    zero = jnp.zeros_like(q4)
    q_lo = jnp.concatenate([q4, zero], axis=0).astype(BF16)
    qr_lo = jnp.concatenate([q4r, zero], axis=0).astype(BF16)
    qr_hi = jnp.concatenate([zero, q4r], axis=0).astype(BF16)

    def masked(s, bias):
        return jnp.concatenate([s[:, h * Q:(h + 1) * Q] + bias for h in range(Hn)], axis=1)


    WK = WINDOW + Q
    w0 = pl.multiple_of(jnp.maximum(t0 - WINDOW, 0), LANES)
    diff = tq - (w0 + lax.broadcasted_iota(jnp.int32, (WK, 1), 0))
    w_bias = jnp.where((diff >= 0) & (diff < WINDOW), 0.0, NEG_INF)

    o_w = []

    def window_stages():
        ss = []
        for h in range(Hn):
            ss.append(_dot(kskw_ref[0, pl.ds(w0, WK), :], qr_hi[:, h * Q:(h + 1) * Q]) + w_bias)
            yield
        ebs = []
        for s in ss:
            ebs.append(jnp.exp2(s - jnp.max(s, axis=0, keepdims=True)).astype(BF16))
            yield
        for eb in ebs:
            ol = _dot(vt_ref[0, w0 // LANES][VAL_ROWS:2 * VAL_ROWS], eb[0:LANES])
            for u in range(1, WK // LANES):
                ol = ol + _dot(vt_ref[0, w0 // LANES + u][VAL_ROWS:2 * VAL_ROWS], eb[u * LANES:(u + 1) * LANES])
            o_w.append(ol[0:D] / jnp.maximum(ol[D:D + 1], 1e-30))
            yield

    cend = lax.broadcasted_iota(jnp.int32, (ng, 1), 0) * CMP_STRIDE + (CMP_BLOCK - 1)
    s = masked(_dot(rm_ref[0], q_lo), jnp.where(cend <= tq, 0.0, NEG_INF))
    eb = jnp.exp2(s - jnp.max(s, axis=0, keepdims=True)).astype(BF16)
    ol = _dot(ct_ref[0], eb)
    seen = _tile_heads((tq >= CMP_BLOCK - 1).astype(F32))
    inv = seen / jnp.maximum(ol[D:D + 1], 1e-30)
    o_c = ol[0:D] * inv
    psum = sum(eb[:, h * Q:(h + 1) * Q] * inv[:, h * Q:(h + 1) * Q] for h in range(Hn))
    imp = _dot(ov_ref[...], psum.astype(BF16))

    blk = lax.broadcasted_iota(jnp.int32, (ns, Q), 0)
    cur = tq // SEL_BLOCK
    forced = (blk == 0) | (blk == cur) | (blk == cur - 1)
    taken = -2.0
    score = jnp.where(forced, taken, jnp.where(blk <= cur, imp, -1.0))
    window = window_stages()
    for _ in range(max(n_top - 3, 0)):
        best = jnp.max(score, axis=0, keepdims=True)
        first = jnp.min(jnp.where(score == best, blk, ns), axis=0, keepdims=True)
        score = jnp.where(blk == first, taken, score)
        next(window, None)
    for _ in window:
        pass
    o_w = jnp.concatenate(o_w, axis=1)

    def with_block_mask(keep):
        bias = _tile_heads(jnp.where(keep, 0.0, NEG_INF))
        pad = oh_ref.shape[1] - ns
        if pad:
            bias = jnp.concatenate([bias, jnp.zeros((pad, Hn * Q), F32)], axis=0)
        return jnp.concatenate([qr_lo, bias.astype(BF16)], axis=0)

    chosen = (score == taken) & (blk <= cur)
    q_own = with_block_mask(chosen)
    q_sel = with_block_mask(chosen & (blk < t0 // SEL_BLOCK))

    def sel_scores(j, s_ref, cmax_ref):
        k0 = pl.multiple_of(j * KC, KC)
        keys = jnp.concatenate([kskw_ref[0, pl.ds(k0, KC), :], oh_ref[pl.ds(k0, KC), :]], axis=1)
        s = _dot(keys, q_sel)
        s_ref[...] = s
        cmax_ref[...] = jnp.max(s, axis=0, keepdims=True)

    UB = KC // MXU_DEPTH

    def sel_update(j, s_ref, cmax_ref):
        m_old = m_ref[...]
        m_new = jnp.maximum(m_old, cmax_ref[...])
        eb = jnp.exp2(s_ref[...] - m_new).astype(BF16)
        pv = acc_ref[...] * jnp.exp2(m_old - m_new)
        for u in range(UB):
            pv = pv + _dot(vs2_ref[0, j * UB + u], eb[u * MXU_DEPTH:(u + 1) * MXU_DEPTH])
        acc_ref[...] = pv
        m_ref[...] = m_new

    n_chunks = (t0 + KC - 1) // KC
    last = kskw_ref.shape[1] // KC - 1
    own = pl.ds(pl.multiple_of(t0, Q), Q)
    s = _dot(jnp.concatenate([kskw_ref[0, own, :], oh_ref[own, :]], axis=1), q_own)
    sel_scores(0, sa_ref, ma_ref)
    lk = lax.broadcasted_iota(jnp.int32, (Q, 1), 0)
    lq = lax.broadcasted_iota(jnp.int32, (1, Q), 1)
    s = masked(s, jnp.where(lk <= lq, 0.0, NEG_INF))
    m0 = jnp.max(s, axis=0, keepdims=True)
    m_ref[...] = m0
    e0 = jnp.exp2(s - m0).astype(BF16)
    QB = Q // MXU_DEPTH
    acc_ref[...] = sum(_dot(vs2_ref[0, i * QB + u], e0[u * MXU_DEPTH:(u + 1) * MXU_DEPTH]) for u in range(QB))

    def chunk_pair(j, carry):
        sel_scores(jnp.minimum(2 * j + 1, last), sb_ref, mb_ref)
        sel_update(2 * j, sa_ref, ma_ref)
        sel_scores(jnp.minimum(2 * j + 2, last), sa_ref, ma_ref)
        sel_update(jnp.minimum(2 * j + 1, last), sb_ref, mb_ref)
        return carry

    lax.fori_loop(0, (n_chunks + 1) // 2, chunk_pair, 0)
    acc = acc_ref[...]
    o_s = acc[0:D] / jnp.maximum(acc[D:D + 1], 1e-30)

    gl = _sigmoid(glog_ref[0].astype(F32) + gb_ref[...])
    ys = []
    for h in range(Hn):
        cs = slice(h * Q, (h + 1) * Q)
        o = (gl[3 * h:3 * h + 1] * o_c[:, cs] + gl[3 * h + 1:3 * h + 2] * o_s[:, cs]
             + gl[3 * h + 2:3 * h + 3] * o_w[:, cs])
        o = o * lax.rsqrt(jnp.mean(o * o, axis=0, keepdims=True) + NORM_EPS)
        ys.append(o)
    y = jnp.concatenate(ys, axis=0) * og_ref[...] * _silu(gate_ref[0].astype(F32))
    y_ref[0] = y.T.astype(y_ref.dtype)


def _nsa_attention(pt, rope, cmp_rm, cmp_t, kskw, vt, vs2, gate_b, out_g):
    B, _, T = pt.shape
    ng = T // CMP_STRIDE
    ns = T // SEL_BLOCK
    Q, W = Q_BLOCK, NSA_WIDTH
    QH = Q * NSA_HEADS
    c0 = np.arange(ng)[None, :] * CMP_STRIDE
    s0 = np.arange(ns)[:, None] * SEL_BLOCK
    ov = np.clip(np.minimum(c0 + CMP_BLOCK, s0 + SEL_BLOCK) - np.maximum(c0, s0), 0, None) / CMP_BLOCK
    ov[:, ng - 1] = 0.0
    oh_lanes = -(-ns // LANES) * LANES
    onehot = (np.arange(T)[:, None] // SEL_BLOCK) == np.arange(oh_lanes)[None, :]
    gb = jnp.zeros((16, 1), F32).at[:NSA_HEADS * N_BRANCH, 0].set(gate_b)
    glog_blk = (4 * HEAD_DIM + 4 * W) // 16
    return pl.pallas_call(
        _nsa_attn_kernel,
        grid=(B, T // Q),
        in_specs=[
            pl.BlockSpec((1, W, Q), lambda b, i: (b, 1, i)),
            pl.BlockSpec((1, W, Q), lambda b, i: (b, 2, i)),
            pl.BlockSpec((1, 16, Q), lambda b, i: (b, glog_blk, i)),
            pl.BlockSpec((2 * ROPE_HALF, Q), lambda b, i: (0, i)),
            pl.BlockSpec((1, ng, LANES), lambda b, i: (b, 0, 0)),
            pl.BlockSpec((1, VAL_ROWS, ng), lambda b, i: (b, 0, 0)),
            pl.BlockSpec((1, T, LANES), lambda b, i: (b, 0, 0)),
            pl.BlockSpec((1, T // LANES, 2 * VAL_ROWS, LANES), lambda b, i: (b, 0, 0, 0)),
            pl.BlockSpec((1, T // MXU_DEPTH, VAL_ROWS, MXU_DEPTH), lambda b, i: (b, 0, 0, 0)),
            pl.BlockSpec((ns, ng), lambda b, i: (0, 0)),
            pl.BlockSpec((T, oh_lanes), lambda b, i: (0, 0)),
            pl.BlockSpec((16, 1), lambda b, i: (0, 0)),
            pl.BlockSpec((W, 1), lambda b, i: (0, 0)),
        ],
        out_specs=pl.BlockSpec((1, Q, W), lambda b, i: (b, i, 0)),
        out_shape=jax.ShapeDtypeStruct((B, T, W), BF16),
        scratch_shapes=[pltpu.VMEM((1, QH), F32), pltpu.VMEM((VAL_ROWS, QH), F32),
                        pltpu.VMEM((SEL_KEY_CHUNK, QH), F32), pltpu.VMEM((SEL_KEY_CHUNK, QH), F32),
                        pltpu.VMEM((1, QH), F32), pltpu.VMEM((1, QH), F32)],
        compiler_params=_params("parallel", "arbitrary"),
        name="nsa_attention",
    )(pt, pt, pt, rope, cmp_rm, cmp_t, kskw, vt, vs2, jnp.asarray(ov, BF16), jnp.asarray(onehot, BF16), gb,
      out_g.reshape(W, 1))


def _mem_kv_kernel(mem_ref, g_ref, w_ref, wt_ref, k_ref, vt_ref):
    x = mem_ref[0]
    h = x * lax.rsqrt(jnp.mean(x * x, axis=-1, keepdims=True) + NORM_EPS) * g_ref[...]
    hb = h.astype(BF16)
    k = _dot(hb, w_ref[...]) * (HEAD_DIM ** -0.5 * LOG2E)
    lane_head = lax.broadcasted_iota(jnp.int32, k.shape, 1) // HEAD_DIM
    for h in range(MEM_HEADS):
        k_ref[0, h] = jnp.where(lane_head == h, k, 0.0).astype(BF16)
    vt = _dot_nt(wt_ref[...], hb)
    ones = jnp.ones((ONES_ROWS, vt.shape[1]), F32)
    vt_ref[0] = jnp.concatenate(
        [part for h in range(MEM_HEADS) for part in (vt[h * HEAD_DIM:(h + 1) * HEAD_DIM], ones)],
        axis=0).astype(BF16)


def _mem_kv(mem, g, w_kv):
    B, M, D = mem.shape
    W = MEM_WIDTH
    const = lambda shape: pl.BlockSpec(shape, lambda b: (0,) * len(shape))
    return pl.pallas_call(
        _mem_kv_kernel,
        grid=(B,),
        in_specs=[pl.BlockSpec((1, M, D), lambda b: (b, 0, 0)), const((1, D)), const((D, W)),
                  const((W, D))],
        out_specs=[pl.BlockSpec((1, MEM_HEADS, M, W), lambda b: (b, 0, 0, 0)),
                   pl.BlockSpec((1, MEM_HEADS * VAL_ROWS, M), lambda b: (b, 0, 0))],
        out_shape=[jax.ShapeDtypeStruct((B, MEM_HEADS, M, W), BF16),
                   jax.ShapeDtypeStruct((B, MEM_HEADS * VAL_ROWS, M), BF16)],
        compiler_params=_params("parallel"),
        name="mem_kv",
    )(mem, g.reshape(1, D), w_kv[:, :W].astype(BF16), w_kv[:, W:].T.astype(BF16))


def _mem_attn_kernel(q_ref, gate_ref, k_ref, vt_ref, og_ref, y_ref):
    D, Hm = HEAD_DIM, MEM_HEADS
    q = q_ref[0]
    vt = vt_ref[0]
    ss = [_dot(k_ref[0, h], q) for h in range(Hm)]
    ebs = [jnp.exp2(s - jnp.max(s, axis=0, keepdims=True)).astype(BF16) for s in ss]
    ols = [_dot(vt[h * VAL_ROWS:(h + 1) * VAL_ROWS], eb) for h, eb in enumerate(ebs)]
    ys = []
    for ol in ols:
        o = ol[0:D] / ol[D:D + 1]
        ys.append(o * lax.rsqrt(jnp.mean(o * o, axis=0, keepdims=True) + NORM_EPS))
    y = jnp.concatenate(ys, axis=0) * og_ref[...] * _silu(gate_ref[0].astype(F32))
    y_ref[0] = y.T.astype(y_ref.dtype)


def _mem_attention(pt, mem_k, mem_vt, out_g, tm):
    B, _, T = pt.shape
    M = mem_k.shape[2]
    W = MEM_WIDTH
    return pl.pallas_call(
        _mem_attn_kernel,
        grid=(B, T // tm),
        in_specs=[
            pl.BlockSpec((1, W, tm), lambda b, i: (b, 3, i)),
            pl.BlockSpec((1, W, tm), lambda b, i: (b, 4, i)),
            pl.BlockSpec((1, MEM_HEADS, M, W), lambda b, i: (b, 0, 0, 0)),
            pl.BlockSpec((1, MEM_HEADS * VAL_ROWS, M), lambda b, i: (b, 0, 0)),
            pl.BlockSpec((W, 1), lambda b, i: (0, 0)),
        ],
        out_specs=pl.BlockSpec((1, tm, W), lambda b, i: (b, i, 0)),
        out_shape=jax.ShapeDtypeStruct((B, T, W), BF16),
        compiler_params=_params("parallel", "parallel"),
        name="mem_attention",
    )(pt, pt, mem_k, mem_vt, out_g.reshape(W, 1))


def _outproj_kernel(x_ref, yr_ref, yn_ref, ym_ref, wr_ref, wn_ref, wm_ref, g_ref, o_ref):
    z = (x_ref[0] + _dot(yr_ref[0], wr_ref[...]) + _dot(yn_ref[0], wn_ref[...])
         + _dot(ym_ref[0], wm_ref[...]))
    o_ref[0] = z * lax.rsqrt(jnp.mean(z * z, axis=-1, keepdims=True) + NORM_EPS) * g_ref[...]


def _output_projection(x, y_rwkv, y_nsa, y_mem, w_out, g, tm):
    B, T, D = x.shape
    wb = w_out.astype(BF16)
    w_r, w_n, w_m = wb[:RWKV_WIDTH], wb[RWKV_WIDTH:RWKV_WIDTH + NSA_WIDTH], wb[RWKV_WIDTH + NSA_WIDTH:]
    tile = lambda w: pl.BlockSpec((1, tm, w), lambda b, i: (b, i, 0))
    const = lambda shape: pl.BlockSpec(shape, lambda b, i: (0,) * len(shape))
    return pl.pallas_call(
        _outproj_kernel,
        grid=(B, T // tm),
        in_specs=[tile(D), tile(RWKV_WIDTH), tile(NSA_WIDTH), tile(MEM_WIDTH),
                  const((RWKV_WIDTH, D)), const((NSA_WIDTH, D)), const((MEM_WIDTH, D)), const((1, D))],
        out_specs=tile(D),
        out_shape=jax.ShapeDtypeStruct((B, T, D), F32),
        compiler_params=_params("parallel", "parallel"),
        name="output_projection",
    )(x, y_rwkv, y_nsa, y_mem, w_r, w_n, w_m, g.reshape(1, D))


def _rope_table(T):
    inv_freq = ROPE_THETA ** (-jnp.arange(ROPE_HALF, dtype=F32) / ROPE_HALF)
    ang = inv_freq[:, None] * jnp.arange(T).astype(F32)[None, :]
    return jnp.concatenate([jnp.cos(ang), jnp.sin(ang)], axis=0)


def _split_w_in(w):
    D = HEAD_DIM
    n0 = RWKV_COLS
    q, gate, glog = n0, n0 + NSA_WIDTH, n0 + 2 * NSA_WIDTH
    kc = glog + NSA_HEADS * N_BRANCH
    vc, ks, vs, kw, vw = kc + D, kc + 2 * D, kc + 3 * D, kc + 4 * D, kc + 5 * D
    m0 = vw + D
    cols = lambda a, n: w[:, a:a + n]
    w_row = jnp.concatenate([cols(0, n0), cols(kc, D), cols(vc, D)], axis=1)
    w_t = jnp.concatenate([cols(ks, D), cols(kw, D), cols(vs, D), cols(vw, D), cols(q, NSA_WIDTH),
                           cols(gate, NSA_WIDTH), cols(m0, MEM_WIDTH), cols(m0 + MEM_WIDTH, MEM_WIDTH),
                           cols(glog, NSA_HEADS * N_BRANCH),
                           jnp.zeros((w.shape[0], 16 - NSA_HEADS * N_BRANCH), w.dtype)], axis=1)
    return w_row.astype(BF16), w_t.T.astype(BF16)


def kernel(x, mem, norm_in_g, w_in, rwkv_mu, rwkv_w0, rwkv_w_up, rwkv_a0, rwkv_a_up, rwkv_k_k,
           rwkv_k_a, rwkv_r_k, rwkv_ln_w, rwkv_ln_b, nsa_cmp_pos, nsa_cmp_k_w1, nsa_cmp_k_w2,
           nsa_cmp_v_w1, nsa_cmp_v_w2, nsa_gate_b, nsa_out_g, mem_norm_g, w_mem_kv, mem_out_g, w_out,
           norm_final_g):
    B, T, D = x.shape
    assert w_in.shape[0] == 1, "single-layer stack: the final norm is fused into the output projection"
    rope = _rope_table(T)
    w_row, w_t = _split_w_in(w_in[0])
    p_row, kcvc, pt = _input_projection(x, norm_in_g[0].reshape(1, D), w_row, w_t, tm=512)
    y_rwkv = _rwkv_group(p_row, rwkv_mu[0], rwkv_w0[0], rwkv_w_up[0], rwkv_a0[0], rwkv_a_up[0],
                         rwkv_k_k[0], rwkv_k_a[0], rwkv_r_k[0].reshape(-1), rwkv_ln_w[0],
                         rwkv_ln_b[0])
    kskw, vt, vs2 = _nsa_keys(pt, rope, tk=2048)
    cmp_rm, cmp_t = _nsa_compress(kcvc, nsa_cmp_pos[0], nsa_cmp_k_w1[0], nsa_cmp_k_w2[0],
                                  nsa_cmp_v_w1[0], nsa_cmp_v_w2[0])
    y_nsa = _nsa_attention(pt, rope, cmp_rm, cmp_t, kskw, vt, vs2, nsa_gate_b[0], nsa_out_g[0])
    mem_k, mem_vt = _mem_kv(mem, mem_norm_g[0], w_mem_kv[0])
    y_mem = _mem_attention(pt, mem_k, mem_vt, mem_out_g[0], tm=1024)
    return _output_projection(x, y_rwkv, y_nsa, y_mem, w_out[0], norm_final_g, tm=1024)
```
